```python
import math
import jax, jax.numpy as jnp
from jax import lax
import numpy as np

D_MODEL = 4096
BATCH = 2
SEQ = 4096
DEPTH = 4

HEAD_DIM = 128
GROUP_W = D_MODEL // 4
A_HEADS = GROUP_W // HEAD_DIM
B_HEADS = GROUP_W // HEAD_DIM
C_HEADS = GROUP_W // (2 * HEAD_DIM)
D_HEADS = GROUP_W // HEAD_DIM
N_META = 16
CONV_W = 4
CHUNK = 64
Q_BLOCK = 128
ROPE_THETA = 500000.0
ROPE_DIMS = HEAD_DIM // 4
RET_THETA = 10000.0
N_GROUPS = 8
EXP_PER_GROUP = 8
N_EXPERTS = N_GROUPS * EXP_PER_GROUP
TOP_K_IN_GROUP = 2
D_EXPERT = 128
ALPHA = (2 * DEPTH) ** 0.25
BETA = (8 * DEPTH) ** -0.25
NORM_EPS = 1e-6
LN_EPS = 1e-5
IN_SPLITS = (GROUP_W, GROUP_W, GROUP_W, GROUP_W, A_HEADS, A_HEADS,
             GROUP_W, GROUP_W, GROUP_W, GROUP_W,
             GROUP_W, GROUP_W, GROUP_W,
             GROUP_W, GROUP_W, GROUP_W, GROUP_W)
P_IN = 15 * GROUP_W + 2 * A_HEADS

kernel_name = 'hybrid_headgroup_hmoe_trunk'


def _split_points(sizes):
    pts, acc = [], 0
    for s in sizes[:-1]:
        acc += s
        pts.append(acc)
    return pts


def layer_norm(x, g, b):
    xf = x.astype(jnp.float32)
    mu = jnp.mean(xf, -1, keepdims=True)
    var = jnp.mean(jnp.square(xf - mu), -1, keepdims=True)
    y = (xf - mu) * lax.rsqrt(var + LN_EPS) * g.astype(jnp.float32) + b.astype(jnp.float32)
    return y.astype(x.dtype)


def rms_norm(x, w=None):
    xf = x.astype(jnp.float32)
    y = xf * lax.rsqrt(jnp.mean(xf * xf, -1, keepdims=True) + NORM_EPS)
    return y if w is None else y * w.astype(jnp.float32)


def l2norm(x):
    return x * lax.rsqrt(jnp.sum(x * x, -1, keepdims=True) + NORM_EPS)


def causal_conv(x, w):
    T = x.shape[1]
    xp = jnp.pad(x, ((0, 0), (CONV_W - 1, 0), (0, 0)))
    return sum(xp[:, j:j + T, :] * w[:, j] for j in range(CONV_W))


def rope(x, pos, n_rot, theta):
    half = n_rot // 2
    inv = 1.0 / (theta ** (jnp.arange(half, dtype=jnp.float32) / half))
    ang = pos.astype(jnp.float32)[:, None] * inv[None, :]
    cos = jnp.cos(ang)[None, :, None, :]
    sin = jnp.sin(ang)[None, :, None, :]
    x1, x2, xp = x[..., :half], x[..., half:n_rot], x[..., n_rot:]
    return jnp.concatenate([x1 * cos - x2 * sin, x2 * cos + x1 * sin, xp], -1)


def causal_mask(n):
    i = jnp.arange(n)
    return i[:, None] >= i[None, :]


def to_chunks(x):
    B, T, H, d = x.shape
    pad = (-T) % CHUNK
    x = jnp.pad(x, ((0, 0), (pad, 0), (0, 0), (0, 0)))
    n = x.shape[1] // CHUNK
    return x.reshape(B, n, CHUNK, H, d).transpose(1, 0, 3, 2, 4)


def from_chunks(y, T):
    N, B, H, C, d = y.shape
    y = y.transpose(1, 0, 3, 2, 4).reshape(B, N * C, H, d)
    return y[:, N * C - T:]


def gated_deltanet(q, k, v, beta, g):
    B, T, H, dk = q.shape
    dv = v.shape[-1]
    q = l2norm(q) * dk ** -0.5
    k = l2norm(k)
    qc, kc, vc = to_chunks(q), to_chunks(k), to_chunks(v)
    bc = to_chunks(beta[..., None])[..., 0]
    gc = to_chunks(g[..., None])[..., 0]
    causal = causal_mask(CHUNK)
    strict = jnp.tril(jnp.ones((CHUNK, CHUNK), bool), -1)
    eye = jnp.eye(CHUNK, dtype=jnp.float32)

    def step(S, inp):
        q_, k_, v_, b_, g_ = inp
        gcum = jnp.cumsum(g_, axis=-1)
        decay = jnp.exp(jnp.where(causal, gcum[..., :, None] - gcum[..., None, :], -jnp.inf))
        kb = k_ * b_[..., None]
        lower = jnp.where(strict, jnp.einsum('bhid,bhjd->bhij', kb, k_) * decay, 0.0)
        rhs = jnp.concatenate([v_ * b_[..., None], kb * jnp.exp(gcum)[..., None]], axis=-1)
        sol = lax.linalg.triangular_solve(eye + lower, rhs, left_side=True, lower=True)
        u, w = sol[..., :dv], sol[..., dv:]
        v_new = u - jnp.einsum('bhik,bhkv->bhiv', w, S)
        attn = jnp.einsum('bhid,bhjd->bhij', q_, k_) * decay
        o = (jnp.einsum('bhik,bhkv->bhiv', q_ * jnp.exp(gcum)[..., None], S)
             + jnp.einsum('bhij,bhjv->bhiv', attn, v_new))
        g_last = gcum[..., -1:]
        S = (S * jnp.exp(g_last)[..., None]
             + jnp.einsum('bhik,bhiv->bhkv', k_ * jnp.exp(g_last - gcum)[..., None], v_new))
        return S, o

    S0 = jnp.zeros((B, H, dk, dv), jnp.float32)
    _, o = lax.scan(step, S0, (qc, kc, vc, bc, gc))
    return from_chunks(o, T)


def hgrn2_recurrence(q, k, v, logf):
    B, T, H, dk = q.shape
    dv = v.shape[-1]
    qc, kc, vc, fc = to_chunks(q), to_chunks(k), to_chunks(v), to_chunks(logf)
    causal = causal_mask(CHUNK)

    def step(S, inp):
        q_, k_, v_, lf = inp
        b = jnp.cumsum(lf, axis=2)
        rel = jnp.where(causal[:, :, None], b[:, :, :, None, :] - b[:, :, None, :, :], -jnp.inf)
        A = jnp.sum(q_[:, :, :, None, :] * k_[:, :, None, :, :] * jnp.exp(rel), -1)
        o = (jnp.einsum('bhik,bhkv->bhiv', q_ * jnp.exp(b), S)
             + jnp.einsum('bhij,bhjv->bhiv', A, v_))
        b_last = b[:, :, -1:, :]
        S = (jnp.exp(b_last)[:, :, 0, :, None] * S
             + jnp.einsum('bhjk,bhjv->bhkv', k_ * jnp.exp(b_last - b), v_))
        return S, o

    S0 = jnp.zeros((B, H, dk, dv), jnp.float32)
    _, o = lax.scan(step, S0, (qc, kc, vc, fc))
    return from_chunks(o, T)


def diff_attention(q, k, v, lam):
    B, T, H2, d = q.shape
    H = H2 // 2
    nb = -(-T // Q_BLOCK)
    Tp = nb * Q_BLOCK
    padT = ((0, 0), (0, Tp - T), (0, 0), (0, 0))
    q, k, v = jnp.pad(q, padT), jnp.pad(k, padT), jnp.pad(v, padT)
    qb = q.reshape(B, nb, Q_BLOCK, H2, d).transpose(1, 0, 3, 2, 4)
    kk = k.transpose(0, 2, 1, 3)
    vv = v.transpose(0, 2, 1, 3)
    kpos = jnp.arange(Tp)
    starts = jnp.arange(nb) * Q_BLOCK

    def block(args):
        qi, start = args
        s = jnp.einsum('bhqd,bhkd->bhqk', qi, kk) * d ** -0.5
        qpos = start + jnp.arange(Q_BLOCK)
        s = jnp.where(qpos[:, None] >= kpos[None, :], s, -jnp.inf)
        p = jax.nn.softmax(s, axis=-1).reshape(B, H, 2, Q_BLOCK, Tp)
        a = p[:, :, 0] - lam * p[:, :, 1]
        return jnp.einsum('bhqk,bhkv->bhqv', a, vv)

    o = lax.map(block, (qb, starts))
    o = o.transpose(1, 0, 3, 2, 4).reshape(B, Tp, H, 2 * d)
    return o[:, :T]


def retention(q, k, v):
    B, T, H, dk = q.shape
    dv = v.shape[-1]
    lg = jnp.log(1.0 - 2.0 ** (-5.0 - jnp.arange(H, dtype=jnp.float32)))
    idx = jnp.arange(CHUNK, dtype=jnp.float32)
    rel = idx[:, None] - idx[None, :]
    dmask = jnp.exp(jnp.where(causal_mask(CHUNK)[None], rel[None] * lg[:, None, None], -jnp.inf))
    q_decay = jnp.exp((idx[None, :] + 1.0) * lg[:, None])
    k_decay = jnp.exp((CHUNK - 1.0 - idx[None, :]) * lg[:, None])
    c_decay = jnp.exp(CHUNK * lg)[:, None, None]
    qc, kc, vc = to_chunks(q), to_chunks(k), to_chunks(v)

    def step(S, inp):
        q_, k_, v_ = inp
        A = jnp.einsum('bhid,bhjd->bhij', q_, k_) * dmask
        o = (jnp.einsum('bhij,bhjv->bhiv', A, v_)
             + jnp.einsum('bhik,bhkv->bhiv', q_ * q_decay[..., None], S))
        S = c_decay * S + jnp.einsum('bhjk,bhjv->bhkv', k_ * k_decay[..., None], v_)
        return S, o

    S0 = jnp.zeros((B, H, dk, dv), jnp.float32)
    _, o = lax.scan(step, S0, (qc, kc, vc))
    return from_chunks(o, T)


def _heads(t, n):
    return t.reshape(t.shape[0], t.shape[1], n, -1)


def hybrid_mixer(h, pos, layer, w_in, conv_a, a_log, dt_bias, norm_a, lb, norm_b,
                 lam_q1, lam_k1, lam_q2, lam_k2, subln_c, w_out):
    B, T, _ = h.shape
    f32 = jnp.float32
    z = jnp.einsum('btd,dp->btp', h, w_in).astype(f32)
    (qa, ka, va, ga, ba, aa, qb, fb, ib, gb, qc, kc, vc,
     qd, kd, vd, gd) = jnp.split(z, _split_points(IN_SPLITS), axis=-1)

    qkv = jax.nn.silu(causal_conv(jnp.concatenate([qa, ka, va], -1), conv_a.astype(f32)))
    qa, ka, va = jnp.split(qkv, 3, axis=-1)
    beta = jax.nn.sigmoid(ba)
    decay = -jnp.exp(a_log.astype(f32)) * jax.nn.softplus(aa + dt_bias.astype(f32))
    oa = gated_deltanet(_heads(qa, A_HEADS), _heads(ka, A_HEADS), _heads(va, A_HEADS), beta, decay)
    oa = rms_norm(oa, norm_a) * jax.nn.silu(_heads(ga, A_HEADS))

    lbh = lb.astype(f32).reshape(B_HEADS, HEAD_DIM)
    fgate = lbh + (1.0 - lbh) * jax.nn.sigmoid(_heads(fb, B_HEADS))
    logf = jnp.log(fgate)
    kb_ = 1.0 - fgate
    ob = hgrn2_recurrence(jax.nn.silu(_heads(qb, B_HEADS)), kb_, _heads(ib, B_HEADS), logf)
    ob = rms_norm(ob, norm_b) * jax.nn.silu(_heads(gb, B_HEADS))

    qc = rope(_heads(qc, 2 * C_HEADS), pos, ROPE_DIMS, ROPE_THETA)
    kc = rope(_heads(kc, 2 * C_HEADS), pos, ROPE_DIMS, ROPE_THETA)
    lam_init = 0.8 - 0.6 * math.exp(-0.3 * layer)
    lam = (jnp.exp(jnp.sum(lam_q1.astype(f32) * lam_k1.astype(f32)))
           - jnp.exp(jnp.sum(lam_q2.astype(f32) * lam_k2.astype(f32))) + lam_init)
    oc = diff_attention(qc, kc, _heads(vc, C_HEADS), lam)
    oc = rms_norm(oc, subln_c) * (1.0 - lam_init)

    qd = rope(_heads(qd, D_HEADS), pos, HEAD_DIM, RET_THETA)
    kd = rope(_heads(kd, D_HEADS), pos, HEAD_DIM, RET_THETA) * HEAD_DIM ** -0.5
    od = retention(qd, kd, _heads(vd, D_HEADS))
    od = rms_norm(od) * jax.nn.silu(_heads(gd, D_HEADS))

    o = jnp.concatenate([oa.reshape(B, T, GROUP_W), ob.reshape(B, T, GROUP_W),
                         oc.reshape(B, T, GROUP_W), od.reshape(B, T, GROUP_W)], -1).astype(h.dtype)
    return jnp.einsum('btc,cd->btd', o, w_out)


def hier_moe(h, w_rg, b_rg, w_re, b_re, w_gate, w_up, w_down):
    B, T, D = h.shape
    f32 = jnp.float32
    xt = h.reshape(B * T, D)
    pg = jax.nn.softmax((xt @ w_rg + b_rg).astype(f32), axis=-1)
    g_w, g_idx = lax.top_k(pg, 1)
    le = (xt @ w_re + b_re).astype(f32).reshape(-1, N_GROUPS, EXP_PER_GROUP)
    le_sel = jnp.einsum('nge,ng->ne', le, jax.nn.one_hot(g_idx[:, 0], N_GROUPS, dtype=f32))
    pe = jax.nn.softmax(le_sel, axis=-1)
    e_w, e_idx = lax.top_k(pe, TOP_K_IN_GROUP)
    e_w = e_w / jnp.sum(e_w, -1, keepdims=True)
    gid = g_idx * EXP_PER_GROUP + e_idx
    gates = jnp.sum(jax.nn.one_hot(gid, N_EXPERTS, dtype=f32) * (g_w * e_w)[..., None], axis=1)
    hg = jnp.einsum('nd,edf->nef', xt, w_gate)
    hu = jnp.einsum('nd,edf->nef', xt, w_up)
    act = jax.nn.silu(hg) * hu * gates[..., None].astype(hg.dtype)
    y = jnp.einsum('nef,efd->nd', act, w_down)
    return y.reshape(B, T, D)


def setup_inputs(seed: int = 0) -> dict:
    key = jax.random.key(seed)
    ks = jax.random.split(key, 32)
    f32 = jnp.float32
    L, D, W, d = DEPTH, D_MODEL, GROUP_W, HEAD_DIM

    def nrm(k, shape, s):
        return jax.random.normal(k, shape, f32) * s

    dt = jnp.exp(jax.random.uniform(ks[5], (L, A_HEADS), f32, math.log(1e-3), math.log(1e-1)))
    return {
        'x': nrm(ks[0], (BATCH, SEQ, D), 1.0),
        'meta_tokens': nrm(ks[1], (N_META, D), 1.0),
        'emb_ln_g': 1.0 + nrm(ks[2], (D,), 0.02),
        'emb_ln_b': nrm(ks[3], (D,), 0.02),
        'w_in': nrm(ks[4], (L, D, P_IN), D ** -0.5),
        'conv_a': nrm(ks[6], (L, 3 * W, CONV_W), CONV_W ** -0.5),
        'a_log': jnp.log(jax.random.uniform(ks[7], (L, A_HEADS), f32, 1.0, 16.0)),
        'dt_bias': dt + jnp.log(-jnp.expm1(-dt)),
        'norm_a': 1.0 + nrm(ks[8], (L, d), 0.02),
        'hgrn_lb': nrm(ks[9], (L, W), 0.1),
        'norm_b': 1.0 + nrm(ks[10], (L, d), 0.02),
        'lam_q1': nrm(ks[11], (L, d), 0.1),
        'lam_k1': nrm(ks[12], (L, d), 0.1),
        'lam_q2': nrm(ks[13], (L, d), 0.1),
        'lam_k2': nrm(ks[14], (L, d), 0.1),
        'subln_c': 1.0 + nrm(ks[15], (L, 2 * d), 0.02),
        'w_out': nrm(ks[16], (L, D, D), BETA * D ** -0.5),
        'ln1_g': 1.0 + nrm(ks[17], (L, D), 0.02),
        'ln1_b': nrm(ks[18], (L, D), 0.02),
        'w_rg': nrm(ks[19], (L, D, N_GROUPS), D ** -0.5),
        'b_rg': nrm(ks[20], (L, N_GROUPS), 0.01),
        'w_re': nrm(ks[21], (L, D, N_EXPERTS), D ** -0.5),
        'b_re': nrm(ks[22], (L, N_EXPERTS), 0.01),
        'w_gate': nrm(ks[23], (L, N_EXPERTS, D, D_EXPERT), D ** -0.5),
        'w_up': nrm(ks[24], (L, N_EXPERTS, D, D_EXPERT), D ** -0.5),
        'w_down': nrm(ks[25], (L, N_EXPERTS, D_EXPERT, D), BETA * D_EXPERT ** -0.5),
        'ln2_g': 1.0 + nrm(ks[26], (L, D), 0.02),
        'ln2_b': nrm(ks[27], (L, D), 0.02),
    }


def reference(x, meta_tokens, emb_ln_g, emb_ln_b, w_in, conv_a, a_log, dt_bias, norm_a,
              hgrn_lb, norm_b, lam_q1, lam_k1, lam_q2, lam_k2, subln_c, w_out, ln1_g, ln1_b,
              w_rg, b_rg, w_re, b_re, w_gate, w_up, w_down, ln2_g, ln2_b):
    B = x.shape[0]
    meta = jnp.broadcast_to(meta_tokens[None].astype(x.dtype), (B, N_META, D_MODEL))
    h = layer_norm(jnp.concatenate([meta, x], axis=1), emb_ln_g, emb_ln_b)
    T = h.shape[1]
    pos = jnp.arange(T)
    lbs = jax.nn.softmax(hgrn_lb.astype(jnp.float32), axis=0)
    lbs = jnp.cumsum(lbs, axis=0) - lbs[0:1]
    for l in range(DEPTH):
        y = hybrid_mixer(h, pos, l, w_in[l], conv_a[l], a_log[l], dt_bias[l], norm_a[l], lbs[l],
                         norm_b[l], lam_q1[l], lam_k1[l], lam_q2[l], lam_k2[l], subln_c[l], w_out[l])
        h = layer_norm(ALPHA * h + y, ln1_g[l], ln1_b[l])
        y = hier_moe(h, w_rg[l], b_rg[l], w_re[l], b_re[l], w_gate[l], w_up[l], w_down[l])
        h = layer_norm(ALPHA * h + y, ln2_g[l], ln2_b[l])
    return h[:, N_META:]
```

```python
import functools
import math

import jax
import jax.numpy as jnp
from jax import lax
from jax.experimental import pallas as pl
from jax.experimental.pallas import tpu as pltpu

HEAD_DIM = 128
CHUNK = 64
SUB = 16
SUB_SHIFT = 4
GROUP_SHIFT = 3
ROPE_THETA = 500000.0
ROPE_DIMS = HEAD_DIM // 4
RET_THETA = 10000.0
N_GROUPS = 8
EXP_PER_GROUP = 8
NORM_EPS = 1e-6
LN_EPS = 1e-5
LANE = 128
ROW_ALIGN = 2 * CHUNK
V7X_VMEM_LIMIT_BYTES = 56 * 1024 * 1024
NEG = -1e30

F32 = jnp.float32
BF16 = jnp.bfloat16
HI = lax.Precision.HIGHEST


def _cparams(sem):
    return pltpu.CompilerParams(dimension_semantics=sem, vmem_limit_bytes=V7X_VMEM_LIMIT_BYTES)


def _divisor_tile(n, cap, align):
    best = None
    for t in range(align, min(n, cap) + 1, align):
        if n % t == 0:
            best = t
    assert best is not None, (n, cap, align)
    return best


def _dot(a, b):
    return jnp.dot(a, b, preferred_element_type=F32)


def _dot_hi(a, b):
    return jnp.dot(a, b, preferred_element_type=F32, precision=HI)


def _dot_nt(a, b):
    return lax.dot_general(a, b, (((1,), (1,)), ((), ())), preferred_element_type=F32)


def _dot_tn(a, b):
    return lax.dot_general(a, b, (((0,), (0,)), ((), ())), preferred_element_type=F32)


def _sigmoid(x):
    return 1.0 / (1.0 + jnp.exp(-x))


def _silu(x):
    return x * _sigmoid(x)


def _softplus(x):
    return jnp.maximum(x, 0.0) + jnp.log(1.0 + jnp.exp(-jnp.abs(x)))


def _valid_rows(row0, n, front, t_valid):
    rb = row0 + lax.broadcasted_iota(jnp.int32, (n, 1), 0)
    return (rb >= front) & (rb < front + t_valid)


def _ln_kernel(*refs, alpha, n_y, front, t_valid, tpp, tm):
    h_ref = refs[0]
    y_refs = refs[1:1 + n_y]
    g_ref, b_ref, o_ref, ob_ref = refs[1 + n_y:]
    x = h_ref[...]
    if n_y:
        x = alpha * x
        for y_ref in y_refs:
            x = x + y_ref[...]
    mu = jnp.mean(x, axis=-1, keepdims=True)
    xc = x - mu
    var = jnp.mean(xc * xc, axis=-1, keepdims=True)
    y = xc * lax.rsqrt(var + LN_EPS) * g_ref[...] + b_ref[...]
    row0 = (pl.program_id(0) * tm) % tpp
    y = jnp.where(_valid_rows(row0, tm, front, t_valid), y, 0.0)
    o_ref[...] = y
    ob_ref[...] = y.astype(BF16)


def _add_ln(h, ys, g, b, alpha, lay):
    front, t_valid, tpp = lay
    n, d = h.shape
    tm = _divisor_tile(tpp, 128, 16)
    spec = pl.BlockSpec((tm, d), lambda i: (i, 0))
    vec = pl.BlockSpec((1, d), lambda i: (0, 0))
    kern = functools.partial(_ln_kernel, alpha=alpha, n_y=len(ys), front=front, t_valid=t_valid, tpp=tpp, tm=tm)
    return pl.pallas_call(
        kern,
        out_shape=(jax.ShapeDtypeStruct((n, d), F32), jax.ShapeDtypeStruct((n, d), BF16)),
        grid=(n // tm,),
        in_specs=[spec] * (1 + len(ys)) + [vec, vec],
        out_specs=(spec, spec),
        compiler_params=_cparams(("parallel",)),
        name="add_ln",
    )(h, *ys, g.reshape(1, d), b.reshape(1, d))


def _mm_kernel(*refs, ks):
    a_refs = refs[:len(ks)]
    w_ref, o_ref = refs[len(ks):]
    acc = None
    off = 0
    for a_ref, k in zip(a_refs, ks):
        p = _dot(a_ref[...], w_ref[off:off + k, :])
        acc = p if acc is None else acc + p
        off += k
    o_ref[...] = acc.astype(o_ref.dtype)


def _matmul(a_list, w, out_dtype, tm_cap, tn_cap):
    n = a_list[0].shape[0]
    ks = tuple(a.shape[1] for a in a_list)
    kdim, ndim = w.shape
    assert sum(ks) == kdim
    tm = _divisor_tile(n, tm_cap, 16)
    tn = _divisor_tile(ndim, tn_cap, LANE)
    in_specs = [pl.BlockSpec((tm, k), lambda j, i: (i, 0)) for k in ks]
    in_specs.append(pl.BlockSpec((kdim, tn), lambda j, i: (0, j)))
    return pl.pallas_call(
        functools.partial(_mm_kernel, ks=ks),
        out_shape=jax.ShapeDtypeStruct((n, ndim), out_dtype),
        grid=(ndim // tn, n // tm),
        in_specs=in_specs,
        out_specs=pl.BlockSpec((tm, tn), lambda j, i: (i, j)),
        compiler_params=_cparams(("parallel", "parallel")),
        name="matmul",
    )(*a_list, w)


def _tri_masks(n):
    r = lax.broadcasted_iota(jnp.int32, (n, n), 0)
    c = lax.broadcasted_iota(jnp.int32, (n, n), 1)
    return r, c


def _gated_rms(o, w_row, gate):
    o = o * lax.rsqrt(jnp.mean(o * o, axis=-1, keepdims=True) + NORM_EPS)
    if w_row is not None:
        o = o * w_row
    return o * _silu(gate)


def _mixer_a_kernel(q_ref, k_ref, v_ref, gt_ref, sm_ref, cw_ref, alog_ref, dtb_ref, nw_ref, o_ref,
                    s_ref, prev_ref, *, hps, n_heads, front, t_valid):
    c = pl.program_id(2)

    @pl.when(c == 0)
    def _():
        s_ref[...] = jnp.zeros_like(s_ref)
        prev_ref[...] = jnp.zeros_like(prev_ref)

    row = lax.broadcasted_iota(jnp.int32, (CHUNK, 1), 0)
    valid = _valid_rows(c * CHUNK, CHUNK, front, t_valid).astype(F32)

    def conv_silu(x_ref, p):
        cur = x_ref[...]
        prev = prev_ref[p]
        acc = cur * cw_ref[p, 3:4, :]
        for s in (1, 2, 3):
            sh = jnp.where(row >= s, pltpu.roll(cur, s, 0), pltpu.roll(prev, s, 0))
            acc = acc + sh * cw_ref[p, 3 - s:4 - s, :]
        prev_ref[p] = cur
        return _silu(acc)

    q_all = conv_silu(q_ref, 0)
    k_all = conv_silu(k_ref, 1) * valid
    v_all = conv_silu(v_ref, 2) * valid

    sm = sm_ref[...]
    beta_all = _sigmoid(sm) * valid
    g_all = -jnp.exp(alog_ref[...]) * _softplus(sm + dtb_ref[...]) * valid
    lane = lax.broadcasted_iota(jnp.int32, (CHUNK, LANE), 1)

    r, cc = _tri_masks(CHUNK)
    causal = r >= cc
    strict = r > cc
    ltri = causal.astype(F32)
    utri = (r <= cc).astype(F32)
    ones = jnp.ones((CHUNK, CHUNK), F32)
    eye = (r == cc).astype(F32)

    for h in range(hps):
        hg = pl.program_id(1) * hps + h
        hs = slice(h * HEAD_DIM, (h + 1) * HEAD_DIM)
        beta = jnp.sum(jnp.where(lane == hg, beta_all, 0.0), axis=-1, keepdims=True)
        g = jnp.sum(jnp.where(lane == n_heads + hg, g_all, 0.0), axis=-1, keepdims=True)
        gb = jnp.broadcast_to(g, (CHUNK, LANE))
        gcum = _dot_hi(ltri, gb)
        gcum_row = _dot_hi(ones, gb[:, :CHUNK] * utri)
        decay = jnp.where(causal, jnp.exp(jnp.minimum(gcum[:, :CHUNK] - gcum_row, 0.0)), 0.0)

        q = q_all[:, hs]
        k = k_all[:, hs]
        v = v_all[:, hs]
        q = q * lax.rsqrt(jnp.sum(q * q, axis=-1, keepdims=True) + NORM_EPS) * HEAD_DIM ** -0.5
        k = k * lax.rsqrt(jnp.sum(k * k, axis=-1, keepdims=True) + NORM_EPS)
        kb = k * beta
        m = jnp.where(strict, -(_dot_nt(kb, k) * decay), 0.0)
        inv = eye + m
        mp = m
        for _ in range(5):
            mp = _dot(mp, mp)
            inv = inv + _dot(inv, mp)
        eg = jnp.exp(gcum)
        sol = _dot(inv, jnp.concatenate([v * beta, kb * eg], axis=1))
        u = sol[:, :HEAD_DIM]
        w = sol[:, HEAD_DIM:]
        s = s_ref[h]
        v_new = u - _dot(w, s)
        attn = _dot_nt(q, k) * decay
        o = _dot(q * eg, s) + _dot(attn, v_new)
        g_last = gcum[CHUNK - 1:CHUNK, :]
        s_ref[h] = s * jnp.exp(g_last) + _dot_tn(k * jnp.exp(g_last - gcum), v_new)
        o_ref[:, hs] = _gated_rms(o, nw_ref[...], gt_ref[:, hs]).astype(o_ref.dtype)


def _mixer_b_kernel(q_ref, f_ref, i_ref, gt_ref, lb_ref, nw_ref, o_ref, st_ref, kbuf, bbuf, vbuf, obuf,
                    *, hps, front, t_valid):
    c = pl.program_id(2)

    @pl.when(c == 0)
    def _():
        st_ref[...] = jnp.zeros_like(st_ref)

    valid = _valid_rows(c * CHUNK, CHUNK, front, t_valid).astype(F32)
    r, cc = _tri_masks(CHUNK)
    lblk = ((r >= cc) & ((r >> SUB_SHIFT) == (cc >> SUB_SHIFT))).astype(F32)
    row16 = lax.broadcasted_iota(jnp.int32, (SUB, 1), 0)

    for h in range(hps):
        hs = slice(h * HEAD_DIM, (h + 1) * HEAD_DIM)
        lb = lb_ref[:, hs]
        f = lb + (1.0 - lb) * _sigmoid(f_ref[:, hs])
        logf = jnp.log(f) * valid
        kk = (1.0 - f) * valid
        q = _silu(q_ref[:, hs])
        v = i_ref[:, hs]
        bl = _dot_hi(lblk, logf)
        kbuf[...] = kk
        bbuf[...] = bl
        vbuf[...] = v
        st = st_ref[h]
        for blk in range(CHUNK // SUB):
            rs = slice(blk * SUB, (blk + 1) * SUB)
            b16 = bl[rs]
            q16 = q[rs]
            b_end = b16[SUB - 1:SUB, :]
            acc = _dot_nt(q16 * jnp.exp(b16), st)
            for j in range(SUB):
                jr = pl.ds(blk * SUB + j, 1)
                e = jnp.exp(jnp.minimum(b16 - bbuf[jr, :], 0.0))
                sc = jnp.sum(q16 * kbuf[jr, :] * e, axis=-1, keepdims=True)
                acc = acc + jnp.where(row16 >= j, sc, 0.0) * vbuf[jr, :]
            st = st * jnp.exp(b_end) + _dot_tn(v[rs], kk[rs] * jnp.exp(b_end - b16))
            obuf[rs, :] = acc
        st_ref[h] = st
        o_ref[:, hs] = _gated_rms(obuf[...], nw_ref[...], gt_ref[:, hs]).astype(o_ref.dtype)


def _rope_partial(x, cos, sin_lo, sin_hi):
    half = ROPE_DIMS // 2
    return x * cos + pltpu.roll(x, LANE - half, 1) * sin_lo + pltpu.roll(x, half, 1) * sin_hi


def _mixer_c_kernel(q_ref, k_ref, v_ref, cq_ref, slq_ref, shq_ref, ck_ref, slk_ref, shk_ref, lam_ref, nw_ref,
                    o_ref, qs_ref, m_ref, l_ref, acc_ref, *, tq, front, out_scale):
    i = pl.program_id(2)
    j = pl.program_id(3)

    @pl.when(j == 0)
    def _():
        for mp in range(2):
            ms = slice(mp * HEAD_DIM, (mp + 1) * HEAD_DIM)
            qs_ref[mp] = _rope_partial(q_ref[:, ms], cq_ref[...], slq_ref[...], shq_ref[...]) * HEAD_DIM ** -0.5
        m_ref[...] = jnp.full_like(m_ref, NEG)
        l_ref[...] = jnp.zeros_like(l_ref)
        acc_ref[...] = jnp.zeros_like(acc_ref)

    @pl.when(j <= i)
    def _():
        rq = i * tq + lax.broadcasted_iota(jnp.int32, (tq, 1), 0)
        rk = j * tq + lax.broadcasted_iota(jnp.int32, (1, tq), 1)
        msk = (rk <= rq) & (rk >= front)
        v = v_ref[...]
        for mp in range(2):
            ms = slice(mp * HEAD_DIM, (mp + 1) * HEAD_DIM)
            km = _rope_partial(k_ref[:, ms], ck_ref[...], slk_ref[...], shk_ref[...])
            s = jnp.where(msk, _dot_nt(qs_ref[mp], km), NEG)
            m_prev = m_ref[mp]
            m_new = jnp.maximum(m_prev, jnp.max(s, axis=-1, keepdims=True))
            p = jnp.exp(s - m_new)
            a = jnp.exp(m_prev - m_new)
            l_ref[mp] = a * l_ref[mp] + jnp.sum(p, axis=-1, keepdims=True)
            acc_ref[mp] = a * acc_ref[mp] + _dot(p, v)
            m_ref[mp] = m_new

    @pl.when(j == i)
    def _():
        o = acc_ref[0] / l_ref[0] - lam_ref[...] * (acc_ref[1] / l_ref[1])
        o = o * lax.rsqrt(jnp.mean(o * o, axis=-1, keepdims=True) + NORM_EPS) * nw_ref[...] * out_scale
        o_ref[...] = o.astype(o_ref.dtype)


def _mixer_d_kernel(q_ref, k_ref, v_ref, gt_ref, cos_ref, sin_ref, dm_ref, qd_ref, kd_ref, cd_ref, o_ref, s_ref, *, hps):
    c = pl.program_id(2)

    @pl.when(c == 0)
    def _():
        s_ref[...] = jnp.zeros_like(s_ref)

    cos = cos_ref[...]
    sin = sin_ref[...]
    for h in range(hps):
        hs = slice(h * HEAD_DIM, (h + 1) * HEAD_DIM)
        q = q_ref[:, hs]
        k = k_ref[:, hs]
        v = v_ref[:, hs]
        q = q * cos + pltpu.roll(q, HEAD_DIM // 2, 1) * sin
        k = (k * cos + pltpu.roll(k, HEAD_DIM // 2, 1) * sin) * HEAD_DIM ** -0.5
        s = s_ref[h]
        a = _dot_nt(q, k) * dm_ref[h]
        o = _dot(a, v) + _dot(q * qd_ref[h], s)
        s_ref[h] = cd_ref[h] * s + _dot_tn(k * kd_ref[h], v)
        o_ref[:, hs] = _gated_rms(o, None, gt_ref[:, hs]).astype(o_ref.dtype)


def _router_kernel(h_ref, w_ref, b_ref, o_ref):
    n_e = N_GROUPS * EXP_PER_GROUP
    lg = _dot_hi(h_ref[...], w_ref[...]) + b_ref[...]
    lane_i = lax.broadcasted_iota(jnp.int32, lg.shape, 1)
    lane = lane_i.astype(F32)
    grp = (lane_i >> GROUP_SHIFT).astype(F32)
    big = float(4 * LANE)
    is_g = (lane_i >= n_e) & (lane_i < n_e + N_GROUPS)
    gl = jnp.where(is_g, lg, NEG)
    gmax = jnp.max(gl, axis=-1, keepdims=True)
    gsum = jnp.sum(jnp.where(is_g, jnp.exp(gl - gmax), 0.0), axis=-1, keepdims=True)
    g_w = 1.0 / gsum
    g_idx = jnp.min(jnp.where(is_g & (gl == gmax), lane - n_e, big), axis=-1, keepdims=True)
    in_grp = (lane_i < n_e) & (grp == g_idx)
    el = jnp.where(in_grp, lg, NEG)
    emax = jnp.max(el, axis=-1, keepdims=True)
    eexp = jnp.where(in_grp, jnp.exp(el - emax), 0.0)
    pe = eexp / jnp.sum(eexp, axis=-1, keepdims=True)
    p1 = jnp.max(jnp.where(in_grp, pe, -1.0), axis=-1, keepdims=True)
    i1 = jnp.min(jnp.where(in_grp & (pe == p1), lane, big), axis=-1, keepdims=True)
    rest = in_grp & (lane != i1)
    p2 = jnp.max(jnp.where(rest, pe, -1.0), axis=-1, keepdims=True)
    i2 = jnp.min(jnp.where(rest & (pe == p2), lane, big), axis=-1, keepdims=True)
    den = p1 + p2
    o_ref[...] = (jnp.where(lane == i1, g_w * (p1 / den), 0.0)
                  + jnp.where(lane == i2, g_w * (p2 / den), 0.0))


def _router(h, w_r, b_r, tpp):
    n, d = h.shape
    tm = _divisor_tile(tpp, 256, 8)
    return pl.pallas_call(
        _router_kernel,
        out_shape=jax.ShapeDtypeStruct((n, LANE), F32),
        grid=(n // tm,),
        in_specs=[pl.BlockSpec((tm, d), lambda i: (i, 0)),
                  pl.BlockSpec((d, LANE), lambda i: (0, 0)),
                  pl.BlockSpec((1, LANE), lambda i: (0, 0))],
        out_specs=pl.BlockSpec((tm, LANE), lambda i: (i, 0)),
        compiler_params=_cparams(("parallel",)),
        name="router",
    )(h, w_r, b_r)


def _moe_dense_kernel(x_ref, gates_ref, wg_ref, wu_ref, wd_ref, o_ref):
    e = pl.program_id(1)

    @pl.when(e == 0)
    def _():
        o_ref[...] = jnp.zeros_like(o_ref)

    x = x_ref[...]
    gates = gates_ref[...]
    lane = lax.broadcasted_iota(jnp.int32, gates.shape, 1)
    gcol = jnp.sum(jnp.where(lane == e, gates, 0.0), axis=-1, keepdims=True)
    hg = _dot(x, wg_ref[...])
    hu = _dot(x, wu_ref[...])
    act = (_silu(hg) * hu * gcol).astype(BF16)
    o_ref[...] += _dot(act, wd_ref[...])


def _moe_dense(xb, gates, wg, wu, wd, tpp):
    n, d = xb.shape
    n_e, _, f = wg.shape
    tm = _divisor_tile(tpp, 384, 16)
    return pl.pallas_call(
        _moe_dense_kernel,
        out_shape=jax.ShapeDtypeStruct((n, d), F32),
        grid=(n // tm, n_e),
        in_specs=[pl.BlockSpec((tm, d), lambda i, e: (i, 0)),
                  pl.BlockSpec((tm, LANE), lambda i, e: (i, 0)),
                  pl.BlockSpec((None, d, f), lambda i, e: (e, 0, 0)),
                  pl.BlockSpec((None, d, f), lambda i, e: (e, 0, 0)),
                  pl.BlockSpec((None, f, d), lambda i, e: (e, 0, 0))],
        out_specs=pl.BlockSpec((tm, d), lambda i, e: (i, 0)),
        compiler_params=_cparams(("parallel", "arbitrary")),
        name="moe_dense",
    )(xb, gates, wg, wu, wd)


def _mixers(z, lay, bsz, w, prm):
    front, t_valid, tpp = lay
    n = z.shape[0]
    nc = tpp // CHUNK
    a_heads = w // HEAD_DIM
    c_heads = w // (2 * HEAD_DIM)
    hps = 2 if a_heads % 2 == 0 else 1
    cw = hps * HEAD_DIM
    wb = w // cw
    small_blk = 15 * w // LANE

    def cspec(region):
        return pl.BlockSpec((CHUNK, cw), lambda b, g, c, region=region: (b * nc + c, region * wb + g))

    ospec = pl.BlockSpec((CHUNK, cw), lambda b, g, c: (b * nc + c, g))
    oshape = jax.ShapeDtypeStruct((n, w), BF16)
    grid = (bsz, a_heads // hps, nc)
    sem = ("parallel", "parallel", "arbitrary")

    def per_group(width):
        return pl.BlockSpec((1, width), lambda b, g, c: (0, g))

    def fixed(shape):
        nd = len(shape)
        return pl.BlockSpec(shape, lambda b, g, c: (0,) * nd)

    oa = pl.pallas_call(
        functools.partial(_mixer_a_kernel, hps=hps, n_heads=a_heads, front=front, t_valid=t_valid),
        out_shape=oshape, grid=grid,
        in_specs=[cspec(0), cspec(1), cspec(2), cspec(3),
                  pl.BlockSpec((CHUNK, LANE), lambda b, g, c: (b * nc + c, small_blk)),
                  pl.BlockSpec((3, 4, cw), lambda b, g, c: (0, 0, g)),
                  fixed((1, LANE)), fixed((1, LANE)), fixed((1, HEAD_DIM))],
        out_specs=ospec,
        scratch_shapes=[pltpu.VMEM((hps, HEAD_DIM, HEAD_DIM), F32), pltpu.VMEM((3, CHUNK, cw), F32)],
        compiler_params=_cparams(sem), name="mixer_a",
    )(z, z, z, z, z, prm["conv_w"], prm["a_log_row"], prm["dt_bias_row"], prm["norm_a"])

    ob = pl.pallas_call(
        functools.partial(_mixer_b_kernel, hps=hps, front=front, t_valid=t_valid),
        out_shape=oshape, grid=grid,
        in_specs=[cspec(4), cspec(5), cspec(6), cspec(7), per_group(cw), fixed((1, HEAD_DIM))],
        out_specs=ospec,
        scratch_shapes=[pltpu.VMEM((hps, HEAD_DIM, HEAD_DIM), F32)] + [pltpu.VMEM((CHUNK, HEAD_DIM), F32)] * 4,
        compiler_params=_cparams(sem), name="mixer_b",
    )(z, z, z, z, prm["lb"], prm["norm_b"])

    tq = _divisor_tile(tpp, 384, LANE)
    nq = tpp // tq
    vw = 2 * HEAD_DIM
    qc_blk, kc_blk, vc_blk = 8 * w // vw, 9 * w // vw, 10 * w // vw

    def qspec(blk):
        return pl.BlockSpec((tq, vw), lambda b, h, i, j, blk=blk: (b * nq + i, blk + h))

    def kspec(blk):
        return pl.BlockSpec((tq, vw), lambda b, h, i, j, blk=blk: (b * nq + jnp.minimum(i, j), blk + h))

    tq_tab = pl.BlockSpec((tq, HEAD_DIM), lambda b, h, i, j: (i, 0))
    tk_tab = pl.BlockSpec((tq, HEAD_DIM), lambda b, h, i, j: (jnp.minimum(i, j), 0))

    def fixed4(shape):
        nd = len(shape)
        return pl.BlockSpec(shape, lambda b, h, i, j: (0,) * nd)

    rc = prm["rope_c"]
    oc = pl.pallas_call(
        functools.partial(_mixer_c_kernel, tq=tq, front=front, out_scale=prm["c_out_scale"]),
        out_shape=oshape, grid=(bsz, c_heads, nq, nq),
        in_specs=[qspec(qc_blk), kspec(kc_blk), kspec(vc_blk), tq_tab, tq_tab, tq_tab, tk_tab, tk_tab, tk_tab,
                  fixed4((1, vw)), fixed4((1, vw))],
        out_specs=pl.BlockSpec((tq, vw), lambda b, h, i, j: (b * nq + i, h)),
        scratch_shapes=[pltpu.VMEM((2, tq, HEAD_DIM), F32), pltpu.VMEM((2, tq, 1), F32),
                        pltpu.VMEM((2, tq, 1), F32), pltpu.VMEM((2, tq, vw), F32)],
        compiler_params=_cparams(("parallel", "parallel", "arbitrary", "arbitrary")), name="mixer_c",
    )(z, z, z, rc[0], rc[1], rc[2], rc[0], rc[1], rc[2], prm["lam_row"], prm["subln_c"])

    def hspec(shape):
        nd = len(shape)
        return pl.BlockSpec((hps,) + shape, lambda b, g, c: (g,) + (0,) * nd)

    tab = pl.BlockSpec((CHUNK, HEAD_DIM), lambda b, g, c: (c, 0))
    rd = prm["rope_d"]
    od = pl.pallas_call(
        functools.partial(_mixer_d_kernel, hps=hps),
        out_shape=oshape, grid=grid,
        in_specs=[cspec(11), cspec(12), cspec(13), cspec(14), tab, tab,
                  hspec((CHUNK, CHUNK)), hspec((CHUNK, HEAD_DIM)), hspec((CHUNK, HEAD_DIM)), hspec((1, HEAD_DIM))],
        out_specs=ospec,
        scratch_shapes=[pltpu.VMEM((hps, HEAD_DIM, HEAD_DIM), F32)],
        compiler_params=_cparams(sem), name="mixer_d",
    )(z, z, z, z, rd[0], rd[1], prm["ret_dmask"], prm["ret_qdec"], prm["ret_kdec"], prm["ret_cdec"])
    return oa, ob, oc, od


def _permute_in_proj(w_in_l, w):
    d = w_in_l.shape[0]
    n_small = w_in_l.shape[1] - 15 * w
    pad = jnp.zeros((d, LANE - n_small), w_in_l.dtype)
    return jnp.concatenate([w_in_l[:, :4 * w], w_in_l[:, 4 * w + n_small:], w_in_l[:, 4 * w:4 * w + n_small], pad],
                           axis=1).astype(BF16)


def _rope_tables(front, tpp):
    pos = (jnp.arange(tpp) - front).astype(F32)[:, None]
    half = ROPE_DIMS // 2
    inv = 1.0 / (ROPE_THETA ** (jnp.arange(half, dtype=F32) / half))
    ang = pos * inv[None, :]
    z_rest = jnp.zeros((tpp, HEAD_DIM - ROPE_DIMS), F32)
    z_half = jnp.zeros((tpp, half), F32)
    cos_c = jnp.concatenate([jnp.cos(ang), jnp.cos(ang), jnp.ones_like(z_rest)], axis=1)
    sin_lo = jnp.concatenate([-jnp.sin(ang), z_half, z_rest], axis=1)
    sin_hi = jnp.concatenate([z_half, jnp.sin(ang), z_rest], axis=1)
    half_d = HEAD_DIM // 2
    inv_d = 1.0 / (RET_THETA ** (jnp.arange(half_d, dtype=F32) / half_d))
    ang_d = pos * inv_d[None, :]
    cos_d = jnp.concatenate([jnp.cos(ang_d), jnp.cos(ang_d)], axis=1)
    sin_d = jnp.concatenate([-jnp.sin(ang_d), jnp.sin(ang_d)], axis=1)
    return (cos_c, sin_lo, sin_hi), (cos_d, sin_d)


def _retention_tables(n_heads):
    lg = jnp.log(1.0 - 2.0 ** (-5.0 - jnp.arange(n_heads, dtype=F32)))
    idx = jnp.arange(CHUNK, dtype=F32)
    rel = idx[:, None] - idx[None, :]
    causal = idx[:, None] >= idx[None, :]
    dmask = jnp.exp(jnp.where(causal[None], rel[None] * lg[:, None, None], -jnp.inf))
    qdec = jnp.exp((idx[None, :] + 1.0) * lg[:, None])
    kdec = jnp.exp((CHUNK - 1.0 - idx[None, :]) * lg[:, None])
    cdec = jnp.exp(CHUNK * lg)
    bc = lambda t: jnp.broadcast_to(t[..., None], t.shape + (HEAD_DIM,))
    return dmask, bc(qdec), bc(kdec), bc(cdec[:, None])


def _row(v, width=LANE, offset=0):
    out = jnp.zeros((1, width), F32)
    return out.at[0, offset:offset + v.shape[0]].set(v.astype(F32))


def kernel(x, meta_tokens, emb_ln_g, emb_ln_b, w_in, conv_a, a_log, dt_bias, norm_a, hgrn_lb, norm_b, lam_q1, lam_k1,
           lam_q2, lam_k2, subln_c, w_out, ln1_g, ln1_b, w_rg, b_rg, w_re, b_re, w_gate, w_up, w_down, ln2_g, ln2_b):
    bsz, seq, d = x.shape
    depth = w_in.shape[0]
    n_meta = meta_tokens.shape[0]
    w = d // 4
    a_heads = w // HEAD_DIM
    t_valid = n_meta + seq
    front = (-t_valid) % CHUNK
    tpp = -(-(front + t_valid) // ROW_ALIGN) * ROW_ALIGN
    lay = (front, t_valid, tpp)
    alpha = (2 * depth) ** 0.25

    meta = jnp.broadcast_to(meta_tokens[None].astype(x.dtype), (bsz, n_meta, d))
    xp = jnp.concatenate([jnp.zeros((bsz, front, d), x.dtype), meta, x,
                          jnp.zeros((bsz, tpp - front - t_valid, d), x.dtype)], axis=1).reshape(bsz * tpp, d)
    h, hb = _add_ln(xp, (), emb_ln_g, emb_ln_b, 1.0, lay)

    lbs = jax.nn.softmax(hgrn_lb.astype(F32), axis=0)
    lbs = jnp.cumsum(lbs, axis=0) - lbs[0:1]
    rope_c, rope_d = _rope_tables(front, tpp)
    dmask, qdec, kdec, cdec = _retention_tables(a_heads)

    for l in range(depth):
        lam_init = 0.8 - 0.6 * math.exp(-0.3 * l)
        lam = (jnp.exp(jnp.sum(lam_q1[l].astype(F32) * lam_k1[l].astype(F32)))
               - jnp.exp(jnp.sum(lam_q2[l].astype(F32) * lam_k2[l].astype(F32))) + lam_init)
        prm = {
            "conv_w": conv_a[l].astype(F32).reshape(3, w, -1).transpose(0, 2, 1),
            "a_log_row": _row(a_log[l], offset=a_heads),
            "dt_bias_row": _row(dt_bias[l], offset=a_heads),
            "norm_a": norm_a[l].astype(F32).reshape(1, HEAD_DIM),
            "lb": lbs[l].reshape(1, w),
            "norm_b": norm_b[l].astype(F32).reshape(1, HEAD_DIM),
            "rope_c": rope_c, "rope_d": rope_d,
            "lam_row": jnp.full((1, 2 * HEAD_DIM), lam, F32),
            "subln_c": subln_c[l].astype(F32).reshape(1, 2 * HEAD_DIM),
            "c_out_scale": 1.0 - lam_init,
            "ret_dmask": dmask, "ret_qdec": qdec, "ret_kdec": kdec, "ret_cdec": cdec,
        }
        z = _matmul([hb], _permute_in_proj(w_in[l], w), F32, 768, 1536)
        o_parts = _mixers(z, lay, bsz, w, prm)
        y = _matmul(list(o_parts), w_out[l].astype(BF16), F32, 768, 1024)
        h, hb = _add_ln(h, (y,), ln1_g[l], ln1_b[l], alpha, lay)

        n_e = w_re.shape[2]
        w_r = jnp.concatenate([w_re[l], w_rg[l], jnp.zeros((d, LANE - n_e - N_GROUPS), F32)], axis=1).astype(F32)
        b_r = jnp.concatenate([b_re[l], b_rg[l], jnp.zeros((LANE - n_e - N_GROUPS,), F32)]).reshape(1, LANE)
        gates = _router(h, w_r, b_r, tpp)
        y = _moe_dense(hb, gates, w_gate[l].astype(BF16), w_up[l].astype(BF16), w_down[l].astype(BF16), tpp)
        h, hb = _add_ln(h, (y,), ln2_g[l], ln2_b[l], alpha, lay)

    return h.reshape(bsz, tpp, d)[:, front + n_meta:front + t_valid]
```

```python
import functools
import math

import jax
import jax.numpy as jnp
from jax import lax
from jax.experimental import pallas as pl
from jax.experimental.pallas import tpu as pltpu

HEAD_DIM = 128
CHUNK = 64
SUB = 16
SUB_SHIFT = 4
GROUP_SHIFT = 3
ROPE_THETA = 500000.0
ROPE_DIMS = HEAD_DIM // 4
RET_THETA = 10000.0
N_GROUPS = 8
EXP_PER_GROUP = 8
NORM_EPS = 1e-6
LN_EPS = 1e-5
LANE = 128
ROW_ALIGN = 2 * CHUNK
V7X_VMEM_LIMIT_BYTES = 56 * 1024 * 1024
MOE_TILE = 128
NEG = -1e30

F32 = jnp.float32
BF16 = jnp.bfloat16
HI = lax.Precision.HIGHEST


def _cparams(sem):
    return pltpu.CompilerParams(dimension_semantics=sem, vmem_limit_bytes=V7X_VMEM_LIMIT_BYTES)


def _divisor_tile(n, cap, align):
    best = None
    for t in range(align, min(n, cap) + 1, align):
        if n % t == 0:
            best = t
    assert best is not None, (n, cap, align)
    return best


def _dot(a, b):
    return jnp.dot(a, b, preferred_element_type=F32)


def _dot_hi(a, b):
    return jnp.dot(a, b, preferred_element_type=F32, precision=HI)


def _dot_nt(a, b):
    return lax.dot_general(a, b, (((1,), (1,)), ((), ())), preferred_element_type=F32)


def _dot_tn(a, b):
    return lax.dot_general(a, b, (((0,), (0,)), ((), ())), preferred_element_type=F32)


def _sigmoid(x):
    return 1.0 / (1.0 + jnp.exp(-x))


def _silu(x):
    return x * _sigmoid(x)


def _softplus(x):
    return jnp.maximum(x, 0.0) + jnp.log(1.0 + jnp.exp(-jnp.abs(x)))


def _valid_rows(row0, n, front, t_valid):
    rb = row0 + lax.broadcasted_iota(jnp.int32, (n, 1), 0)
    return (rb >= front) & (rb < front + t_valid)


def _ln_kernel(*refs, alpha, n_y, routed, front, t_valid, tpp, tm):
    h_ref = refs[0]
    y_refs = refs[1:1 + n_y]
    rest = refs[1 + n_y:]
    x = h_ref[...]
    if routed:
        rt_ref, rest = rest[0], rest[1:]
        x = alpha * x + rt_ref[:, 2:3] * y_refs[0][...] + rt_ref[:, 3:4] * y_refs[1][...]
    elif n_y:
        x = alpha * x
        for y_ref in y_refs:
            x = x + y_ref[...]
    g_ref, b_ref, o_ref, ob_ref = rest
    mu = jnp.mean(x, axis=-1, keepdims=True)
    xc = x - mu
    var = jnp.mean(xc * xc, axis=-1, keepdims=True)
    y = xc * lax.rsqrt(var + LN_EPS) * g_ref[...] + b_ref[...]
    row0 = (pl.program_id(0) * tm) % tpp
    y = jnp.where(_valid_rows(row0, tm, front, t_valid), y, 0.0)
    o_ref[...] = y
    ob_ref[...] = y.astype(BF16)


def _add_ln(h, ys, g, b, alpha, lay, route=None):
    front, t_valid, tpp = lay
    n, d = h.shape
    tm = _divisor_tile(tpp, 128, 16)
    spec = pl.BlockSpec((tm, d), lambda i: (i, 0))
    vec = pl.BlockSpec((1, d), lambda i: (0, 0))
    y_specs = [spec] * len(ys)
    extra, extra_specs = (), []
    if route is not None:
        ys = (ys[0], ys[0])
        y_specs = [spec, pl.BlockSpec((tm, d), lambda i: (i + n // tm, 0))]
        extra, extra_specs = (route,), [pl.BlockSpec((tm, LANE), lambda i: (i, 0))]
    kern = functools.partial(_ln_kernel, alpha=alpha, n_y=len(ys), routed=route is not None, front=front,
                             t_valid=t_valid, tpp=tpp, tm=tm)
    return pl.pallas_call(
        kern,
        out_shape=(jax.ShapeDtypeStruct((n, d), F32), jax.ShapeDtypeStruct((n, d), BF16)),
        grid=(n // tm,),
        in_specs=[spec] + y_specs + extra_specs + [vec, vec],
        out_specs=(spec, spec),
        compiler_params=_cparams(("parallel",)),
        name="add_ln",
    )(h, *ys, *extra, g.reshape(1, d), b.reshape(1, d))


def _mm_kernel(*refs, ks):
    a_refs = refs[:len(ks)]
    w_ref, o_ref = refs[len(ks):]
    acc = None
    off = 0
    for a_ref, k in zip(a_refs, ks):
        p = _dot(a_ref[...], w_ref[off:off + k, :])
        acc = p if acc is None else acc + p
        off += k
    o_ref[...] = acc.astype(o_ref.dtype)


def _matmul(a_list, w, out_dtype, tm_cap, tn_cap):
    n = a_list[0].shape[0]
    ks = tuple(a.shape[1] for a in a_list)
    kdim, ndim = w.shape
    assert sum(ks) == kdim
    tm = _divisor_tile(n, tm_cap, 16)
    tn = _divisor_tile(ndim, tn_cap, LANE)
    in_specs = [pl.BlockSpec((tm, k), lambda j, i: (i, 0)) for k in ks]
    in_specs.append(pl.BlockSpec((kdim, tn), lambda j, i: (0, j)))
    return pl.pallas_call(
        functools.partial(_mm_kernel, ks=ks),
        out_shape=jax.ShapeDtypeStruct((n, ndim), out_dtype),
        grid=(ndim // tn, n // tm),
        in_specs=in_specs,
        out_specs=pl.BlockSpec((tm, tn), lambda j, i: (i, j)),
        compiler_params=_cparams(("parallel", "parallel")),
        name="matmul",
    )(*a_list, w)


def _tri_masks(n):
    r = lax.broadcasted_iota(jnp.int32, (n, n), 0)
    c = lax.broadcasted_iota(jnp.int32, (n, n), 1)
    return r, c


def _gated_rms(o, w_row, gate):
    o = o * lax.rsqrt(jnp.mean(o * o, axis=-1, keepdims=True) + NORM_EPS)
    if w_row is not None:
        o = o * w_row
    return o * _silu(gate)


def _mixer_a_kernel(q_ref, k_ref, v_ref, gt_ref, sm_ref, cw_ref, alog_ref, dtb_ref, nw_ref, o_ref,
                    s_ref, prev_ref, *, hps, n_heads, front, t_valid):
    c = pl.program_id(2)

    @pl.when(c == 0)
    def _():
        s_ref[...] = jnp.zeros_like(s_ref)
        prev_ref[...] = jnp.zeros_like(prev_ref)

    row = lax.broadcasted_iota(jnp.int32, (CHUNK, 1), 0)
    valid = _valid_rows(c * CHUNK, CHUNK, front, t_valid).astype(F32)

    def conv_silu(x_ref, p):
        cur = x_ref[...]
        prev = prev_ref[p]
        acc = cur * cw_ref[p, 3:4, :]
        for s in (1, 2, 3):
            sh = jnp.where(row >= s, pltpu.roll(cur, s, 0), pltpu.roll(prev, s, 0))
            acc = acc + sh * cw_ref[p, 3 - s:4 - s, :]
        prev_ref[p] = cur
        return _silu(acc)

    q_all = conv_silu(q_ref, 0)
    k_all = conv_silu(k_ref, 1) * valid
    v_all = conv_silu(v_ref, 2) * valid

    sm = sm_ref[...]
    beta_all = _sigmoid(sm) * valid
    g_all = -jnp.exp(alog_ref[...]) * _softplus(sm + dtb_ref[...]) * valid
    lane = lax.broadcasted_iota(jnp.int32, (CHUNK, LANE), 1)

    r, cc = _tri_masks(CHUNK)
    causal = r >= cc
    strict = r > cc
    ltri = causal.astype(F32)
    utri = (r <= cc).astype(F32)
    ones = jnp.ones((CHUNK, CHUNK), F32)
    eye = (r == cc).astype(F32)

    for h in range(hps):
        hg = pl.program_id(1) * hps + h
        hs = slice(h * HEAD_DIM, (h + 1) * HEAD_DIM)
        beta = jnp.sum(jnp.where(lane == hg, beta_all, 0.0), axis=-1, keepdims=True)
        g = jnp.sum(jnp.where(lane == n_heads + hg, g_all, 0.0), axis=-1, keepdims=True)
        gb = jnp.broadcast_to(g, (CHUNK, LANE))
        gcum = _dot_hi(ltri, gb)
        gcum_row = _dot_hi(ones, gb[:, :CHUNK] * utri)
        decay = jnp.where(causal, jnp.exp(jnp.minimum(gcum[:, :CHUNK] - gcum_row, 0.0)), 0.0)

        q = q_all[:, hs]
        k = k_all[:, hs]
        v = v_all[:, hs]
        q = q * lax.rsqrt(jnp.sum(q * q, axis=-1, keepdims=True) + NORM_EPS) * HEAD_DIM ** -0.5
        k = k * lax.rsqrt(jnp.sum(k * k, axis=-1, keepdims=True) + NORM_EPS)
        kb = k * beta
        m = jnp.where(strict, -(_dot_nt(kb, k) * decay), 0.0)
        inv = eye + m
        mp = m
        for _ in range(5):
            mp = _dot(mp, mp)
            inv = inv + _dot(inv, mp)
        eg = jnp.exp(gcum)
        sol = _dot(inv, jnp.concatenate([v * beta, kb * eg], axis=1))
        u = sol[:, :HEAD_DIM]
        w = sol[:, HEAD_DIM:]
        s = s_ref[h]
        v_new = u - _dot(w, s)
        attn = _dot_nt(q, k) * decay
        o = _dot(q * eg, s) + _dot(attn, v_new)
        g_last = gcum[CHUNK - 1:CHUNK, :]
        s_ref[h] = s * jnp.exp(g_last) + _dot_tn(k * jnp.exp(g_last - gcum), v_new)
        o_ref[:, hs] = _gated_rms(o, nw_ref[...], gt_ref[:, hs]).astype(o_ref.dtype)


def _mixer_b_kernel(q_ref, f_ref, i_ref, gt_ref, lb_ref, nw_ref, o_ref, st_ref, kbuf, bbuf, vbuf, obuf,
                    *, hps, front, t_valid):
    c = pl.program_id(2)

    @pl.when(c == 0)
    def _():
        st_ref[...] = jnp.zeros_like(st_ref)

    valid = _valid_rows(c * CHUNK, CHUNK, front, t_valid).astype(F32)
    r, cc = _tri_masks(CHUNK)
    lblk = ((r >= cc) & ((r >> SUB_SHIFT) == (cc >> SUB_SHIFT))).astype(F32)
    row16 = lax.broadcasted_iota(jnp.int32, (SUB, 1), 0)

    for h in range(hps):
        hs = slice(h * HEAD_DIM, (h + 1) * HEAD_DIM)
        lb = lb_ref[:, hs]
        f = lb + (1.0 - lb) * _sigmoid(f_ref[:, hs])
        logf = jnp.log(f) * valid
        kk = (1.0 - f) * valid
        q = _silu(q_ref[:, hs])
        v = i_ref[:, hs]
        bl = _dot_hi(lblk, logf)
        kbuf[...] = kk
        bbuf[...] = bl
        vbuf[...] = v
        st = st_ref[h]
        for blk in range(CHUNK // SUB):
            rs = slice(blk * SUB, (blk + 1) * SUB)
            b16 = bl[rs]
            q16 = q[rs]
            b_end = b16[SUB - 1:SUB, :]
            acc = _dot_nt(q16 * jnp.exp(b16), st)
            for j in range(SUB):
                jr = pl.ds(blk * SUB + j, 1)
                e = jnp.exp(jnp.minimum(b16 - bbuf[jr, :], 0.0))
                sc = jnp.sum(q16 * kbuf[jr, :] * e, axis=-1, keepdims=True)
                acc = acc + jnp.where(row16 >= j, sc, 0.0) * vbuf[jr, :]
            st = st * jnp.exp(b_end) + _dot_tn(v[rs], kk[rs] * jnp.exp(b_end - b16))
            obuf[rs, :] = acc
        st_ref[h] = st
        o_ref[:, hs] = _gated_rms(obuf[...], nw_ref[...], gt_ref[:, hs]).astype(o_ref.dtype)


def _rope_partial(x, cos, sin_lo, sin_hi):
    half = ROPE_DIMS // 2
    return x * cos + pltpu.roll(x, LANE - half, 1) * sin_lo + pltpu.roll(x, half, 1) * sin_hi


def _mixer_c_kernel(q_ref, k_ref, v_ref, cq_ref, slq_ref, shq_ref, ck_ref, slk_ref, shk_ref, lam_ref, nw_ref,
                    o_ref, qs_ref, m_ref, l_ref, acc_ref, *, tq, front, out_scale):
    i = pl.program_id(2)
    j = pl.program_id(3)

    @pl.when(j == 0)
    def _():
        for mp in range(2):
            ms = slice(mp * HEAD_DIM, (mp + 1) * HEAD_DIM)
            qs_ref[mp] = _rope_partial(q_ref[:, ms], cq_ref[...], slq_ref[...], shq_ref[...]) * HEAD_DIM ** -0.5
        m_ref[...] = jnp.full_like(m_ref, NEG)
        l_ref[...] = jnp.zeros_like(l_ref)
        acc_ref[...] = jnp.zeros_like(acc_ref)

    @pl.when(j <= i)
    def _():
        rq = i * tq + lax.broadcasted_iota(jnp.int32, (tq, 1), 0)
        rk = j * tq + lax.broadcasted_iota(jnp.int32, (1, tq), 1)
        msk = (rk <= rq) & (rk >= front)
        v = v_ref[...]
        for mp in range(2):
            ms = slice(mp * HEAD_DIM, (mp + 1) * HEAD_DIM)
            km = _rope_partial(k_ref[:, ms], ck_ref[...], slk_ref[...], shk_ref[...])
            s = jnp.where(msk, _dot_nt(qs_ref[mp], km), NEG)
            m_prev = m_ref[mp]
            m_new = jnp.maximum(m_prev, jnp.max(s, axis=-1, keepdims=True))
            p = jnp.exp(s - m_new)
            a = jnp.exp(m_prev - m_new)
            l_ref[mp] = a * l_ref[mp] + jnp.sum(p, axis=-1, keepdims=True)
            acc_ref[mp] = a * acc_ref[mp] + _dot(p, v)
            m_ref[mp] = m_new

    @pl.when(j == i)
    def _():
        o = acc_ref[0] / l_ref[0] - lam_ref[...] * (acc_ref[1] / l_ref[1])
        o = o * lax.rsqrt(jnp.mean(o * o, axis=-1, keepdims=True) + NORM_EPS) * nw_ref[...] * out_scale
        o_ref[...] = o.astype(o_ref.dtype)


def _mixer_d_kernel(q_ref, k_ref, v_ref, gt_ref, cos_ref, sin_ref, dm_ref, qd_ref, kd_ref, cd_ref, o_ref, s_ref, *, hps):
    c = pl.program_id(2)

    @pl.when(c == 0)
    def _():
        s_ref[...] = jnp.zeros_like(s_ref)

    cos = cos_ref[...]
    sin = sin_ref[...]
    for h in range(hps):
        hs = slice(h * HEAD_DIM, (h + 1) * HEAD_DIM)
        q = q_ref[:, hs]
        k = k_ref[:, hs]
        v = v_ref[:, hs]
        q = q * cos + pltpu.roll(q, HEAD_DIM // 2, 1) * sin
        k = (k * cos + pltpu.roll(k, HEAD_DIM // 2, 1) * sin) * HEAD_DIM ** -0.5
        s = s_ref[h]
        a = _dot_nt(q, k) * dm_ref[h]
        o = _dot(a, v) + _dot(q * qd_ref[h], s)
        s_ref[h] = cd_ref[h] * s + _dot_tn(k * kd_ref[h], v)
        o_ref[:, hs] = _gated_rms(o, None, gt_ref[:, hs]).astype(o_ref.dtype)


def _router_kernel(h_ref, w_ref, b_ref, o_ref):
    n_e = N_GROUPS * EXP_PER_GROUP
    lg = _dot_hi(h_ref[...], w_ref[...]) + b_ref[...]
    lane_i = lax.broadcasted_iota(jnp.int32, lg.shape, 1)
    lane = lane_i.astype(F32)
    grp = (lane_i >> GROUP_SHIFT).astype(F32)
    big = float(4 * LANE)
    is_g = (lane_i >= n_e) & (lane_i < n_e + N_GROUPS)
    gl = jnp.where(is_g, lg, NEG)
    gmax = jnp.max(gl, axis=-1, keepdims=True)
    gsum = jnp.sum(jnp.where(is_g, jnp.exp(gl - gmax), 0.0), axis=-1, keepdims=True)
    g_w = 1.0 / gsum
    g_idx = jnp.min(jnp.where(is_g & (gl == gmax), lane - n_e, big), axis=-1, keepdims=True)
    in_grp = (lane_i < n_e) & (grp == g_idx)
    el = jnp.where(in_grp, lg, NEG)
    emax = jnp.max(el, axis=-1, keepdims=True)
    eexp = jnp.where(in_grp, jnp.exp(el - emax), 0.0)
    pe = eexp / jnp.sum(eexp, axis=-1, keepdims=True)
    p1 = jnp.max(jnp.where(in_grp, pe, -1.0), axis=-1, keepdims=True)
    i1 = jnp.min(jnp.where(in_grp & (pe == p1), lane, big), axis=-1, keepdims=True)
    rest = in_grp & (lane != i1)
    p2 = jnp.max(jnp.where(rest, pe, -1.0), axis=-1, keepdims=True)
    i2 = jnp.min(jnp.where(rest & (pe == p2), lane, big), axis=-1, keepdims=True)
    den = p1 + p2
    o_ref[...] = (jnp.where(lane_i == 0, i1, 0.0) + jnp.where(lane_i == 1, i2, 0.0)
                  + jnp.where(lane_i == 2, g_w * (p1 / den), 0.0) + jnp.where(lane_i == 3, g_w * (p2 / den), 0.0))


def _router(h, w_r, b_r, tpp):
    n, d = h.shape
    tm = _divisor_tile(tpp, 256, 8)
    return pl.pallas_call(
        _router_kernel,
        out_shape=jax.ShapeDtypeStruct((n, LANE), F32),
        grid=(n // tm,),
        in_specs=[pl.BlockSpec((tm, d), lambda i: (i, 0)),
                  pl.BlockSpec((d, LANE), lambda i: (0, 0)),
                  pl.BlockSpec((1, LANE), lambda i: (0, 0))],
        out_specs=pl.BlockSpec((tm, LANE), lambda i: (i, 0)),
        compiler_params=_cparams(("parallel",)),
        name="router",
    )(h, w_r, b_r)


def _row_copy(src, src_row, dst, dst_row, sem):
    return pltpu.make_async_copy(src.at[pl.ds(src_row, 1)], dst.at[pl.ds(dst_row, 1)], sem)


def _moe_kernel(te_ref, cnt_ref, src_ref, dst_ref, x_hbm, wg_ref, wu_ref, wd_ref, y_hbm, xbuf, ybuf, sem_in, sem_out,
                *, tm):
    del te_ref
    cnt = cnt_ref[pl.program_id(0)]

    @pl.when(cnt > 0)
    def _():
        def gather_start(r, c):
            _row_copy(x_hbm, src_ref[0, 0, r], xbuf, r, sem_in).start()
            return c

        def gather_wait(r, c):
            _row_copy(x_hbm, src_ref[0, 0, r], xbuf, r, sem_in).wait()
            return c

        lax.fori_loop(0, tm, gather_start, 0)
        lax.fori_loop(0, tm, gather_wait, 0)
        x = xbuf[...].astype(BF16)
        hg = _dot(x, wg_ref[...].astype(BF16))
        hu = _dot(x, wu_ref[...].astype(BF16))
        act = (_silu(hg) * hu).astype(BF16)
        ybuf[...] = _dot(act, wd_ref[...].astype(BF16))

        def scatter_start(r, c):
            _row_copy(ybuf, r, y_hbm, dst_ref[0, 0, r], sem_out).start()
            return c

        def scatter_wait(r, c):
            _row_copy(ybuf, r, y_hbm, dst_ref[0, 0, r], sem_out).wait()
            return c

        lax.fori_loop(0, cnt, scatter_start, 0)
        lax.fori_loop(0, cnt, scatter_wait, 0)


def _dispatch(route, n, tm, n_e):
    ids = jnp.concatenate([route[:, 0], route[:, 1]]).astype(jnp.int32)
    n_pairs = 2 * n
    n_tiles = n_pairs // tm + n_e
    order = jnp.argsort(ids, stable=True).astype(jnp.int32)
    ids_s = ids[order]
    counts = jnp.sum((ids[:, None] == jnp.arange(n_e, dtype=jnp.int32)[None, :]).astype(jnp.int32), axis=0)
    tiles_e = (counts + tm - 1) // tm
    tile_end = jnp.cumsum(tiles_e)
    tile_start = tile_end - tiles_e
    first = jnp.cumsum(counts) - counts
    pos = tile_start[ids_s] * tm + (jnp.arange(n_pairs, dtype=jnp.int32) - first[ids_s])
    src = jnp.zeros((n_tiles * tm,), jnp.int32).at[pos].set(order % n)
    dst = jnp.zeros((n_tiles * tm,), jnp.int32).at[pos].set(order)
    tile = jnp.arange(n_tiles, dtype=jnp.int32)
    te = jnp.minimum(jnp.searchsorted(tile_end, tile, side="right").astype(jnp.int32), n_e - 1)
    cnt = jnp.clip(counts[te] - (tile - tile_start[te]) * tm, 0, tm)
    cnt = jnp.where(tile < tile_end[n_e - 1], cnt, 0).astype(jnp.int32)
    te = jnp.where(cnt > 0, te, jnp.max(jnp.where(cnt > 0, te, 0)))
    return te, cnt, src.reshape(n_tiles, 1, tm), dst.reshape(n_tiles, 1, tm)


def _moe_routed(h, route, wg, wu, wd, tm):
    n, d = h.shape
    n_e, _, f = wg.shape
    te, cnt, src, dst = _dispatch(route, n, tm, n_e)
    n_tiles = te.shape[0]
    idx_spec = pl.BlockSpec((1, 1, tm), lambda t, te, cnt: (t, 0, 0), memory_space=pltpu.SMEM)
    grid_spec = pltpu.PrefetchScalarGridSpec(
        num_scalar_prefetch=2, grid=(n_tiles,),
        in_specs=[idx_spec, idx_spec,
                  pl.BlockSpec(memory_space=pl.ANY),
                  pl.BlockSpec((None, d, f), lambda t, te, cnt: (te[t], 0, 0)),
                  pl.BlockSpec((None, d, f), lambda t, te, cnt: (te[t], 0, 0)),
                  pl.BlockSpec((None, f, d), lambda t, te, cnt: (te[t], 0, 0))],
        out_specs=pl.BlockSpec(memory_space=pl.ANY),
        scratch_shapes=[pltpu.VMEM((tm, d), F32), pltpu.VMEM((tm, d), F32),
                        pltpu.SemaphoreType.DMA, pltpu.SemaphoreType.DMA])
    return pl.pallas_call(
        functools.partial(_moe_kernel, tm=tm),
        out_shape=jax.ShapeDtypeStruct((2 * n, d), F32),
        grid_spec=grid_spec,
        compiler_params=_cparams(("arbitrary",)),
        name="moe_routed",
    )(te, cnt, src, dst, h, wg, wu, wd)


def _mixers(z, lay, bsz, w, prm):
    front, t_valid, tpp = lay
    n = z.shape[0]
    nc = tpp // CHUNK
    a_heads = w // HEAD_DIM
    c_heads = w // (2 * HEAD_DIM)
    hps = 2 if a_heads % 2 == 0 else 1
    cw = hps * HEAD_DIM
    wb = w // cw
    small_blk = 15 * w // LANE

    def cspec(region):
        return pl.BlockSpec((CHUNK, cw), lambda b, g, c, region=region: (b * nc + c, region * wb + g))

    ospec = pl.BlockSpec((CHUNK, cw), lambda b, g, c: (b * nc + c, g))
    oshape = jax.ShapeDtypeStruct((n, w), BF16)
    grid = (bsz, a_heads // hps, nc)
    sem = ("parallel", "parallel", "arbitrary")

    def per_group(width):
        return pl.BlockSpec((1, width), lambda b, g, c: (0, g))

    def fixed(shape):
        nd = len(shape)
        return pl.BlockSpec(shape, lambda b, g, c: (0,) * nd)

    oa = pl.pallas_call(
        functools.partial(_mixer_a_kernel, hps=hps, n_heads=a_heads, front=front, t_valid=t_valid),
        out_shape=oshape, grid=grid,
        in_specs=[cspec(0), cspec(1), cspec(2), cspec(3),
                  pl.BlockSpec((CHUNK, LANE), lambda b, g, c: (b * nc + c, small_blk)),
                  pl.BlockSpec((3, 4, cw), lambda b, g, c: (0, 0, g)),
                  fixed((1, LANE)), fixed((1, LANE)), fixed((1, HEAD_DIM))],
        out_specs=ospec,
        scratch_shapes=[pltpu.VMEM((hps, HEAD_DIM, HEAD_DIM), F32), pltpu.VMEM((3, CHUNK, cw), F32)],
        compiler_params=_cparams(sem), name="mixer_a",
    )(z, z, z, z, z, prm["conv_w"], prm["a_log_row"], prm["dt_bias_row"], prm["norm_a"])

    ob = pl.pallas_call(
        functools.partial(_mixer_b_kernel, hps=hps, front=front, t_valid=t_valid),
        out_shape=oshape, grid=grid,
        in_specs=[cspec(4), cspec(5), cspec(6), cspec(7), per_group(cw), fixed((1, HEAD_DIM))],
        out_specs=ospec,
        scratch_shapes=[pltpu.VMEM((hps, HEAD_DIM, HEAD_DIM), F32)] + [pltpu.VMEM((CHUNK, HEAD_DIM), F32)] * 4,
        compiler_params=_cparams(sem), name="mixer_b",
    )(z, z, z, z, prm["lb"], prm["norm_b"])

    tq = _divisor_tile(tpp, 384, LANE)
    nq = tpp // tq
    vw = 2 * HEAD_DIM
    qc_blk, kc_blk, vc_blk = 8 * w // vw, 9 * w // vw, 10 * w // vw

    def qspec(blk):
        return pl.BlockSpec((tq, vw), lambda b, h, i, j, blk=blk: (b * nq + i, blk + h))

    def kspec(blk):
        return pl.BlockSpec((tq, vw), lambda b, h, i, j, blk=blk: (b * nq + jnp.minimum(i, j), blk + h))

    tq_tab = pl.BlockSpec((tq, HEAD_DIM), lambda b, h, i, j: (i, 0))
    tk_tab = pl.BlockSpec((tq, HEAD_DIM), lambda b, h, i, j: (jnp.minimum(i, j), 0))

    def fixed4(shape):
        nd = len(shape)
        return pl.BlockSpec(shape, lambda b, h, i, j: (0,) * nd)

    rc = prm["rope_c"]
    oc = pl.pallas_call(
        functools.partial(_mixer_c_kernel, tq=tq, front=front, out_scale=prm["c_out_scale"]),
        out_shape=oshape, grid=(bsz, c_heads, nq, nq),
        in_specs=[qspec(qc_blk), kspec(kc_blk), kspec(vc_blk), tq_tab, tq_tab, tq_tab, tk_tab, tk_tab, tk_tab,
                  fixed4((1, vw)), fixed4((1, vw))],
        out_specs=pl.BlockSpec((tq, vw), lambda b, h, i, j: (b * nq + i, h)),
        scratch_shapes=[pltpu.VMEM((2, tq, HEAD_DIM), F32), pltpu.VMEM((2, tq, 1), F32),
                        pltpu.VMEM((2, tq, 1), F32), pltpu.VMEM((2, tq, vw), F32)],
        compiler_params=_cparams(("parallel", "parallel", "arbitrary", "arbitrary")), name="mixer_c",
    )(z, z, z, rc[0], rc[1], rc[2], rc[0], rc[1], rc[2], prm["lam_row"], prm["subln_c"])

    def hspec(shape):
        nd = len(shape)
        return pl.BlockSpec((hps,) + shape, lambda b, g, c: (g,) + (0,) * nd)

    tab = pl.BlockSpec((CHUNK, HEAD_DIM), lambda b, g, c: (c, 0))
    rd = prm["rope_d"]
    od = pl.pallas_call(
        functools.partial(_mixer_d_kernel, hps=hps),
        out_shape=oshape, grid=grid,
        in_specs=[cspec(11), cspec(12), cspec(13), cspec(14), tab, tab,
                  hspec((CHUNK, CHUNK)), hspec((CHUNK, HEAD_DIM)), hspec((CHUNK, HEAD_DIM)), hspec((1, HEAD_DIM))],
        out_specs=ospec,
        scratch_shapes=[pltpu.VMEM((hps, HEAD_DIM, HEAD_DIM), F32)],
        compiler_params=_cparams(sem), name="mixer_d",
    )(z, z, z, z, rd[0], rd[1], prm["ret_dmask"], prm["ret_qdec"], prm["ret_kdec"], prm["ret_cdec"])
    return oa, ob, oc, od


def _permute_in_proj(w_in_l, w):
    d = w_in_l.shape[0]
    n_small = w_in_l.shape[1] - 15 * w
    pad = jnp.zeros((d, LANE - n_small), w_in_l.dtype)
    return jnp.concatenate([w_in_l[:, :4 * w], w_in_l[:, 4 * w + n_small:], w_in_l[:, 4 * w:4 * w + n_small], pad],
                           axis=1).astype(BF16)


def _rope_tables(front, tpp):
    pos = (jnp.arange(tpp) - front).astype(F32)[:, None]
    half = ROPE_DIMS // 2
    inv = 1.0 / (ROPE_THETA ** (jnp.arange(half, dtype=F32) / half))
    ang = pos * inv[None, :]
    z_rest = jnp.zeros((tpp, HEAD_DIM - ROPE_DIMS), F32)
    z_half = jnp.zeros((tpp, half), F32)
    cos_c = jnp.concatenate([jnp.cos(ang), jnp.cos(ang), jnp.ones_like(z_rest)], axis=1)
    sin_lo = jnp.concatenate([-jnp.sin(ang), z_half, z_rest], axis=1)
    sin_hi = jnp.concatenate([z_half, jnp.sin(ang), z_rest], axis=1)
    half_d = HEAD_DIM // 2
    inv_d = 1.0 / (RET_THETA ** (jnp.arange(half_d, dtype=F32) / half_d))
    ang_d = pos * inv_d[None, :]
    cos_d = jnp.concatenate([jnp.cos(ang_d), jnp.cos(ang_d)], axis=1)
    sin_d = jnp.concatenate([-jnp.sin(ang_d), jnp.sin(ang_d)], axis=1)
    return (cos_c, sin_lo, sin_hi), (cos_d, sin_d)


def _retention_tables(n_heads):
    lg = jnp.log(1.0 - 2.0 ** (-5.0 - jnp.arange(n_heads, dtype=F32)))
    idx = jnp.arange(CHUNK, dtype=F32)
    rel = idx[:, None] - idx[None, :]
    causal = idx[:, None] >= idx[None, :]
    dmask = jnp.exp(jnp.where(causal[None], rel[None] * lg[:, None, None], -jnp.inf))
    qdec = jnp.exp((idx[None, :] + 1.0) * lg[:, None])
    kdec = jnp.exp((CHUNK - 1.0 - idx[None, :]) * lg[:, None])
    cdec = jnp.exp(CHUNK * lg)
    bc = lambda t: jnp.broadcast_to(t[..., None], t.shape + (HEAD_DIM,))
    return dmask, bc(qdec), bc(kdec), bc(cdec[:, None])


def _row(v, width=LANE, offset=0):
    out = jnp.zeros((1, width), F32)
    return out.at[0, offset:offset + v.shape[0]].set(v.astype(F32))


def kernel(x, meta_tokens, emb_ln_g, emb_ln_b, w_in, conv_a, a_log, dt_bias, norm_a, hgrn_lb, norm_b, lam_q1, lam_k1,
           lam_q2, lam_k2, subln_c, w_out, ln1_g, ln1_b, w_rg, b_rg, w_re, b_re, w_gate, w_up, w_down, ln2_g, ln2_b):
    bsz, seq, d = x.shape
    depth = w_in.shape[0]
    n_meta = meta_tokens.shape[0]
    w = d // 4
    a_heads = w // HEAD_DIM
    t_valid = n_meta + seq
    front = (-t_valid) % CHUNK
    tpp = -(-(front + t_valid) // ROW_ALIGN) * ROW_ALIGN
    lay = (front, t_valid, tpp)
    alpha = (2 * depth) ** 0.25

    meta = jnp.broadcast_to(meta_tokens[None].astype(x.dtype), (bsz, n_meta, d))
    xp = jnp.concatenate([jnp.zeros((bsz, front, d), x.dtype), meta, x,
                          jnp.zeros((bsz, tpp - front - t_valid, d), x.dtype)], axis=1).reshape(bsz * tpp, d)
    h, hb = _add_ln(xp, (), emb_ln_g, emb_ln_b, 1.0, lay)

    lbs = jax.nn.softmax(hgrn_lb.astype(F32), axis=0)
    lbs = jnp.cumsum(lbs, axis=0) - lbs[0:1]
    rope_c, rope_d = _rope_tables(front, tpp)
    dmask, qdec, kdec, cdec = _retention_tables(a_heads)

    for l in range(depth):
        lam_init = 0.8 - 0.6 * math.exp(-0.3 * l)
        lam = (jnp.exp(jnp.sum(lam_q1[l].astype(F32) * lam_k1[l].astype(F32)))
               - jnp.exp(jnp.sum(lam_q2[l].astype(F32) * lam_k2[l].astype(F32))) + lam_init)
        prm = {
            "conv_w": conv_a[l].astype(F32).reshape(3, w, -1).transpose(0, 2, 1),
            "a_log_row": _row(a_log[l], offset=a_heads),
            "dt_bias_row": _row(dt_bias[l], offset=a_heads),
            "norm_a": norm_a[l].astype(F32).reshape(1, HEAD_DIM),
            "lb": lbs[l].reshape(1, w),
            "norm_b": norm_b[l].astype(F32).reshape(1, HEAD_DIM),
            "rope_c": rope_c, "rope_d": rope_d,
            "lam_row": jnp.full((1, 2 * HEAD_DIM), lam, F32),
            "subln_c": subln_c[l].astype(F32).reshape(1, 2 * HEAD_DIM),
            "c_out_scale": 1.0 - lam_init,
            "ret_dmask": dmask, "ret_qdec": qdec, "ret_kdec": kdec, "ret_cdec": cdec,
        }
        z = _matmul([hb], _permute_in_proj(w_in[l], w), F32, 768, 1536)
        o_parts = _mixers(z, lay, bsz, w, prm)
        y = _matmul(list(o_parts), w_out[l].astype(BF16), F32, 768, 1024)
        h, hb = _add_ln(h, (y,), ln1_g[l], ln1_b[l], alpha, lay)

        n_e = w_re.shape[2]
        w_r = jnp.concatenate([w_re[l], w_rg[l], jnp.zeros((d, LANE - n_e - N_GROUPS), F32)], axis=1).astype(F32)
        b_r = jnp.concatenate([b_re[l], b_rg[l], jnp.zeros((LANE - n_e - N_GROUPS,), F32)]).reshape(1, LANE)
        route = _router(h, w_r, b_r, tpp)
        y2 = _moe_routed(h, route, w_gate[l], w_up[l], w_down[l], MOE_TILE)
        h, hb = _add_ln(h, (y2,), ln2_g[l], ln2_b[l], alpha, lay, route=route)

    return h.reshape(bsz, tpp, d)[:, front + n_meta:front + t_valid]
```

```python
import functools
import math

import jax
import jax.numpy as jnp
from jax import lax
from jax.experimental import pallas as pl
from jax.experimental.pallas import tpu as pltpu

HEAD_DIM = 128
CHUNK = 64
SUB = 16
GROUP_SHIFT = 3
ROPE_THETA = 500000.0
ROPE_DIMS = HEAD_DIM // 4
RET_THETA = 10000.0
N_GROUPS = 8
EXP_PER_GROUP = 8
NORM_EPS = 1e-6
LN_EPS = 1e-5
LANE = 128
SUBLANES = 8
ROW_ALIGN = 2 * CHUNK
V7X_VMEM_LIMIT_BYTES = 56 * 1024 * 1024
MOE_TILE = 128
NEG = -1e30

F32 = jnp.float32
BF16 = jnp.bfloat16
HI = lax.Precision.HIGHEST


def _cparams(sem):
    return pltpu.CompilerParams(dimension_semantics=sem, vmem_limit_bytes=V7X_VMEM_LIMIT_BYTES)


def _divisor_tile(n, cap, align):
    best = None
    for t in range(align, min(n, cap) + 1, align):
        if n % t == 0:
            best = t
    assert best is not None, (n, cap, align)
    return best


def _dot(a, b):
    return jnp.dot(a, b, preferred_element_type=F32)


def _dot_hi(a, b):
    return jnp.dot(a, b, preferred_element_type=F32, precision=HI)


def _dot_nt(a, b):
    return lax.dot_general(a, b, (((1,), (1,)), ((), ())), preferred_element_type=F32)


def _dot_tn(a, b):
    return lax.dot_general(a, b, (((0,), (0,)), ((), ())), preferred_element_type=F32)


def _sigmoid(x):
    return 1.0 / (1.0 + jnp.exp(-x))


def _silu(x):
    return x * _sigmoid(x)


def _softplus(x):
    return jnp.maximum(x, 0.0) + jnp.log(1.0 + jnp.exp(-jnp.abs(x)))


def _valid_rows(row0, n, front, t_valid):
    rb = row0 + lax.broadcasted_iota(jnp.int32, (n, 1), 0)
    return (rb >= front) & (rb < front + t_valid)


def _ln_kernel(*refs, alpha, n_y, routed, front, t_valid, tpp, tm):
    h_ref = refs[0]
    y_refs = refs[1:1 + n_y]
    rest = refs[1 + n_y:]
    x = h_ref[...]
    if routed:
        rt_ref, rest = rest[0], rest[1:]
        x = alpha * x + rt_ref[:, 2:3] * y_refs[0][...] + rt_ref[:, 3:4] * y_refs[1][...]
    elif n_y:
        x = alpha * x
        for y_ref in y_refs:
            x = x + y_ref[...]
    g_ref, b_ref, o_ref, ob_ref = rest
    mu = jnp.mean(x, axis=-1, keepdims=True)
    xc = x - mu
    var = jnp.mean(xc * xc, axis=-1, keepdims=True)
    y = xc * lax.rsqrt(var + LN_EPS) * g_ref[...] + b_ref[...]
    row0 = (pl.program_id(0) * tm) % tpp
    y = jnp.where(_valid_rows(row0, tm, front, t_valid), y, 0.0)
    o_ref[...] = y
    ob_ref[...] = y.astype(BF16)


def _add_ln(h, ys, g, b, alpha, lay, route=None):
    front, t_valid, tpp = lay
    n, d = h.shape
    tm = _divisor_tile(tpp, 128, 16)
    spec = pl.BlockSpec((tm, d), lambda i: (i, 0))
    vec = pl.BlockSpec((1, d), lambda i: (0, 0))
    y_specs = [spec] * len(ys)
    extra, extra_specs = (), []
    if route is not None:
        ys = (ys[0], ys[0])
        y_specs = [spec, pl.BlockSpec((tm, d), lambda i: (i + n // tm, 0))]
        extra, extra_specs = (route,), [pl.BlockSpec((tm, LANE), lambda i: (i, 0))]
    kern = functools.partial(_ln_kernel, alpha=alpha, n_y=len(ys), routed=route is not None, front=front,
                             t_valid=t_valid, tpp=tpp, tm=tm)
    return pl.pallas_call(
        kern,
        out_shape=(jax.ShapeDtypeStruct((n, d), F32), jax.ShapeDtypeStruct((n, d), BF16)),
        grid=(n // tm,),
        in_specs=[spec] + y_specs + extra_specs + [vec, vec],
        out_specs=(spec, spec),
        compiler_params=_cparams(("parallel",)),
        name="add_ln",
    )(h, *ys, *extra, g.reshape(1, d), b.reshape(1, d))


def _mm_kernel(*refs, ks):
    a_refs = refs[:len(ks)]
    w_ref, o_ref = refs[len(ks):]
    acc = None
    off = 0
    for a_ref, k in zip(a_refs, ks):
        p = _dot(a_ref[...], w_ref[off:off + k, :])
        acc = p if acc is None else acc + p
        off += k
    o_ref[...] = acc.astype(o_ref.dtype)


def _matmul(a_list, w, out_dtype, tm_cap, tn_cap):
    n = a_list[0].shape[0]
    ks = tuple(a.shape[1] for a in a_list)
    kdim, ndim = w.shape
    assert sum(ks) == kdim
    tm = _divisor_tile(n, tm_cap, 16)
    tn = _divisor_tile(ndim, tn_cap, LANE)
    in_specs = [pl.BlockSpec((tm, k), lambda j, i: (i, 0)) for k in ks]
    in_specs.append(pl.BlockSpec((kdim, tn), lambda j, i: (0, j)))
    return pl.pallas_call(
        functools.partial(_mm_kernel, ks=ks),
        out_shape=jax.ShapeDtypeStruct((n, ndim), out_dtype),
        grid=(ndim // tn, n // tm),
        in_specs=in_specs,
        out_specs=pl.BlockSpec((tm, tn), lambda j, i: (i, j)),
        compiler_params=_cparams(("parallel", "parallel")),
        name="matmul",
    )(*a_list, w)


def _tri_masks(n):
    r = lax.broadcasted_iota(jnp.int32, (n, n), 0)
    c = lax.broadcasted_iota(jnp.int32, (n, n), 1)
    return r, c


def _gated_rms(o, w_row, gate):
    o = o * lax.rsqrt(jnp.mean(o * o, axis=-1, keepdims=True) + NORM_EPS)
    if w_row is not None:
        o = o * w_row
    return o * _silu(gate)


def _prefix_rows(x, row, seg):
    pos = row & (seg - 1)
    s = 1
    while s < seg:
        x = x + jnp.where(pos >= s, pltpu.roll(x, s, 0), 0.0)
        s *= 2
    return x


def _mixer_a_kernel(q_ref, k_ref, v_ref, gt_ref, sm_ref, cw_ref, alog_ref, dtb_ref, nw_ref, o_ref,
                    s_ref, prev_ref, *, n_heads, front, t_valid):
    c = pl.program_id(1)

    @pl.when(c == 0)
    def _():
        s_ref[...] = jnp.zeros_like(s_ref)
        prev_ref[...] = jnp.zeros_like(prev_ref)

    row = lax.broadcasted_iota(jnp.int32, (CHUNK, 1), 0)
    row8 = lax.broadcasted_iota(jnp.int32, (SUBLANES, 1), 0)
    valid = _valid_rows(c * CHUNK, CHUNK, front, t_valid).astype(F32)

    def conv_silu(x_ref, p):
        cur = x_ref[...]
        prev8 = prev_ref[p]
        acc = cur * cw_ref[p, 3:4, :]
        for s in (1, 2, 3):
            rolled = pltpu.roll(cur, s, 0)
            top = jnp.where(row8 >= s, rolled[:SUBLANES], pltpu.roll(prev8, s, 0))
            acc = acc + jnp.concatenate([top, rolled[SUBLANES:]], axis=0) * cw_ref[p, 3 - s:4 - s, :]
        prev_ref[p] = cur[CHUNK - SUBLANES:]
        return _silu(acc)

    q_all = conv_silu(q_ref, 0)
    k_all = conv_silu(k_ref, 1) * valid
    v_all = conv_silu(v_ref, 2) * valid

    sm = sm_ref[...]
    beta_all = _sigmoid(sm) * valid
    g_all = -jnp.exp(alog_ref[...]) * _softplus(sm + dtb_ref[...]) * valid
    gcum_all = _prefix_rows(g_all, row, CHUNK)
    r, cc = _tri_masks(CHUNK)
    causal = r >= cc
    strict = r > cc
    eye = (r == cc).astype(F32)
    gcum_t = lax.dot_general(gcum_all, eye, (((0,), (0,)), ((), ())), preferred_element_type=F32,
                             precision=HI)

    heads = range(n_heads)
    hsl = [slice(h * HEAD_DIM, (h + 1) * HEAD_DIM) for h in heads]
    beta = [beta_all[:, h:h + 1] for h in heads]
    gcol = [gcum_all[:, n_heads + h:n_heads + h + 1] for h in heads]
    decay = [jnp.where(causal, jnp.exp(jnp.minimum(gcol[h] - gcum_t[n_heads + h:n_heads + h + 1, :], 0.0)), 0.0)
             for h in heads]
    eg = [jnp.exp(g) for g in gcol]
    g_last = [g[CHUNK - 1:CHUNK, :] for g in gcol]
    q = [q_all[:, s] * lax.rsqrt(jnp.sum(q_all[:, s] * q_all[:, s], axis=-1, keepdims=True) + NORM_EPS)
         * HEAD_DIM ** -0.5 for s in hsl]
    k = [k_all[:, s] * lax.rsqrt(jnp.sum(k_all[:, s] * k_all[:, s], axis=-1, keepdims=True) + NORM_EPS) for s in hsl]
    kb = [k[h] * beta[h] for h in heads]
    k16 = [x.astype(BF16) for x in k]
    m = [jnp.where(strict, -(_dot_nt(kb[h].astype(BF16), k16[h]) * decay[h]), 0.0) for h in heads]
    attn = [_dot_nt(q[h].astype(BF16), k16[h]) * decay[h] for h in heads]
    inv = [eye + x for x in m]
    mp = m
    for _ in range(5):
        mp = [_dot(x.astype(BF16), x.astype(BF16)) for x in mp]
        inv = [inv[h] + _dot(inv[h].astype(BF16), mp[h].astype(BF16)) for h in heads]
    sol = [_dot(inv[h].astype(BF16),
                jnp.concatenate([v_all[:, hsl[h]] * beta[h], kb[h] * eg[h]], axis=1).astype(BF16)) for h in heads]
    s_old = [s_ref[h] for h in heads]
    s16 = [x.astype(BF16) for x in s_old]
    v_new = [sol[h][:, :HEAD_DIM] - _dot(sol[h][:, HEAD_DIM:].astype(BF16), s16[h]) for h in heads]
    o = [_dot((q[h] * eg[h]).astype(BF16), s16[h]) + _dot(attn[h].astype(BF16), v_new[h].astype(BF16)) for h in heads]
    for h in heads:
        s_ref[h] = s_old[h] * jnp.exp(g_last[h]) + _dot_tn(k[h] * jnp.exp(g_last[h] - gcol[h]), v_new[h])
        o_ref[:, hsl[h]] = _gated_rms(o[h], nw_ref[...], gt_ref[:, hsl[h]]).astype(o_ref.dtype)


def _mixer_b_kernel(q_ref, f_ref, i_ref, gt_ref, lb_ref, nw_ref, o_ref, st_ref, kbuf, bbuf, obuf,
                    *, n_heads, front, t_valid):
    c = pl.program_id(1)

    @pl.when(c == 0)
    def _():
        st_ref[...] = jnp.zeros_like(st_ref)

    row = lax.broadcasted_iota(jnp.int32, (CHUNK, 1), 0)
    valid = _valid_rows(c * CHUNK, CHUNK, front, t_valid).astype(F32)
    row16 = lax.broadcasted_iota(jnp.int32, (SUB, 1), 0)

    lb = lb_ref[...]
    f = lb + (1.0 - lb) * _sigmoid(f_ref[...])
    kbuf[...] = (1.0 - f) * valid
    bbuf[...] = _prefix_rows(jnp.log(f) * valid, row, SUB)

    heads = range(n_heads)
    hsl = [slice(h * HEAD_DIM, (h + 1) * HEAD_DIM) for h in heads]
    st = [st_ref[h] for h in heads]
    for blk in range(CHUNK // SUB):
        rs = slice(blk * SUB, (blk + 1) * SUB)
        for h in heads:
            hs = hsl[h]
            b16 = bbuf[rs, hs]
            q16 = _silu(q_ref[rs, hs])
            b_end = b16[SUB - 1:SUB, :]
            acc = _dot_nt((q16 * jnp.exp(b16)).astype(BF16), st[h].astype(BF16))
            for j in range(SUB):
                jr = pl.ds(blk * SUB + j, 1)
                e = jnp.exp(jnp.minimum(b16 - bbuf[jr, hs], 0.0))
                sc = jnp.sum(q16 * kbuf[jr, hs] * e, axis=-1, keepdims=True)
                acc = acc + jnp.where(row16 >= j, sc, 0.0) * i_ref[jr, hs]
            st[h] = st[h] * jnp.exp(b_end) + _dot_tn(i_ref[rs, hs], kbuf[rs, hs] * jnp.exp(b_end - b16))
            obuf[rs, hs] = acc
    for h in heads:
        st_ref[h] = st[h]
        o_ref[:, hsl[h]] = _gated_rms(obuf[:, hsl[h]], nw_ref[...], gt_ref[:, hsl[h]]).astype(o_ref.dtype)


def _rope_partial(x, cos, sin_lo, sin_hi):
    half = ROPE_DIMS // 2
    return x * cos + pltpu.roll(x, LANE - half, 1) * sin_lo + pltpu.roll(x, half, 1) * sin_hi


def _rope_c_kernel(q_ref, k_ref, v_ref, cos_ref, slo_ref, shi_ref, qo_ref, ko_ref, vo_ref, *, n_maps):
    cos, slo, shi = cos_ref[...], slo_ref[...], shi_ref[...]
    for mp in range(n_maps):
        ms = slice(mp * HEAD_DIM, (mp + 1) * HEAD_DIM)
        qo_ref[:, ms] = (_rope_partial(q_ref[:, ms], cos, slo, shi) * HEAD_DIM ** -0.5).astype(BF16)
        ko_ref[:, ms] = _rope_partial(k_ref[:, ms], cos, slo, shi).astype(BF16)
    vo_ref[...] = v_ref[...].astype(BF16)


def _mixer_c_kernel(qi_ref, kj_ref, q_ref, k_ref, v_ref, lam_ref, nw_ref, o_ref, m_ref, l_ref, acc_ref,
                    *, tq, front, out_scale):
    p = pl.program_id(2)
    i = qi_ref[p]
    j = kj_ref[p]

    @pl.when(j == 0)
    def _():
        m_ref[...] = jnp.full_like(m_ref, NEG)
        l_ref[...] = jnp.zeros_like(l_ref)
        acc_ref[...] = jnp.zeros_like(acc_ref)

    def step(masked):
        v = v_ref[...]
        if masked:
            rq = i * tq + lax.broadcasted_iota(jnp.int32, (tq, 1), 0)
            rk = j * tq + lax.broadcasted_iota(jnp.int32, (1, tq), 1)
            msk = (rk <= rq) & (rk >= front)
        for mp in range(2):
            ms = slice(mp * HEAD_DIM, (mp + 1) * HEAD_DIM)
            s = _dot_nt(q_ref[:, ms], k_ref[:, ms])
            if masked:
                s = jnp.where(msk, s, NEG)
            m_prev = m_ref[mp]
            m_new = jnp.maximum(m_prev, jnp.max(s, axis=-1, keepdims=True))
            pr = jnp.exp(s - m_new)
            a = jnp.exp(m_prev - m_new)
            l_ref[mp] = a * l_ref[mp] + jnp.sum(pr, axis=-1, keepdims=True)
            acc_ref[mp] = a * acc_ref[mp] + _dot(pr.astype(BF16), v)
            m_ref[mp] = m_new

    needs_mask = (j == i) | (j == 0)

    @pl.when(needs_mask)
    def _():
        step(True)

    @pl.when(jnp.logical_not(needs_mask))
    def _():
        step(False)

    @pl.when(j == i)
    def _():
        o = acc_ref[0] / l_ref[0] - lam_ref[...] * (acc_ref[1] / l_ref[1])
        o = o * lax.rsqrt(jnp.mean(o * o, axis=-1, keepdims=True) + NORM_EPS) * nw_ref[...] * out_scale
        o_ref[...] = o.astype(o_ref.dtype)


def _mixer_d_kernel(q_ref, k_ref, v_ref, gt_ref, cos_ref, sin_ref, dm_ref, qd_ref, kd_ref, cd_ref, o_ref, s_ref,
                    *, n_heads):
    c = pl.program_id(1)

    @pl.when(c == 0)
    def _():
        s_ref[...] = jnp.zeros_like(s_ref)

    cos = cos_ref[...]
    sin = sin_ref[...]
    heads = range(n_heads)
    hsl = [slice(h * HEAD_DIM, (h + 1) * HEAD_DIM) for h in heads]
    q = [q_ref[:, s] * cos + pltpu.roll(q_ref[:, s], HEAD_DIM // 2, 1) * sin for s in hsl]
    k = [(k_ref[:, s] * cos + pltpu.roll(k_ref[:, s], HEAD_DIM // 2, 1) * sin) * HEAD_DIM ** -0.5 for s in hsl]
    s_old = [s_ref[h] for h in heads]
    a = [_dot_nt(q[h].astype(BF16), k[h].astype(BF16)) * dm_ref[h] for h in heads]
    o = [_dot(a[h].astype(BF16), v_ref[:, hsl[h]].astype(BF16))
         + _dot((q[h] * qd_ref[h]).astype(BF16), s_old[h].astype(BF16)) for h in heads]
    for h in heads:
        s_ref[h] = cd_ref[h] * s_old[h] + _dot_tn(k[h] * kd_ref[h], v_ref[:, hsl[h]])
        o_ref[:, hsl[h]] = _gated_rms(o[h], None, gt_ref[:, hsl[h]]).astype(o_ref.dtype)


def _router_kernel(h_ref, w_ref, b_ref, o_ref):
    n_e = N_GROUPS * EXP_PER_GROUP
    lg = _dot_hi(h_ref[...], w_ref[...]) + b_ref[...]
    lane_i = lax.broadcasted_iota(jnp.int32, lg.shape, 1)
    lane = lane_i.astype(F32)
    grp = (lane_i >> GROUP_SHIFT).astype(F32)
    big = float(4 * LANE)
    is_g = (lane_i >= n_e) & (lane_i < n_e + N_GROUPS)
    gl = jnp.where(is_g, lg, NEG)
    gmax = jnp.max(gl, axis=-1, keepdims=True)
    gsum = jnp.sum(jnp.where(is_g, jnp.exp(gl - gmax), 0.0), axis=-1, keepdims=True)
    g_w = 1.0 / gsum
    g_idx = jnp.min(jnp.where(is_g & (gl == gmax), lane - n_e, big), axis=-1, keepdims=True)
    in_grp = (lane_i < n_e) & (grp == g_idx)
    el = jnp.where(in_grp, lg, NEG)
    emax = jnp.max(el, axis=-1, keepdims=True)
    eexp = jnp.where(in_grp, jnp.exp(el - emax), 0.0)
    pe = eexp / jnp.sum(eexp, axis=-1, keepdims=True)
    p1 = jnp.max(jnp.where(in_grp, pe, -1.0), axis=-1, keepdims=True)
    i1 = jnp.min(jnp.where(in_grp & (pe == p1), lane, big), axis=-1, keepdims=True)
    rest = in_grp & (lane != i1)
    p2 = jnp.max(jnp.where(rest, pe, -1.0), axis=-1, keepdims=True)
    i2 = jnp.min(jnp.where(rest & (pe == p2), lane, big), axis=-1, keepdims=True)
    den = p1 + p2
    o_ref[...] = (jnp.where(lane_i == 0, i1, 0.0) + jnp.where(lane_i == 1, i2, 0.0)
                  + jnp.where(lane_i == 2, g_w * (p1 / den), 0.0) + jnp.where(lane_i == 3, g_w * (p2 / den), 0.0))


def _router(h, w_r, b_r, tpp):
    n, d = h.shape
    tm = _divisor_tile(tpp, 256, 8)
    return pl.pallas_call(
        _router_kernel,
        out_shape=jax.ShapeDtypeStruct((n, LANE), F32),
        grid=(n // tm,),
        in_specs=[pl.BlockSpec((tm, d), lambda i: (i, 0)),
                  pl.BlockSpec((d, LANE), lambda i: (0, 0)),
                  pl.BlockSpec((1, LANE), lambda i: (0, 0))],
        out_specs=pl.BlockSpec((tm, LANE), lambda i: (i, 0)),
        compiler_params=_cparams(("parallel",)),
        name="router",
    )(h, w_r, b_r)


def _row_copy(src, src_row, dst, dst_row, sem):
    return pltpu.make_async_copy(src.at[pl.ds(src_row, 1)], dst.at[pl.ds(dst_row, 1)], sem)


def _moe_kernel(te_ref, cnt_ref, first_ref, src_ref, srcn_ref, dst_ref, x_hbm, wg_ref, wu_ref, wd_ref, y_hbm,
                xbuf, ybuf, wgb, wub, wdb, sem_in, sem_out, *, tm, n_tiles):
    del te_ref
    t = pl.program_id(0)
    slot = t % 2
    cnt = cnt_ref[t]

    def gather_start(idx_ref, s):
        def body(r, c):
            _row_copy(x_hbm, idx_ref[0, 0, r], xbuf.at[s], r, sem_in.at[s]).start()
            return c
        lax.fori_loop(0, tm, body, 0, unroll=8)

    def gather_wait(s):
        def body(r, c):
            _row_copy(x_hbm, r, xbuf.at[s], r, sem_in.at[s]).wait()
            return c
        lax.fori_loop(0, tm, body, 0, unroll=8)

    def scatter_wait(s, rows):
        def body(r, c):
            _row_copy(ybuf.at[s], r, y_hbm, r, sem_out.at[s]).wait()
            return c
        lax.fori_loop(0, rows, body, 0)

    @pl.when((t == 0) & (cnt > 0))
    def _():
        gather_start(src_ref, 0)

    nxt = jnp.minimum(t + 1, n_tiles - 1)

    @pl.when((t + 1 < n_tiles) & (cnt_ref[nxt] > 0))
    def _():
        gather_start(srcn_ref, 1 - slot)

    old = jnp.maximum(t - 2, 0)

    @pl.when((t >= 2) & (cnt_ref[old] > 0))
    def _():
        scatter_wait(slot, cnt_ref[old])

    @pl.when(cnt > 0)
    def _():
        @pl.when(first_ref[t] == 1)
        def _():
            wgb[...] = wg_ref[...].astype(BF16)
            wub[...] = wu_ref[...].astype(BF16)
            wdb[...] = wd_ref[...].astype(BF16)

        gather_wait(slot)
        x = xbuf[slot].astype(BF16)
        hg = _dot(x, wgb[...])
        hu = _dot(x, wub[...])
        act = (_silu(hg) * hu).astype(BF16)
        ybuf[slot] = _dot(act, wdb[...])

        def body(r, c):
            _row_copy(ybuf.at[slot], r, y_hbm, dst_ref[0, 0, r], sem_out.at[slot]).start()
            return c
        lax.fori_loop(0, cnt, body, 0)

    @pl.when(t == n_tiles - 1)
    def _():
        prev = jnp.maximum(t - 1, 0)

        @pl.when((t >= 1) & (cnt_ref[prev] > 0))
        def _():
            scatter_wait(1 - slot, cnt_ref[prev])

        @pl.when(cnt > 0)
        def _():
            scatter_wait(slot, cnt)


def _dispatch(route, n, tm, n_e):
    ids = jnp.concatenate([route[:, 0], route[:, 1]]).astype(jnp.int32)
    n_pairs = 2 * n
    n_tiles = n_pairs // tm + n_e
    order = jnp.argsort(ids, stable=True).astype(jnp.int32)
    ids_s = ids[order]
    counts = jnp.sum((ids[:, None] == jnp.arange(n_e, dtype=jnp.int32)[None, :]).astype(jnp.int32), axis=0)
    tiles_e = (counts + tm - 1) // tm
    tile_end = jnp.cumsum(tiles_e)
    tile_start = tile_end - tiles_e
    first = jnp.cumsum(counts) - counts
    pos = tile_start[ids_s] * tm + (jnp.arange(n_pairs, dtype=jnp.int32) - first[ids_s])
    src = jnp.zeros((n_tiles * tm,), jnp.int32).at[pos].set(order % n)
    dst = jnp.zeros((n_tiles * tm,), jnp.int32).at[pos].set(order)
    tile = jnp.arange(n_tiles, dtype=jnp.int32)
    te = jnp.minimum(jnp.searchsorted(tile_end, tile, side="right").astype(jnp.int32), n_e - 1)
    cnt = jnp.clip(counts[te] - (tile - tile_start[te]) * tm, 0, tm)
    cnt = jnp.where(tile < tile_end[n_e - 1], cnt, 0).astype(jnp.int32)
    te = jnp.where(cnt > 0, te, jnp.max(jnp.where(cnt > 0, te, 0)))
    new_expert = jnp.concatenate([jnp.ones((1,), jnp.int32), (te[1:] != te[:-1]).astype(jnp.int32)])
    return te, cnt, new_expert, src.reshape(n_tiles, 1, tm), dst.reshape(n_tiles, 1, tm)


def _moe_routed(h, route, wg, wu, wd, layer, tm):
    n, d = h.shape
    _, n_e, _, f = wg.shape
    te, cnt, new_expert, src, dst = _dispatch(route, n, tm, n_e)
    n_tiles = te.shape[0]

    def idx_spec(shift):
        return pl.BlockSpec((1, 1, tm), lambda t, *_: (jnp.minimum(t + shift, n_tiles - 1), 0, 0),
                            memory_space=pltpu.SMEM)

    grid_spec = pltpu.PrefetchScalarGridSpec(
        num_scalar_prefetch=3, grid=(n_tiles,),
        in_specs=[idx_spec(0), idx_spec(1), idx_spec(0),
                  pl.BlockSpec(memory_space=pl.ANY),
                  pl.BlockSpec((None, None, d, f), lambda t, te, *_: (layer, te[t], 0, 0)),
                  pl.BlockSpec((None, None, d, f), lambda t, te, *_: (layer, te[t], 0, 0)),
                  pl.BlockSpec((None, None, f, d), lambda t, te, *_: (layer, te[t], 0, 0))],
        out_specs=pl.BlockSpec(memory_space=pl.ANY),
        scratch_shapes=[pltpu.VMEM((2, tm, d), F32), pltpu.VMEM((2, tm, d), F32),
                        pltpu.VMEM((d, f), BF16), pltpu.VMEM((d, f), BF16), pltpu.VMEM((f, d), BF16),
                        pltpu.SemaphoreType.DMA((2,)), pltpu.SemaphoreType.DMA((2,))])
    return pl.pallas_call(
        functools.partial(_moe_kernel, tm=tm, n_tiles=n_tiles),
        out_shape=jax.ShapeDtypeStruct((2 * n, d), F32),
        grid_spec=grid_spec,
        compiler_params=_cparams(("arbitrary",)),
        name="moe_routed",
    )(te, cnt, new_expert, src, src, dst, h, wg, wu, wd)


def _mixers(z, lay, bsz, w, prm):
    front, t_valid, tpp = lay
    n = z.shape[0]
    nc = tpp // CHUNK
    heads = w // HEAD_DIM
    c_heads = w // (2 * HEAD_DIM)
    small_blk = 15 * w // LANE

    def cspec(region):
        return pl.BlockSpec((CHUNK, w), lambda b, c, region=region: (b * nc + c, region))

    def fixed(shape):
        nd = len(shape)
        return pl.BlockSpec(shape, lambda b, c: (0,) * nd)

    ospec = pl.BlockSpec((CHUNK, w), lambda b, c: (b * nc + c, 0))
    oshape = jax.ShapeDtypeStruct((n, w), BF16)
    grid = (bsz, nc)
    sem = ("parallel", "arbitrary")
    state = pltpu.VMEM((heads, HEAD_DIM, HEAD_DIM), F32)

    oa = pl.pallas_call(
        functools.partial(_mixer_a_kernel, n_heads=heads, front=front, t_valid=t_valid),
        out_shape=oshape, grid=grid,
        in_specs=[cspec(0), cspec(1), cspec(2), cspec(3),
                  pl.BlockSpec((CHUNK, LANE), lambda b, c: (b * nc + c, small_blk)),
                  fixed((3, 4, w)), fixed((1, LANE)), fixed((1, LANE)), fixed((1, HEAD_DIM))],
        out_specs=ospec,
        scratch_shapes=[state, pltpu.VMEM((3, SUBLANES, w), F32)],
        compiler_params=_cparams(sem), name="mixer_a",
    )(z, z, z, z, z, prm["conv_w"], prm["a_log_row"], prm["dt_bias_row"], prm["norm_a"])

    ob = pl.pallas_call(
        functools.partial(_mixer_b_kernel, n_heads=heads, front=front, t_valid=t_valid),
        out_shape=oshape, grid=grid,
        in_specs=[cspec(4), cspec(5), cspec(6), cspec(7), fixed((1, w)), fixed((1, HEAD_DIM))],
        out_specs=ospec,
        scratch_shapes=[state] + [pltpu.VMEM((CHUNK, w), F32)] * 3,
        compiler_params=_cparams(sem), name="mixer_b",
    )(z, z, z, z, prm["lb"], prm["norm_b"])

    tq = _divisor_tile(tpp, 384, LANE)
    nq = tpp // tq
    vw = 2 * HEAD_DIM
    rc = prm["rope_c"]
    rspec = pl.BlockSpec((tq, w), lambda i: (i, 0))
    tab = pl.BlockSpec((tq, HEAD_DIM), lambda i: (i % nq, 0))
    qr, kr, vr = pl.pallas_call(
        functools.partial(_rope_c_kernel, n_maps=heads),
        out_shape=(oshape, oshape, oshape), grid=(n // tq,),
        in_specs=[pl.BlockSpec((tq, w), lambda i: (i, 8)), pl.BlockSpec((tq, w), lambda i: (i, 9)),
                  pl.BlockSpec((tq, w), lambda i: (i, 10)), tab, tab, tab],
        out_specs=(rspec, rspec, rspec),
        compiler_params=_cparams(("parallel",)), name="rope_c",
    )(z, z, z, rc[0], rc[1], rc[2])

    pairs = [(i, j) for i in range(nq) for j in range(i + 1)]
    qi = jnp.asarray([p[0] for p in pairs], jnp.int32)
    kj = jnp.asarray([p[1] for p in pairs], jnp.int32)
    qspec = pl.BlockSpec((tq, vw), lambda b, h, p, qi, kj: (b * nq + qi[p], h))
    kspec = pl.BlockSpec((tq, vw), lambda b, h, p, qi, kj: (b * nq + kj[p], h))
    row_vw = pl.BlockSpec((1, vw), lambda b, h, p, qi, kj: (0, 0))
    oc = pl.pallas_call(
        functools.partial(_mixer_c_kernel, tq=tq, front=front, out_scale=prm["c_out_scale"]),
        out_shape=oshape,
        grid_spec=pltpu.PrefetchScalarGridSpec(
            num_scalar_prefetch=2, grid=(bsz, c_heads, len(pairs)),
            in_specs=[qspec, kspec, kspec, row_vw, row_vw],
            out_specs=qspec,
            scratch_shapes=[pltpu.VMEM((2, tq, 1), F32), pltpu.VMEM((2, tq, 1), F32), pltpu.VMEM((2, tq, vw), F32)]),
        compiler_params=_cparams(("parallel", "parallel", "arbitrary")), name="mixer_c",
    )(qi, kj, qr, kr, vr, prm["lam_row"], prm["subln_c"])

    tabd = pl.BlockSpec((CHUNK, HEAD_DIM), lambda b, c: (c, 0))
    rd = prm["rope_d"]
    od = pl.pallas_call(
        functools.partial(_mixer_d_kernel, n_heads=heads),
        out_shape=oshape, grid=grid,
        in_specs=[cspec(11), cspec(12), cspec(13), cspec(14), tabd, tabd,
                  fixed((heads, CHUNK, CHUNK)), fixed((heads, CHUNK, HEAD_DIM)), fixed((heads, CHUNK, HEAD_DIM)),
                  fixed((heads, 1, HEAD_DIM))],
        out_specs=ospec,
        scratch_shapes=[state],
        compiler_params=_cparams(sem), name="mixer_d",
    )(z, z, z, z, rd[0], rd[1], prm["ret_dmask"], prm["ret_qdec"], prm["ret_kdec"], prm["ret_cdec"])
    return oa, ob, oc, od


def _permute_in_proj(w_in_l, w):
    d = w_in_l.shape[0]
    n_small = w_in_l.shape[1] - 15 * w
    pad = jnp.zeros((d, LANE - n_small), w_in_l.dtype)
    return jnp.concatenate([w_in_l[:, :4 * w], w_in_l[:, 4 * w + n_small:], w_in_l[:, 4 * w:4 * w + n_small], pad],
                           axis=1).astype(BF16)


def _rope_tables(front, tpp):
    pos = (jnp.arange(tpp) - front).astype(F32)[:, None]
    half = ROPE_DIMS // 2
    inv = 1.0 / (ROPE_THETA ** (jnp.arange(half, dtype=F32) / half))
    ang = pos * inv[None, :]
    z_rest = jnp.zeros((tpp, HEAD_DIM - ROPE_DIMS), F32)
    z_half = jnp.zeros((tpp, half), F32)
    cos_c = jnp.concatenate([jnp.cos(ang), jnp.cos(ang), jnp.ones_like(z_rest)], axis=1)
    sin_lo = jnp.concatenate([-jnp.sin(ang), z_half, z_rest], axis=1)
    sin_hi = jnp.concatenate([z_half, jnp.sin(ang), z_rest], axis=1)
    half_d = HEAD_DIM // 2
    inv_d = 1.0 / (RET_THETA ** (jnp.arange(half_d, dtype=F32) / half_d))
    ang_d = pos * inv_d[None, :]
    cos_d = jnp.concatenate([jnp.cos(ang_d), jnp.cos(ang_d)], axis=1)
    sin_d = jnp.concatenate([-jnp.sin(ang_d), jnp.sin(ang_d)], axis=1)
    return (cos_c, sin_lo, sin_hi), (cos_d, sin_d)


def _retention_tables(n_heads):
    lg = jnp.log(1.0 - 2.0 ** (-5.0 - jnp.arange(n_heads, dtype=F32)))
    idx = jnp.arange(CHUNK, dtype=F32)
    rel = idx[:, None] - idx[None, :]
    causal = idx[:, None] >= idx[None, :]
    dmask = jnp.exp(jnp.where(causal[None], rel[None] * lg[:, None, None], -jnp.inf))
    qdec = jnp.exp((idx[None, :] + 1.0) * lg[:, None])
    kdec = jnp.exp((CHUNK - 1.0 - idx[None, :]) * lg[:, None])
    cdec = jnp.exp(CHUNK * lg)
    bc = lambda t: jnp.broadcast_to(t[..., None], t.shape + (HEAD_DIM,))
    return dmask, bc(qdec), bc(kdec), bc(cdec[:, None])


def _row(v, width=LANE, offset=0):
    out = jnp.zeros((1, width), F32)
    return out.at[0, offset:offset + v.shape[0]].set(v.astype(F32))


def kernel(x, meta_tokens, emb_ln_g, emb_ln_b, w_in, conv_a, a_log, dt_bias, norm_a, hgrn_lb, norm_b, lam_q1, lam_k1,
           lam_q2, lam_k2, subln_c, w_out, ln1_g, ln1_b, w_rg, b_rg, w_re, b_re, w_gate, w_up, w_down, ln2_g, ln2_b):
    bsz, seq, d = x.shape
    depth = w_in.shape[0]
    n_meta = meta_tokens.shape[0]
    w = d // 4
    a_heads = w // HEAD_DIM
    t_valid = n_meta + seq
    front = (-t_valid) % CHUNK
    tpp = -(-(front + t_valid) // ROW_ALIGN) * ROW_ALIGN
    lay = (front, t_valid, tpp)
    alpha = (2 * depth) ** 0.25

    meta = jnp.broadcast_to(meta_tokens[None].astype(x.dtype), (bsz, n_meta, d))
    xp = jnp.concatenate([jnp.zeros((bsz, front, d), x.dtype), meta, x,
                          jnp.zeros((bsz, tpp - front - t_valid, d), x.dtype)], axis=1).reshape(bsz * tpp, d)
    h, hb = _add_ln(xp, (), emb_ln_g, emb_ln_b, 1.0, lay)

    lbs = jax.nn.softmax(hgrn_lb.astype(F32), axis=0)
    lbs = jnp.cumsum(lbs, axis=0) - lbs[0:1]
    rope_c, rope_d = _rope_tables(front, tpp)
    dmask, qdec, kdec, cdec = _retention_tables(a_heads)

    for l in range(depth):
        lam_init = 0.8 - 0.6 * math.exp(-0.3 * l)
        lam = (jnp.exp(jnp.sum(lam_q1[l].astype(F32) * lam_k1[l].astype(F32)))
               - jnp.exp(jnp.sum(lam_q2[l].astype(F32) * lam_k2[l].astype(F32))) + lam_init)
        prm = {
            "conv_w": conv_a[l].astype(F32).reshape(3, w, -1).transpose(0, 2, 1),
            "a_log_row": _row(a_log[l], offset=a_heads),
            "dt_bias_row": _row(dt_bias[l], offset=a_heads),
            "norm_a": norm_a[l].astype(F32).reshape(1, HEAD_DIM),
            "lb": lbs[l].reshape(1, w),
            "norm_b": norm_b[l].astype(F32).reshape(1, HEAD_DIM),
            "rope_c": rope_c, "rope_d": rope_d,
            "lam_row": jnp.full((1, 2 * HEAD_DIM), lam, F32),
            "subln_c": subln_c[l].astype(F32).reshape(1, 2 * HEAD_DIM),
            "c_out_scale": 1.0 - lam_init,
            "ret_dmask": dmask, "ret_qdec": qdec, "ret_kdec": kdec, "ret_cdec": cdec,
        }
        z = _matmul([hb], _permute_in_proj(w_in[l], w), F32, 768, 1536)
        o_parts = _mixers(z, lay, bsz, w, prm)
        y = _matmul(list(o_parts), w_out[l].astype(BF16), F32, 768, 1024)
        h, hb = _add_ln(h, (y,), ln1_g[l], ln1_b[l], alpha, lay)

        n_e = w_re.shape[2]
        w_r = jnp.concatenate([w_re[l], w_rg[l], jnp.zeros((d, LANE - n_e - N_GROUPS), F32)], axis=1).astype(F32)
        b_r = jnp.concatenate([b_re[l], b_rg[l], jnp.zeros((LANE - n_e - N_GROUPS,), F32)]).reshape(1, LANE)
        route = _router(h, w_r, b_r, tpp)
        y2 = _moe_routed(h, route, w_gate, w_up, w_down, l, MOE_TILE)
        h, hb = _add_ln(h, (y2,), ln2_g[l], ln2_b[l], alpha, lay, route=route)

    return h.reshape(bsz, tpp, d)[:, front + n_meta:front + t_valid]
```

```python
import functools
import math

import jax
import jax.numpy as jnp
from jax import lax
from jax.experimental import pallas as pl
from jax.experimental.pallas import tpu as pltpu

HEAD_DIM = 128
CHUNK = 64
SUB = 16
GROUP_SHIFT = 3
ROPE_THETA = 500000.0
ROPE_DIMS = HEAD_DIM // 4
RET_THETA = 10000.0
N_GROUPS = 8
EXP_PER_GROUP = 8
NORM_EPS = 1e-6
LN_EPS = 1e-5
LANE = 128
SUBLANES = 8
ROW_ALIGN = 2 * CHUNK
V7X_VMEM_LIMIT_BYTES = 56 * 1024 * 1024
MOE_TILE = 256
NEG = -1e30

F32 = jnp.float32
BF16 = jnp.bfloat16
HI = lax.Precision.HIGHEST


def _cparams(sem):
    return pltpu.CompilerParams(dimension_semantics=sem, vmem_limit_bytes=V7X_VMEM_LIMIT_BYTES)


def _divisor_tile(n, cap, align):
    best = None
    for t in range(align, min(n, cap) + 1, align):
        if n % t == 0:
            best = t
    assert best is not None, (n, cap, align)
    return best


def _dot(a, b):
    return jnp.dot(a, b, preferred_element_type=F32)


def _dot_hi(a, b):
    return jnp.dot(a, b, preferred_element_type=F32, precision=HI)


def _dot_nt(a, b):
    return lax.dot_general(a, b, (((1,), (1,)), ((), ())), preferred_element_type=F32)


def _dot_tn(a, b):
    return lax.dot_general(a, b, (((0,), (0,)), ((), ())), preferred_element_type=F32)


def _sigmoid(x):
    return 1.0 / (1.0 + jnp.exp(-x))


def _silu(x):
    return x * _sigmoid(x)


def _softplus(x):
    return jnp.maximum(x, 0.0) + jnp.log(1.0 + jnp.exp(-jnp.abs(x)))


def _valid_rows(row0, n, front, t_valid):
    rb = row0 + lax.broadcasted_iota(jnp.int32, (n, 1), 0)
    return (rb >= front) & (rb < front + t_valid)


def _ln_kernel(*refs, alpha, n_y, routed, front, t_valid, tpp, tm):
    h_ref = refs[0]
    y_refs = refs[1:1 + n_y]
    rest = refs[1 + n_y:]
    x = h_ref[...]
    if routed:
        rt_ref, rest = rest[0], rest[1:]
        x = alpha * x + rt_ref[:, 2:3] * y_refs[0][...] + rt_ref[:, 3:4] * y_refs[1][...]
    elif n_y:
        x = alpha * x
        for y_ref in y_refs:
            x = x + y_ref[...]
    g_ref, b_ref, o_ref, ob_ref = rest
    mu = jnp.mean(x, axis=-1, keepdims=True)
    xc = x - mu
    var = jnp.mean(xc * xc, axis=-1, keepdims=True)
    y = xc * lax.rsqrt(var + LN_EPS) * g_ref[...] + b_ref[...]
    row0 = (pl.program_id(0) * tm) % tpp
    y = jnp.where(_valid_rows(row0, tm, front, t_valid), y, 0.0)
    o_ref[...] = y
    ob_ref[...] = y.astype(BF16)


def _add_ln(h, ys, g, b, alpha, lay, route=None):
    front, t_valid, tpp = lay
    n, d = h.shape
    tm = _divisor_tile(tpp, 128, 16)
    spec = pl.BlockSpec((tm, d), lambda i: (i, 0))
    vec = pl.BlockSpec((1, d), lambda i: (0, 0))
    y_specs = [spec] * len(ys)
    extra, extra_specs = (), []
    if route is not None:
        ys = (ys[0], ys[0])
        y_specs = [spec, pl.BlockSpec((tm, d), lambda i: (i + n // tm, 0))]
        extra, extra_specs = (route,), [pl.BlockSpec((tm, LANE), lambda i: (i, 0))]
    kern = functools.partial(_ln_kernel, alpha=alpha, n_y=len(ys), routed=route is not None, front=front,
                             t_valid=t_valid, tpp=tpp, tm=tm)
    return pl.pallas_call(
        kern,
        out_shape=(jax.ShapeDtypeStruct((n, d), F32), jax.ShapeDtypeStruct((n, d), BF16)),
        grid=(n // tm,),
        in_specs=[spec] + y_specs + extra_specs + [vec, vec],
        out_specs=(spec, spec),
        compiler_params=_cparams(("parallel",)),
        name="add_ln",
    )(h, *ys, *extra, g.reshape(1, d), b.reshape(1, d))


def _mm_kernel(*refs, ks):
    a_refs = refs[:len(ks)]
    w_ref, o_ref = refs[len(ks):]
    acc = None
    off = 0
    for a_ref, k in zip(a_refs, ks):
        p = _dot(a_ref[...], w_ref[off:off + k, :])
        acc = p if acc is None else acc + p
        off += k
    o_ref[...] = acc.astype(o_ref.dtype)


def _matmul(a_list, w, out_dtype, tm_cap, tn_cap):
    n = a_list[0].shape[0]
    ks = tuple(a.shape[1] for a in a_list)
    kdim, ndim = w.shape
    assert sum(ks) == kdim
    tm = _divisor_tile(n, tm_cap, 16)
    tn = _divisor_tile(ndim, tn_cap, LANE)
    in_specs = [pl.BlockSpec((tm, k), lambda j, i: (i, 0)) for k in ks]
    in_specs.append(pl.BlockSpec((kdim, tn), lambda j, i: (0, j)))
    return pl.pallas_call(
        functools.partial(_mm_kernel, ks=ks),
        out_shape=jax.ShapeDtypeStruct((n, ndim), out_dtype),
        grid=(ndim // tn, n // tm),
        in_specs=in_specs,
        out_specs=pl.BlockSpec((tm, tn), lambda j, i: (i, j)),
        compiler_params=_cparams(("parallel", "parallel")),
        name="matmul",
    )(*a_list, w)


def _in_proj_kernel(a_ref, w_ref, wn_ref, o_ref, wb_ref, *, n_plain, shift, rows_per_cast):
    j = pl.program_id(0)

    @pl.when(pl.program_id(1) == 0)
    def _():
        kdim = w_ref.shape[0]
        for r0 in range(0, kdim, rows_per_cast):
            rs = slice(r0, r0 + rows_per_cast)

            @pl.when(j < n_plain)
            def _():
                wb_ref[rs, :] = w_ref[rs, :].astype(BF16)

            @pl.when(j >= n_plain)
            def _():
                wb_ref[rs, :] = jnp.concatenate([w_ref[rs, shift:], wn_ref[rs, :shift]], axis=1).astype(BF16)

    o_ref[...] = _dot(a_ref[...], wb_ref[...])


def _small_proj_kernel(a_ref, w_ref, o_ref):
    o_ref[...] = _dot(a_ref[...], w_ref[...].astype(BF16))


def _in_proj(hb, w_in, layer, w):
    n, d = hb.shape
    n_small = w_in.shape[2] - 15 * w
    tm = _divisor_tile(n, 768, 16)
    tn = _divisor_tile(w, 512, LANE)
    z = pl.pallas_call(
        functools.partial(_in_proj_kernel, n_plain=4 * w // tn, shift=n_small, rows_per_cast=min(d, 512)),
        out_shape=jax.ShapeDtypeStruct((n, 15 * w), F32),
        grid=(15 * w // tn, n // tm),
        in_specs=[pl.BlockSpec((tm, d), lambda j, i: (i, 0)),
                  pl.BlockSpec((None, d, tn), lambda j, i: (layer, 0, j)),
                  pl.BlockSpec((None, d, LANE), lambda j, i: (layer, 0, (j + 1) * (tn // LANE)))],
        out_specs=pl.BlockSpec((tm, tn), lambda j, i: (i, j)),
        scratch_shapes=[pltpu.VMEM((d, tn), BF16)],
        compiler_params=_cparams(("parallel", "arbitrary")),
        name="in_proj",
    )(hb, w_in, w_in)
    zs = pl.pallas_call(
        _small_proj_kernel,
        out_shape=jax.ShapeDtypeStruct((n, LANE), F32),
        grid=(n // tm,),
        in_specs=[pl.BlockSpec((tm, d), lambda i: (i, 0)),
                  pl.BlockSpec((None, d, LANE), lambda i: (layer, 0, 4 * w // LANE))],
        out_specs=pl.BlockSpec((tm, LANE), lambda i: (i, 0)),
        compiler_params=_cparams(("parallel",)),
        name="small_proj",
    )(hb, w_in)
    return z, zs


def _tri_masks(n):
    r = lax.broadcasted_iota(jnp.int32, (n, n), 0)
    c = lax.broadcasted_iota(jnp.int32, (n, n), 1)
    return r, c


def _gated_rms(o, w_row, gate):
    o = o * lax.rsqrt(jnp.mean(o * o, axis=-1, keepdims=True) + NORM_EPS)
    if w_row is not None:
        o = o * w_row
    return o * _silu(gate)


def _prefix_rows(x, row, seg):
    pos = row & (seg - 1)
    s = 1
    while s < seg:
        x = x + jnp.where(pos >= s, pltpu.roll(x, s, 0), 0.0)
        s *= 2
    return x


def _mixer_a_kernel(q_ref, k_ref, v_ref, gt_ref, sm_ref, cw_ref, alog_ref, dtb_ref, nw_ref, o_ref,
                    s_ref, prev_ref, *, n_heads, front, t_valid):
    c = pl.program_id(1)

    @pl.when(c == 0)
    def _():
        s_ref[...] = jnp.zeros_like(s_ref)
        prev_ref[...] = jnp.zeros_like(prev_ref)

    row = lax.broadcasted_iota(jnp.int32, (CHUNK, 1), 0)
    row8 = lax.broadcasted_iota(jnp.int32, (SUBLANES, 1), 0)
    valid = _valid_rows(c * CHUNK, CHUNK, front, t_valid).astype(F32)

    def conv_silu(x_ref, p):
        cur = x_ref[...]
        prev8 = prev_ref[p]
        acc = cur * cw_ref[p, 3:4, :]
        for s in (1, 2, 3):
            rolled = pltpu.roll(cur, s, 0)
            top = jnp.where(row8 >= s, rolled[:SUBLANES], pltpu.roll(prev8, s, 0))
            acc = acc + jnp.concatenate([top, rolled[SUBLANES:]], axis=0) * cw_ref[p, 3 - s:4 - s, :]
        prev_ref[p] = cur[CHUNK - SUBLANES:]
        return _silu(acc)

    q_all = conv_silu(q_ref, 0)
    k_all = conv_silu(k_ref, 1) * valid
    v_all = conv_silu(v_ref, 2) * valid

    sm = sm_ref[...]
    beta_all = _sigmoid(sm) * valid
    g_all = -jnp.exp(alog_ref[...]) * _softplus(sm + dtb_ref[...]) * valid
    gcum_all = _prefix_rows(g_all, row, CHUNK)
    r, cc = _tri_masks(CHUNK)
    causal = r >= cc
    strict = r > cc
    eye = (r == cc).astype(F32)
    gcum_t = lax.dot_general(gcum_all, eye, (((0,), (0,)), ((), ())), preferred_element_type=F32,
                             precision=HI)

    heads = range(n_heads)
    hsl = [slice(h * HEAD_DIM, (h + 1) * HEAD_DIM) for h in heads]
    beta = [beta_all[:, h:h + 1] for h in heads]
    gcol = [gcum_all[:, n_heads + h:n_heads + h + 1] for h in heads]
    decay = [jnp.where(causal, jnp.exp(jnp.minimum(gcol[h] - gcum_t[n_heads + h:n_heads + h + 1, :], 0.0)), 0.0)
             for h in heads]
    eg = [jnp.exp(g) for g in gcol]
    g_last = [g[CHUNK - 1:CHUNK, :] for g in gcol]
    q = [q_all[:, s] * lax.rsqrt(jnp.sum(q_all[:, s] * q_all[:, s], axis=-1, keepdims=True) + NORM_EPS)
         * HEAD_DIM ** -0.5 for s in hsl]
    k = [k_all[:, s] * lax.rsqrt(jnp.sum(k_all[:, s] * k_all[:, s], axis=-1, keepdims=True) + NORM_EPS) for s in hsl]
    kb = [k[h] * beta[h] for h in heads]
    k16 = [x.astype(BF16) for x in k]
    m = [jnp.where(strict, -(_dot_nt(kb[h].astype(BF16), k16[h]) * decay[h]), 0.0) for h in heads]
    attn = [_dot_nt(q[h].astype(BF16), k16[h]) * decay[h] for h in heads]
    inv = [eye + x for x in m]
    mp = m
    for _ in range(5):
        mp = [_dot(x.astype(BF16), x.astype(BF16)) for x in mp]
        inv = [inv[h] + _dot(inv[h].astype(BF16), mp[h].astype(BF16)) for h in heads]
    sol = [_dot(inv[h].astype(BF16),
                jnp.concatenate([v_all[:, hsl[h]] * beta[h], kb[h] * eg[h]], axis=1).astype(BF16)) for h in heads]
    s_old = [s_ref[h] for h in heads]
    s16 = [x.astype(BF16) for x in s_old]
    v_new = [sol[h][:, :HEAD_DIM] - _dot(sol[h][:, HEAD_DIM:].astype(BF16), s16[h]) for h in heads]
    o = [_dot((q[h] * eg[h]).astype(BF16), s16[h]) + _dot(attn[h].astype(BF16), v_new[h].astype(BF16)) for h in heads]
    for h in heads:
        s_ref[h] = s_old[h] * jnp.exp(g_last[h]) + _dot_tn(k[h] * jnp.exp(g_last[h] - gcol[h]), v_new[h])
        o_ref[:, hsl[h]] = _gated_rms(o[h], nw_ref[...], gt_ref[:, hsl[h]]).astype(o_ref.dtype)


def _mixer_b_kernel(q_ref, f_ref, i_ref, gt_ref, lb_ref, nw_ref, o_ref, st_ref, kbuf, bbuf, obuf,
                    *, n_heads, front, t_valid):
    c = pl.program_id(1)

    @pl.when(c == 0)
    def _():
        st_ref[...] = jnp.zeros_like(st_ref)

    row = lax.broadcasted_iota(jnp.int32, (CHUNK, 1), 0)
    valid = _valid_rows(c * CHUNK, CHUNK, front, t_valid).astype(F32)
    row16 = lax.broadcasted_iota(jnp.int32, (SUB, 1), 0)

    lb = lb_ref[...]
    f = lb + (1.0 - lb) * _sigmoid(f_ref[...])
    kbuf[...] = (1.0 - f) * valid
    bbuf[...] = _prefix_rows(jnp.log(f) * valid, row, SUB)

    heads = range(n_heads)
    hsl = [slice(h * HEAD_DIM, (h + 1) * HEAD_DIM) for h in heads]
    st = [st_ref[h] for h in heads]
    for blk in range(CHUNK // SUB):
        rs = slice(blk * SUB, (blk + 1) * SUB)
        for h in heads:
            hs = hsl[h]
            b16 = bbuf[rs, hs]
            q16 = _silu(q_ref[rs, hs])
            b_end = b16[SUB - 1:SUB, :]
            acc = _dot_nt((q16 * jnp.exp(b16)).astype(BF16), st[h].astype(BF16))
            for j in range(SUB):
                jr = pl.ds(blk * SUB + j, 1)
                e = jnp.exp(jnp.minimum(b16 - bbuf[jr, hs], 0.0))
                sc = jnp.sum(q16 * kbuf[jr, hs] * e, axis=-1, keepdims=True)
                acc = acc + jnp.where(row16 >= j, sc, 0.0) * i_ref[jr, hs]
            st[h] = st[h] * jnp.exp(b_end) + _dot_tn(i_ref[rs, hs], kbuf[rs, hs] * jnp.exp(b_end - b16))
            obuf[rs, hs] = acc
    for h in heads:
        st_ref[h] = st[h]
        o_ref[:, hsl[h]] = _gated_rms(obuf[:, hsl[h]], nw_ref[...], gt_ref[:, hsl[h]]).astype(o_ref.dtype)


def _rope_partial(x, cos, sin_lo, sin_hi):
    half = ROPE_DIMS // 2
    return x * cos + pltpu.roll(x, LANE - half, 1) * sin_lo + pltpu.roll(x, half, 1) * sin_hi


def _rope_c_kernel(q_ref, k_ref, v_ref, cos_ref, slo_ref, shi_ref, qo_ref, ko_ref, vo_ref, *, n_maps):
    cos, slo, shi = cos_ref[...], slo_ref[...], shi_ref[...]
    for mp in range(n_maps):
        ms = slice(mp * HEAD_DIM, (mp + 1) * HEAD_DIM)
        qo_ref[:, ms] = (_rope_partial(q_ref[:, ms], cos, slo, shi) * HEAD_DIM ** -0.5).astype(BF16)
        ko_ref[:, ms] = _rope_partial(k_ref[:, ms], cos, slo, shi).astype(BF16)
    vo_ref[...] = v_ref[...].astype(BF16)


def _mixer_c_kernel(qi_ref, kj_ref, q_ref, k_ref, v_ref, lam_ref, nw_ref, o_ref, m_ref, l_ref, acc_ref,
                    *, tq, front, out_scale):
    p = pl.program_id(2)
    i = qi_ref[p]
    j = kj_ref[p]

    @pl.when(j == 0)
    def _():
        m_ref[...] = jnp.full_like(m_ref, NEG)
        l_ref[...] = jnp.zeros_like(l_ref)
        acc_ref[...] = jnp.zeros_like(acc_ref)

    def step(masked):
        v = v_ref[...]
        if masked:
            rq = i * tq + lax.broadcasted_iota(jnp.int32, (tq, 1), 0)
            rk = j * tq + lax.broadcasted_iota(jnp.int32, (1, tq), 1)
            msk = (rk <= rq) & (rk >= front)
        for mp in range(2):
            ms = slice(mp * HEAD_DIM, (mp + 1) * HEAD_DIM)
            s = _dot_nt(q_ref[:, ms], k_ref[:, ms])
            if masked:
                s = jnp.where(msk, s, NEG)
            m_prev = m_ref[mp]
            m_new = jnp.maximum(m_prev, jnp.max(s, axis=-1, keepdims=True))
            pr = jnp.exp(s - jnp.tile(m_new, (1, tq // LANE)))
            a = jnp.exp(m_prev - m_new)
            l_ref[mp] = a * l_ref[mp] + jnp.sum(pr, axis=-1, keepdims=True)
            acc_ref[mp] = jnp.tile(a, (1, 2 * HEAD_DIM // LANE)) * acc_ref[mp] + _dot(pr.astype(BF16), v)
            m_ref[mp] = m_new

    needs_mask = (j == i) | (j == 0)

    @pl.when(needs_mask)
    def _():
        step(True)

    @pl.when(jnp.logical_not(needs_mask))
    def _():
        step(False)

    @pl.when(j == i)
    def _():
        reps = (1, 2 * HEAD_DIM // LANE)
        o = acc_ref[0] / jnp.tile(l_ref[0], reps) - lam_ref[...] * (acc_ref[1] / jnp.tile(l_ref[1], reps))
        o = o * lax.rsqrt(jnp.mean(o * o, axis=-1, keepdims=True) + NORM_EPS) * nw_ref[...] * out_scale
        o_ref[...] = o.astype(o_ref.dtype)


def _mixer_d_kernel(q_ref, k_ref, v_ref, gt_ref, cos_ref, sin_ref, dm_ref, qd_ref, kd_ref, cd_ref, o_ref, s_ref,
                    *, n_heads):
    c = pl.program_id(1)

    @pl.when(c == 0)
    def _():
        s_ref[...] = jnp.zeros_like(s_ref)

    cos = cos_ref[...]
    sin = sin_ref[...]
    heads = range(n_heads)
    hsl = [slice(h * HEAD_DIM, (h + 1) * HEAD_DIM) for h in heads]
    q = [q_ref[:, s] * cos + pltpu.roll(q_ref[:, s], HEAD_DIM // 2, 1) * sin for s in hsl]
    k = [(k_ref[:, s] * cos + pltpu.roll(k_ref[:, s], HEAD_DIM // 2, 1) * sin) * HEAD_DIM ** -0.5 for s in hsl]
    s_old = [s_ref[h] for h in heads]
    a = [_dot_nt(q[h].astype(BF16), k[h].astype(BF16)) * dm_ref[h] for h in heads]
    o = [_dot(a[h].astype(BF16), v_ref[:, hsl[h]].astype(BF16))
         + _dot((q[h] * qd_ref[h]).astype(BF16), s_old[h].astype(BF16)) for h in heads]
    for h in heads:
        s_ref[h] = cd_ref[h] * s_old[h] + _dot_tn(k[h] * kd_ref[h], v_ref[:, hsl[h]])
        o_ref[:, hsl[h]] = _gated_rms(o[h], None, gt_ref[:, hsl[h]]).astype(o_ref.dtype)


def _router_kernel(h_ref, w_ref, b_ref, o_ref):
    n_e = N_GROUPS * EXP_PER_GROUP
    lg = _dot_hi(h_ref[...], w_ref[...]) + b_ref[...]
    lane_i = lax.broadcasted_iota(jnp.int32, lg.shape, 1)
    lane = lane_i.astype(F32)
    grp = (lane_i >> GROUP_SHIFT).astype(F32)
    big = float(4 * LANE)
    is_g = (lane_i >= n_e) & (lane_i < n_e + N_GROUPS)
    gl = jnp.where(is_g, lg, NEG)
    gmax = jnp.max(gl, axis=-1, keepdims=True)
    gsum = jnp.sum(jnp.where(is_g, jnp.exp(gl - gmax), 0.0), axis=-1, keepdims=True)
    g_w = 1.0 / gsum
    g_idx = jnp.min(jnp.where(is_g & (gl == gmax), lane - n_e, big), axis=-1, keepdims=True)
    in_grp = (lane_i < n_e) & (grp == g_idx)
    el = jnp.where(in_grp, lg, NEG)
    emax = jnp.max(el, axis=-1, keepdims=True)
    eexp = jnp.where(in_grp, jnp.exp(el - emax), 0.0)
    pe = eexp / jnp.sum(eexp, axis=-1, keepdims=True)
    p1 = jnp.max(jnp.where(in_grp, pe, -1.0), axis=-1, keepdims=True)
    i1 = jnp.min(jnp.where(in_grp & (pe == p1), lane, big), axis=-1, keepdims=True)
    rest = in_grp & (lane != i1)
    p2 = jnp.max(jnp.where(rest, pe, -1.0), axis=-1, keepdims=True)
    i2 = jnp.min(jnp.where(rest & (pe == p2), lane, big), axis=-1, keepdims=True)
    den = p1 + p2
    o_ref[...] = (jnp.where(lane_i == 0, i1, 0.0) + jnp.where(lane_i == 1, i2, 0.0)
                  + jnp.where(lane_i == 2, g_w * (p1 / den), 0.0) + jnp.where(lane_i == 3, g_w * (p2 / den), 0.0))


def _router(h, w_r, b_r, tpp):
    n, d = h.shape
    tm = _divisor_tile(tpp, 256, 8)
    return pl.pallas_call(
        _router_kernel,
        out_shape=jax.ShapeDtypeStruct((n, LANE), F32),
        grid=(n // tm,),
        in_specs=[pl.BlockSpec((tm, d), lambda i: (i, 0)),
                  pl.BlockSpec((d, LANE), lambda i: (0, 0)),
                  pl.BlockSpec((1, LANE), lambda i: (0, 0))],
        out_specs=pl.BlockSpec((tm, LANE), lambda i: (i, 0)),
        compiler_params=_cparams(("parallel",)),
        name="router",
    )(h, w_r, b_r)


def _row_copy(src, src_row, dst, dst_row, sem):
    return pltpu.make_async_copy(src.at[pl.ds(src_row, 1)], dst.at[pl.ds(dst_row, 1)], sem)


def _moe_kernel(te_ref, cnt_ref, first_ref, src_ref, srcn_ref, dst_ref, x_hbm, wg_ref, wu_ref, wd_ref, y_hbm,
                xbuf, ybuf, wgb, wub, wdb, sem_in, sem_out, *, tm, n_tiles):
    del te_ref
    t = pl.program_id(0)
    slot = t % 2
    cnt = cnt_ref[t]

    def pairs(rows):
        return (rows + 1) // 2

    def gather_start(idx_ref, s, rows):
        def body(g, c):
            for u in range(2):
                r = 2 * g + u
                _row_copy(x_hbm, idx_ref[0, 0, r], xbuf.at[s], r, sem_in.at[s]).start(priority=u)
            return c
        lax.fori_loop(0, pairs(rows), body, 0)

    def gather_wait(s, rows):
        def body(g, c):
            for u in range(2):
                _row_copy(x_hbm, 2 * g + u, xbuf.at[s], 2 * g + u, sem_in.at[s]).wait()
            return c
        lax.fori_loop(0, pairs(rows), body, 0)

    def scatter_start(s, rows):
        def body(g, c):
            for u in range(2):
                r = 2 * g + u
                _row_copy(ybuf.at[s], r, y_hbm, dst_ref[0, 0, r], sem_out.at[s]).start(priority=u)
            return c
        lax.fori_loop(0, rows // 2, body, 0)

        @pl.when(rows % 2 == 1)
        def _():
            _row_copy(ybuf.at[s], rows - 1, y_hbm, dst_ref[0, 0, rows - 1], sem_out.at[s]).start()

    def scatter_wait(s, rows):
        def body(r, c):
            _row_copy(ybuf.at[s], r, y_hbm, r, sem_out.at[s]).wait()
            return c
        lax.fori_loop(0, rows, body, 0)

    @pl.when(t == 0)
    def _():
        xbuf[...] = jnp.zeros_like(xbuf)

        @pl.when(cnt > 0)
        def _():
            gather_start(src_ref, 0, cnt)

    nxt = jnp.minimum(t + 1, n_tiles - 1)

    @pl.when((t + 1 < n_tiles) & (cnt_ref[nxt] > 0))
    def _():
        gather_start(srcn_ref, 1 - slot, cnt_ref[nxt])

    old = jnp.maximum(t - 2, 0)

    @pl.when((t >= 2) & (cnt_ref[old] > 0))
    def _():
        scatter_wait(slot, cnt_ref[old])

    @pl.when(cnt > 0)
    def _():
        @pl.when(first_ref[t] == 1)
        def _():
            wgb[...] = wg_ref[...].astype(BF16)
            wub[...] = wu_ref[...].astype(BF16)
            wdb[...] = wd_ref[...].astype(BF16)

        gather_wait(slot, cnt)
        x = xbuf[slot].astype(BF16)
        hg = _dot(x, wgb[...])
        hu = _dot(x, wub[...])
        act = (_silu(hg) * hu).astype(BF16)
        ybuf[slot] = _dot(act, wdb[...])
        scatter_start(slot, cnt)

    @pl.when(t == n_tiles - 1)
    def _():
        prev = jnp.maximum(t - 1, 0)

        @pl.when((t >= 1) & (cnt_ref[prev] > 0))
        def _():
            scatter_wait(1 - slot, cnt_ref[prev])

        @pl.when(cnt > 0)
        def _():
            scatter_wait(slot, cnt)


def _dispatch(route, n, tm, n_e):
    ids = jnp.concatenate([route[:, 0], route[:, 1]]).astype(jnp.int32)
    n_pairs = 2 * n
    n_tiles = n_pairs // tm + n_e
    order = jnp.argsort(ids, stable=True).astype(jnp.int32)
    ids_s = ids[order]
    counts = jnp.sum((ids[:, None] == jnp.arange(n_e, dtype=jnp.int32)[None, :]).astype(jnp.int32), axis=0)
    tiles_e = (counts + tm - 1) // tm
    tile_end = jnp.cumsum(tiles_e)
    tile_start = tile_end - tiles_e
    first = jnp.cumsum(counts) - counts
    pos = tile_start[ids_s] * tm + (jnp.arange(n_pairs, dtype=jnp.int32) - first[ids_s])
    dst = jnp.zeros((n_tiles * tm,), jnp.int32).at[pos].set(order)
    src = dst % n
    tile = jnp.arange(n_tiles, dtype=jnp.int32)
    te = jnp.minimum(jnp.searchsorted(tile_end, tile, side="right").astype(jnp.int32), n_e - 1)
    cnt = jnp.clip(counts[te] - (tile - tile_start[te]) * tm, 0, tm)
    cnt = jnp.where(tile < tile_end[n_e - 1], cnt, 0).astype(jnp.int32)
    te = jnp.where(cnt > 0, te, jnp.max(jnp.where(cnt > 0, te, 0)))
    new_expert = jnp.concatenate([jnp.ones((1,), jnp.int32), (te[1:] != te[:-1]).astype(jnp.int32)])
    return te, cnt, new_expert, src.reshape(n_tiles, 1, tm), dst.reshape(n_tiles, 1, tm)


def _moe_routed(h, route, wg, wu, wd, layer, tm):
    n, d = h.shape
    _, n_e, _, f = wg.shape
    te, cnt, new_expert, src, dst = _dispatch(route, n, tm, n_e)
    n_tiles = te.shape[0]

    def idx_spec(shift):
        return pl.BlockSpec((1, 1, tm), lambda t, *_: (jnp.minimum(t + shift, n_tiles - 1), 0, 0),
                            memory_space=pltpu.SMEM)

    grid_spec = pltpu.PrefetchScalarGridSpec(
        num_scalar_prefetch=3, grid=(n_tiles,),
        in_specs=[idx_spec(0), idx_spec(1), idx_spec(0),
                  pl.BlockSpec(memory_space=pl.ANY),
                  pl.BlockSpec((None, None, d, f), lambda t, te, *_: (layer, te[t], 0, 0)),
                  pl.BlockSpec((None, None, d, f), lambda t, te, *_: (layer, te[t], 0, 0)),
                  pl.BlockSpec((None, None, f, d), lambda t, te, *_: (layer, te[t], 0, 0))],
        out_specs=pl.BlockSpec(memory_space=pl.ANY),
        scratch_shapes=[pltpu.VMEM((2, tm, d), F32), pltpu.VMEM((2, tm, d), F32),
                        pltpu.VMEM((d, f), BF16), pltpu.VMEM((d, f), BF16), pltpu.VMEM((f, d), BF16),
                        pltpu.SemaphoreType.DMA((2,)), pltpu.SemaphoreType.DMA((2,))])
    return pl.pallas_call(
        functools.partial(_moe_kernel, tm=tm, n_tiles=n_tiles),
        out_shape=jax.ShapeDtypeStruct((2 * n, d), F32),
        grid_spec=grid_spec,
        compiler_params=_cparams(("arbitrary",)),
        name="moe_routed",
    )(te, cnt, new_expert, src, src, dst, h, wg, wu, wd)


def _mixers(z, zs, lay, bsz, w, prm):
    front, t_valid, tpp = lay
    n = z.shape[0]
    nc = tpp // CHUNK
    heads = w // HEAD_DIM
    c_heads = w // (2 * HEAD_DIM)

    def cspec(region):
        return pl.BlockSpec((CHUNK, w), lambda b, c, region=region: (b * nc + c, region))

    def fixed(shape):
        nd = len(shape)
        return pl.BlockSpec(shape, lambda b, c: (0,) * nd)

    ospec = pl.BlockSpec((CHUNK, w), lambda b, c: (b * nc + c, 0))
    oshape = jax.ShapeDtypeStruct((n, w), BF16)
    grid = (bsz, nc)
    sem = ("parallel", "arbitrary")
    state = pltpu.VMEM((heads, HEAD_DIM, HEAD_DIM), F32)

    oa = pl.pallas_call(
        functools.partial(_mixer_a_kernel, n_heads=heads, front=front, t_valid=t_valid),
        out_shape=oshape, grid=grid,
        in_specs=[cspec(0), cspec(1), cspec(2), cspec(3),
                  pl.BlockSpec((CHUNK, LANE), lambda b, c: (b * nc + c, 0)),
                  fixed((3, 4, w)), fixed((1, LANE)), fixed((1, LANE)), fixed((1, HEAD_DIM))],
        out_specs=ospec,
        scratch_shapes=[state, pltpu.VMEM((3, SUBLANES, w), F32)],
        compiler_params=_cparams(sem), name="mixer_a",
    )(z, z, z, z, zs, prm["conv_w"], prm["a_log_row"], prm["dt_bias_row"], prm["norm_a"])

    ob = pl.pallas_call(
        functools.partial(_mixer_b_kernel, n_heads=heads, front=front, t_valid=t_valid),
        out_shape=oshape, grid=grid,
        in_specs=[cspec(4), cspec(5), cspec(6), cspec(7), fixed((1, w)), fixed((1, HEAD_DIM))],
        out_specs=ospec,
        scratch_shapes=[state] + [pltpu.VMEM((CHUNK, w), F32)] * 3,
        compiler_params=_cparams(sem), name="mixer_b",
    )(z, z, z, z, prm["lb"], prm["norm_b"])

    tq = _divisor_tile(tpp, 384, LANE)
    nq = tpp // tq
    vw = 2 * HEAD_DIM
    rc = prm["rope_c"]
    rspec = pl.BlockSpec((tq, w), lambda i: (i, 0))
    tab = pl.BlockSpec((tq, HEAD_DIM), lambda i: (i % nq, 0))
    qr, kr, vr = pl.pallas_call(
        functools.partial(_rope_c_kernel, n_maps=heads),
        out_shape=(oshape, oshape, oshape), grid=(n // tq,),
        in_specs=[pl.BlockSpec((tq, w), lambda i: (i, 8)), pl.BlockSpec((tq, w), lambda i: (i, 9)),
                  pl.BlockSpec((tq, w), lambda i: (i, 10)), tab, tab, tab],
        out_specs=(rspec, rspec, rspec),
        compiler_params=_cparams(("parallel",)), name="rope_c",
    )(z, z, z, rc[0], rc[1], rc[2])

    pairs = [(i, j) for i in range(nq) for j in range(i + 1)]
    qi = jnp.asarray([p[0] for p in pairs], jnp.int32)
    kj = jnp.asarray([p[1] for p in pairs], jnp.int32)
    qspec = pl.BlockSpec((tq, vw), lambda b, h, p, qi, kj: (b * nq + qi[p], h))
    kspec = pl.BlockSpec((tq, vw), lambda b, h, p, qi, kj: (b * nq + kj[p], h))
    row_vw = pl.BlockSpec((1, vw), lambda b, h, p, qi, kj: (0, 0))
    oc = pl.pallas_call(
        functools.partial(_mixer_c_kernel, tq=tq, front=front, out_scale=prm["c_out_scale"]),
        out_shape=oshape,
        grid_spec=pltpu.PrefetchScalarGridSpec(
            num_scalar_prefetch=2, grid=(bsz, c_heads, len(pairs)),
            in_specs=[qspec, kspec, kspec, row_vw, row_vw],
            out_specs=qspec,
            scratch_shapes=[pltpu.VMEM((2, tq, LANE), F32), pltpu.VMEM((2, tq, LANE), F32),
                            pltpu.VMEM((2, tq, vw), F32)]),
        compiler_params=_cparams(("parallel", "parallel", "arbitrary")), name="mixer_c",
    )(qi, kj, qr, kr, vr, prm["lam_row"], prm["subln_c"])

    tabd = pl.BlockSpec((CHUNK, HEAD_DIM), lambda b, c: (c, 0))
    rd = prm["rope_d"]
    od = pl.pallas_call(
        functools.partial(_mixer_d_kernel, n_heads=heads),
        out_shape=oshape, grid=grid,
        in_specs=[cspec(11), cspec(12), cspec(13), cspec(14), tabd, tabd,
                  fixed((heads, CHUNK, CHUNK)), fixed((heads, CHUNK, HEAD_DIM)), fixed((heads, CHUNK, HEAD_DIM)),
                  fixed((heads, 1, HEAD_DIM))],
        out_specs=ospec,
        scratch_shapes=[state],
        compiler_params=_cparams(sem), name="mixer_d",
    )(z, z, z, z, rd[0], rd[1], prm["ret_dmask"], prm["ret_qdec"], prm["ret_kdec"], prm["ret_cdec"])
    return oa, ob, oc, od


def _rope_tables(front, tpp):
    pos = (jnp.arange(tpp) - front).astype(F32)[:, None]
    half = ROPE_DIMS // 2
    inv = 1.0 / (ROPE_THETA ** (jnp.arange(half, dtype=F32) / half))
    ang = pos * inv[None, :]
    z_rest = jnp.zeros((tpp, HEAD_DIM - ROPE_DIMS), F32)
    z_half = jnp.zeros((tpp, half), F32)
    cos_c = jnp.concatenate([jnp.cos(ang), jnp.cos(ang), jnp.ones_like(z_rest)], axis=1)
    sin_lo = jnp.concatenate([-jnp.sin(ang), z_half, z_rest], axis=1)
    sin_hi = jnp.concatenate([z_half, jnp.sin(ang), z_rest], axis=1)
    half_d = HEAD_DIM // 2
    inv_d = 1.0 / (RET_THETA ** (jnp.arange(half_d, dtype=F32) / half_d))
    ang_d = pos * inv_d[None, :]
    cos_d = jnp.concatenate([jnp.cos(ang_d), jnp.cos(ang_d)], axis=1)
    sin_d = jnp.concatenate([-jnp.sin(ang_d), jnp.sin(ang_d)], axis=1)
    return (cos_c, sin_lo, sin_hi), (cos_d, sin_d)


def _retention_tables(n_heads):
    lg = jnp.log(1.0 - 2.0 ** (-5.0 - jnp.arange(n_heads, dtype=F32)))
    idx = jnp.arange(CHUNK, dtype=F32)
    rel = idx[:, None] - idx[None, :]
    causal = idx[:, None] >= idx[None, :]
    dmask = jnp.exp(jnp.where(causal[None], rel[None] * lg[:, None, None], -jnp.inf))
    qdec = jnp.exp((idx[None, :] + 1.0) * lg[:, None])
    kdec = jnp.exp((CHUNK - 1.0 - idx[None, :]) * lg[:, None])
    cdec = jnp.exp(CHUNK * lg)
    bc = lambda t: jnp.broadcast_to(t[..., None], t.shape + (HEAD_DIM,))
    return dmask, bc(qdec), bc(kdec), bc(cdec[:, None])


def _row(v, width=LANE, offset=0):
    out = jnp.zeros((1, width), F32)
    return out.at[0, offset:offset + v.shape[0]].set(v.astype(F32))


def kernel(x, meta_tokens, emb_ln_g, emb_ln_b, w_in, conv_a, a_log, dt_bias, norm_a, hgrn_lb, norm_b, lam_q1, lam_k1,
           lam_q2, lam_k2, subln_c, w_out, ln1_g, ln1_b, w_rg, b_rg, w_re, b_re, w_gate, w_up, w_down, ln2_g, ln2_b):
    bsz, seq, d = x.shape
    depth = w_in.shape[0]
    n_meta = meta_tokens.shape[0]
    w = d // 4
    a_heads = w // HEAD_DIM
    t_valid = n_meta + seq
    front = (-t_valid) % CHUNK
    tpp = -(-(front + t_valid) // ROW_ALIGN) * ROW_ALIGN
    lay = (front, t_valid, tpp)
    alpha = (2 * depth) ** 0.25

    meta = jnp.broadcast_to(meta_tokens[None].astype(x.dtype), (bsz, n_meta, d))
    xp = jnp.concatenate([jnp.zeros((bsz, front, d), x.dtype), meta, x,
                          jnp.zeros((bsz, tpp - front - t_valid, d), x.dtype)], axis=1).reshape(bsz * tpp, d)
    h, hb = _add_ln(xp, (), emb_ln_g, emb_ln_b, 1.0, lay)

    lbs = jax.nn.softmax(hgrn_lb.astype(F32), axis=0)
    lbs = jnp.cumsum(lbs, axis=0) - lbs[0:1]
    rope_c, rope_d = _rope_tables(front, tpp)
    dmask, qdec, kdec, cdec = _retention_tables(a_heads)

    for l in range(depth):
        lam_init = 0.8 - 0.6 * math.exp(-0.3 * l)
        lam = (jnp.exp(jnp.sum(lam_q1[l].astype(F32) * lam_k1[l].astype(F32)))
               - jnp.exp(jnp.sum(lam_q2[l].astype(F32) * lam_k2[l].astype(F32))) + lam_init)
        prm = {
            "conv_w": conv_a[l].astype(F32).reshape(3, w, -1).transpose(0, 2, 1),
            "a_log_row": _row(a_log[l], offset=a_heads),
            "dt_bias_row": _row(dt_bias[l], offset=a_heads),
            "norm_a": norm_a[l].astype(F32).reshape(1, HEAD_DIM),
            "lb": lbs[l].reshape(1, w),
            "norm_b": norm_b[l].astype(F32).reshape(1, HEAD_DIM),
            "rope_c": rope_c, "rope_d": rope_d,
            "lam_row": jnp.full((1, 2 * HEAD_DIM), lam, F32),
            "subln_c": subln_c[l].astype(F32).reshape(1, 2 * HEAD_DIM),
            "c_out_scale": 1.0 - lam_init,
            "ret_dmask": dmask, "ret_qdec": qdec, "ret_kdec": kdec, "ret_cdec": cdec,
        }
        z, zs = _in_proj(hb, w_in, l, w)
        o_parts = _mixers(z, zs, lay, bsz, w, prm)
        y = _matmul(list(o_parts), w_out[l].astype(BF16), F32, 768, 1024)
        h, hb = _add_ln(h, (y,), ln1_g[l], ln1_b[l], alpha, lay)

        n_e = w_re.shape[2]
        w_r = jnp.concatenate([w_re[l], w_rg[l], jnp.zeros((d, LANE - n_e - N_GROUPS), F32)], axis=1).astype(F32)
        b_r = jnp.concatenate([b_re[l], b_rg[l], jnp.zeros((LANE - n_e - N_GROUPS,), F32)]).reshape(1, LANE)
        route = _router(h, w_r, b_r, tpp)
        y2 = _moe_routed(h, route, w_gate, w_up, w_down, l, MOE_TILE)
        h, hb = _add_ln(h, (y2,), ln2_g[l], ln2_b[l], alpha, lay, route=route)

    return h.reshape(bsz, tpp, d)[:, front + n_meta:front + t_valid]
```

```python
import functools
import math

import jax
import jax.numpy as jnp
from jax import lax
from jax.experimental import pallas as pl
from jax.experimental.pallas import tpu as pltpu

HEAD_DIM = 128
CHUNK = 64
SUB = 16
GROUP_SHIFT = 3
ROPE_THETA = 500000.0
ROPE_DIMS = HEAD_DIM // 4
RET_THETA = 10000.0
N_GROUPS = 8
EXP_PER_GROUP = 8
NORM_EPS = 1e-6
LN_EPS = 1e-5
LANE = 128
SUBLANES = 8
ROW_ALIGN = 2 * CHUNK
V7X_VMEM_LIMIT_BYTES = 56 * 1024 * 1024
MOE_TILE = 256
NEG = -1e30

F32 = jnp.float32
BF16 = jnp.bfloat16
HI = lax.Precision.HIGHEST


def _cparams(sem):
    return pltpu.CompilerParams(dimension_semantics=sem, vmem_limit_bytes=V7X_VMEM_LIMIT_BYTES)


def _divisor_tile(n, cap, align):
    best = None
    for t in range(align, min(n, cap) + 1, align):
        if n % t == 0:
            best = t
    assert best is not None, (n, cap, align)
    return best


def _dot(a, b):
    return jnp.dot(a, b, preferred_element_type=F32)


def _dot_nt(a, b):
    return lax.dot_general(a, b, (((1,), (1,)), ((), ())), preferred_element_type=F32)


def _dot_tn(a, b):
    return lax.dot_general(a, b, (((0,), (0,)), ((), ())), preferred_element_type=F32)


def _sigmoid(x):
    return 1.0 / (1.0 + jnp.exp(-x))


def _silu(x):
    return x * _sigmoid(x)


def _softplus(x):
    return jnp.maximum(x, 0.0) + jnp.log(1.0 + jnp.exp(-jnp.abs(x)))


def _valid_rows(row0, n, front, t_valid):
    rb = row0 + lax.broadcasted_iota(jnp.int32, (n, 1), 0)
    return (rb >= front) & (rb < front + t_valid)


def _layer_norm_rows(x, g_ref, b_ref):
    mu = jnp.mean(x, axis=-1, keepdims=True)
    xc = x - mu
    var = jnp.mean(xc * xc, axis=-1, keepdims=True)
    return xc * lax.rsqrt(var + LN_EPS) * g_ref[...] + b_ref[...]


def _ln_kernel(*refs, alpha, mode, front, t_valid, tpp, tm):
    h_ref = refs[0]
    if mode == "experts":
        y0_ref, y1_ref, rt_ref, g_ref, b_ref, o_ref, ob_ref = refs[1:]
        x = alpha * h_ref[...] + rt_ref[:, 2:3] * y0_ref[...] + rt_ref[:, 3:4] * y1_ref[...]
    elif mode == "router":
        y_ref, g_ref, b_ref, wh_ref, wl_ref, br_ref, o_ref, ob_ref, ro_ref = refs[1:]
        x = alpha * h_ref[...] + y_ref[...]
    else:
        y_ref, g_ref, b_ref, o_ref, ob_ref = refs[1:]
        x = alpha * h_ref[...] + y_ref[...]
    y = _layer_norm_rows(x, g_ref, b_ref)
    row0 = (pl.program_id(0) * tm) % tpp
    y = jnp.where(_valid_rows(row0, tm, front, t_valid), y, 0.0)
    o_ref[...] = y
    yb = y.astype(BF16)
    ob_ref[...] = yb
    if mode == "router":
        yl = (y - yb.astype(F32)).astype(BF16)
        lg = _dot(yb, wh_ref[...]) + _dot(yl, wh_ref[...]) + _dot(yb, wl_ref[...]) + br_ref[...]
        ro_ref[...] = _route(lg)


def _add_ln(h, y, g, b, alpha, lay, route=None, router=None):
    front, t_valid, tpp = lay
    n, d = h.shape
    tm = _divisor_tile(tpp, 128, 16)
    spec = pl.BlockSpec((tm, d), lambda i: (i, 0))
    vec = pl.BlockSpec((1, d), lambda i: (0, 0))
    lane_rows = pl.BlockSpec((tm, LANE), lambda i: (i, 0))
    out_shape = [jax.ShapeDtypeStruct((n, d), F32), jax.ShapeDtypeStruct((n, d), BF16)]
    out_specs = [spec, spec]
    if route is not None:
        mode = "experts"
        args = (h, y, y, route, g.reshape(1, d), b.reshape(1, d))
        in_specs = [spec, spec, pl.BlockSpec((tm, d), lambda i: (i + n // tm, 0)), lane_rows, vec, vec]
    elif router is not None:
        mode = "router"
        wmat = pl.BlockSpec((d, LANE), lambda i: (0, 0))
        args = (h, y, g.reshape(1, d), b.reshape(1, d)) + tuple(router)
        in_specs = [spec, spec, vec, vec, wmat, wmat, pl.BlockSpec((1, LANE), lambda i: (0, 0))]
        out_shape.append(jax.ShapeDtypeStruct((n, LANE), F32))
        out_specs.append(lane_rows)
    else:
        mode = "plain"
        args = (h, y, g.reshape(1, d), b.reshape(1, d))
        in_specs = [spec, spec, vec, vec]
    kern = functools.partial(_ln_kernel, alpha=alpha, mode=mode, front=front, t_valid=t_valid, tpp=tpp, tm=tm)
    return pl.pallas_call(
        kern, out_shape=tuple(out_shape), grid=(n // tm,), in_specs=in_specs, out_specs=tuple(out_specs),
        compiler_params=_cparams(("parallel",)), name="add_ln_" + mode,
    )(*args)


def _embed_ln_kernel(x_ref, meta_ref, g_ref, b_ref, o_ref, ob_ref, *, front, n_meta, n_blocks):
    j = pl.program_id(1)

    @pl.when((j > 0) & (j <= n_blocks))
    def _():
        y = _layer_norm_rows(x_ref[...], g_ref, b_ref)
        o_ref[...] = y
        ob_ref[...] = y.astype(BF16)

    @pl.when(j == 0)
    def _():
        y = _layer_norm_rows(meta_ref[...], g_ref, b_ref)
        o_ref[...] = jnp.zeros_like(o_ref)
        ob_ref[...] = jnp.zeros_like(ob_ref)
        o_ref[front:front + n_meta, :] = y
        ob_ref[front:front + n_meta, :] = y.astype(BF16)

    @pl.when(j > n_blocks)
    def _():
        o_ref[...] = jnp.zeros_like(o_ref)
        ob_ref[...] = jnp.zeros_like(ob_ref)


def _embed_ln(x, meta_tokens, g, b, lay):
    front, t_valid, tpp = lay
    bsz, seq, d = x.shape
    n_meta = meta_tokens.shape[0]
    assert front + n_meta == CHUNK and seq % CHUNK == 0 and tpp == seq + 2 * CHUNK
    nb = seq // CHUNK
    blocks = tpp // CHUNK
    spec = pl.BlockSpec((CHUNK, d), lambda bb, j: (bb * blocks + j, 0))
    vec = pl.BlockSpec((1, d), lambda bb, j: (0, 0))
    return pl.pallas_call(
        functools.partial(_embed_ln_kernel, front=front, n_meta=n_meta, n_blocks=nb),
        out_shape=(jax.ShapeDtypeStruct((bsz * tpp, d), F32), jax.ShapeDtypeStruct((bsz * tpp, d), BF16)),
        grid=(bsz, blocks),
        in_specs=[pl.BlockSpec((None, CHUNK, d), lambda bb, j: (bb, jnp.clip(j - 1, 0, nb - 1), 0)),
                  pl.BlockSpec((n_meta, d), lambda bb, j: (0, 0)), vec, vec],
        out_specs=(spec, spec),
        compiler_params=_cparams(("parallel", "parallel")), name="embed_ln",
    )(x, meta_tokens.astype(x.dtype), g.reshape(1, d), b.reshape(1, d))


def _final_ln_kernel(h_ref, y0_ref, y1_ref, rt_ref, g_ref, b_ref, o_ref, *, alpha):
    x = alpha * h_ref[...] + rt_ref[:, 2:3] * y0_ref[...] + rt_ref[:, 3:4] * y1_ref[...]
    o_ref[...] = _layer_norm_rows(x, g_ref, b_ref)


def _final_ln(h, y2, route, g, b, alpha, lay, bsz, seq):
    front, t_valid, tpp = lay
    n, d = h.shape
    blocks = tpp // CHUNK
    first = (front + t_valid - seq) // CHUNK
    row = lambda bb, j: bb * blocks + first + j
    return pl.pallas_call(
        functools.partial(_final_ln_kernel, alpha=alpha),
        out_shape=jax.ShapeDtypeStruct((bsz, seq, d), F32),
        grid=(bsz, seq // CHUNK),
        in_specs=[pl.BlockSpec((CHUNK, d), lambda bb, j: (row(bb, j), 0)),
                  pl.BlockSpec((CHUNK, d), lambda bb, j: (row(bb, j), 0)),
                  pl.BlockSpec((CHUNK, d), lambda bb, j: (row(bb, j) + n // CHUNK, 0)),
                  pl.BlockSpec((CHUNK, LANE), lambda bb, j: (row(bb, j), 0)),
                  pl.BlockSpec((1, d), lambda bb, j: (0, 0)), pl.BlockSpec((1, d), lambda bb, j: (0, 0))],
        out_specs=pl.BlockSpec((None, CHUNK, d), lambda bb, j: (bb, j, 0)),
        compiler_params=_cparams(("parallel", "parallel")), name="final_ln",
    )(h, y2, y2, route, g.reshape(1, d), b.reshape(1, d))


def _mm_kernel(*refs, ks):
    a_refs = refs[:len(ks)]
    w_ref, o_ref = refs[len(ks):]
    acc = None
    off = 0
    for a_ref, k in zip(a_refs, ks):
        p = _dot(a_ref[...], w_ref[off:off + k, :])
        acc = p if acc is None else acc + p
        off += k
    o_ref[...] = acc.astype(o_ref.dtype)


def _matmul(a_list, w, out_dtype, tm_cap, tn_cap):
    n = a_list[0].shape[0]
    ks = tuple(a.shape[1] for a in a_list)
    kdim, ndim = w.shape
    assert sum(ks) == kdim
    tm = _divisor_tile(n, tm_cap, 16)
    tn = _divisor_tile(ndim, tn_cap, LANE)
    in_specs = [pl.BlockSpec((tm, k), lambda j, i: (i, 0)) for k in ks]
    in_specs.append(pl.BlockSpec((kdim, tn), lambda j, i: (0, j)))
    return pl.pallas_call(
        functools.partial(_mm_kernel, ks=ks),
        out_shape=jax.ShapeDtypeStruct((n, ndim), out_dtype),
        grid=(ndim // tn, n // tm),
        in_specs=in_specs,
        out_specs=pl.BlockSpec((tm, tn), lambda j, i: (i, j)),
        compiler_params=_cparams(("parallel", "parallel")),
        name="matmul",
    )(*a_list, w)


def _in_proj_kernel(a_ref, w_ref, wn_ref, o_ref, wb_ref, *, n_plain, shift, rows_per_cast):
    j = pl.program_id(0)

    @pl.when(pl.program_id(1) == 0)
    def _():
        kdim = w_ref.shape[0]
        for r0 in range(0, kdim, rows_per_cast):
            rs = slice(r0, r0 + rows_per_cast)

            @pl.when(j < n_plain)
            def _():
                wb_ref[rs, :] = w_ref[rs, :].astype(BF16)

            @pl.when(j >= n_plain)
            def _():
                wb_ref[rs, :] = jnp.concatenate([w_ref[rs, shift:], wn_ref[rs, :shift]], axis=1).astype(BF16)

    o_ref[...] = _dot(a_ref[...], wb_ref[...])


def _small_proj_kernel(a_ref, w_ref, o_ref):
    o_ref[...] = _dot(a_ref[...], w_ref[...].astype(BF16))


def _in_proj(hb, w_in, layer, w):
    n, d = hb.shape
    n_small = w_in.shape[2] - 15 * w
    tm = _divisor_tile(n, 768, 16)
    tn = _divisor_tile(w, 512, LANE)
    z = pl.pallas_call(
        functools.partial(_in_proj_kernel, n_plain=4 * w // tn, shift=n_small, rows_per_cast=min(d, 512)),
        out_shape=jax.ShapeDtypeStruct((n, 15 * w), F32),
        grid=(15 * w // tn, n // tm),
        in_specs=[pl.BlockSpec((tm, d), lambda j, i: (i, 0)),
                  pl.BlockSpec((None, d, tn), lambda j, i: (layer, 0, j)),
                  pl.BlockSpec((None, d, LANE), lambda j, i: (layer, 0, (j + 1) * (tn // LANE)))],
        out_specs=pl.BlockSpec((tm, tn), lambda j, i: (i, j)),
        scratch_shapes=[pltpu.VMEM((d, tn), BF16)],
        compiler_params=_cparams(("parallel", "arbitrary")),
        name="in_proj",
    )(hb, w_in, w_in)
    zs = pl.pallas_call(
        _small_proj_kernel,
        out_shape=jax.ShapeDtypeStruct((n, LANE), F32),
        grid=(n // tm,),
        in_specs=[pl.BlockSpec((tm, d), lambda i: (i, 0)),
                  pl.BlockSpec((None, d, LANE), lambda i: (layer, 0, 4 * w // LANE))],
        out_specs=pl.BlockSpec((tm, LANE), lambda i: (i, 0)),
        compiler_params=_cparams(("parallel",)),
        name="small_proj",
    )(hb, w_in)
    return z, zs


def _tri_masks(n):
    r = lax.broadcasted_iota(jnp.int32, (n, n), 0)
    c = lax.broadcasted_iota(jnp.int32, (n, n), 1)
    return r, c


def _gated_rms(o, w_row, gate):
    o = o * lax.rsqrt(jnp.mean(o * o, axis=-1, keepdims=True) + NORM_EPS)
    if w_row is not None:
        o = o * w_row
    return o * _silu(gate)


def _prefix_rows(x, row, seg):
    pos = row & (seg - 1)
    s = 1
    while s < seg:
        x = x + jnp.where(pos >= s, pltpu.roll(x, s, 0), 0.0)
        s *= 2
    return x


def _mixer_a_kernel(q_ref, k_ref, v_ref, gt_ref, sm_ref, cw_ref, alog_ref, dtb_ref, nw_ref, o_ref,
                    s_ref, prev_ref, *, n_heads, front, t_valid):
    c = pl.program_id(1)

    @pl.when(c == 0)
    def _():
        s_ref[...] = jnp.zeros_like(s_ref)
        prev_ref[...] = jnp.zeros_like(prev_ref)

    row = lax.broadcasted_iota(jnp.int32, (CHUNK, 1), 0)
    row8 = lax.broadcasted_iota(jnp.int32, (SUBLANES, 1), 0)
    valid = _valid_rows(c * CHUNK, CHUNK, front, t_valid).astype(F32)

    def conv_silu(x_ref, p):
        cur = x_ref[...]
        prev8 = prev_ref[p]
        acc = cur * cw_ref[p, 3:4, :]
        for s in (1, 2, 3):
            rolled = pltpu.roll(cur, s, 0)
            top = jnp.where(row8 >= s, rolled[:SUBLANES], pltpu.roll(prev8, s, 0))
            acc = acc + jnp.concatenate([top, rolled[SUBLANES:]], axis=0) * cw_ref[p, 3 - s:4 - s, :]
        prev_ref[p] = cur[CHUNK - SUBLANES:]
        return _silu(acc)

    q_all = conv_silu(q_ref, 0)
    k_all = conv_silu(k_ref, 1) * valid
    v_all = conv_silu(v_ref, 2) * valid

    sm = sm_ref[...]
    beta_all = _sigmoid(sm) * valid
    g_all = -jnp.exp(alog_ref[...]) * _softplus(sm + dtb_ref[...]) * valid
    gcum_all = _prefix_rows(g_all, row, CHUNK)
    r, cc = _tri_masks(CHUNK)
    causal = r >= cc
    strict = r > cc
    eye = (r == cc).astype(F32)
    gcum_t = lax.dot_general(gcum_all, eye, (((0,), (0,)), ((), ())), preferred_element_type=F32,
                             precision=HI)

    heads = range(n_heads)
    hsl = [slice(h * HEAD_DIM, (h + 1) * HEAD_DIM) for h in heads]
    beta = [beta_all[:, h:h + 1] for h in heads]
    gcol = [gcum_all[:, n_heads + h:n_heads + h + 1] for h in heads]
    decay = [jnp.where(causal, jnp.exp(jnp.minimum(gcol[h] - gcum_t[n_heads + h:n_heads + h + 1, :], 0.0)), 0.0)
             for h in heads]
    eg = [jnp.exp(g) for g in gcol]
    g_last = [g[CHUNK - 1:CHUNK, :] for g in gcol]
    q = [q_all[:, s] * lax.rsqrt(jnp.sum(q_all[:, s] * q_all[:, s], axis=-1, keepdims=True) + NORM_EPS)
         * HEAD_DIM ** -0.5 for s in hsl]
    k = [k_all[:, s] * lax.rsqrt(jnp.sum(k_all[:, s] * k_all[:, s], axis=-1, keepdims=True) + NORM_EPS) for s in hsl]
    kb = [k[h] * beta[h] for h in heads]
    k16 = [x.astype(BF16) for x in k]
    m = [jnp.where(strict, -(_dot_nt(kb[h].astype(BF16), k16[h]) * decay[h]), 0.0) for h in heads]
    attn = [_dot_nt(q[h].astype(BF16), k16[h]) * decay[h] for h in heads]
    inv = [eye + x for x in m]
    mp = m
    for _ in range(5):
        mp = [_dot(x.astype(BF16), x.astype(BF16)) for x in mp]
        inv = [inv[h] + _dot(inv[h].astype(BF16), mp[h].astype(BF16)) for h in heads]
    sol = [_dot(inv[h].astype(BF16),
                jnp.concatenate([v_all[:, hsl[h]] * beta[h], kb[h] * eg[h]], axis=1).astype(BF16)) for h in heads]
    s_old = [s_ref[h] for h in heads]
    s16 = [x.astype(BF16) for x in s_old]
    v_new = [sol[h][:, :HEAD_DIM] - _dot(sol[h][:, HEAD_DIM:].astype(BF16), s16[h]) for h in heads]
    o = [_dot((q[h] * eg[h]).astype(BF16), s16[h]) + _dot(attn[h].astype(BF16), v_new[h].astype(BF16)) for h in heads]
    for h in heads:
        s_ref[h] = s_old[h] * jnp.exp(g_last[h]) + _dot_tn(k[h] * jnp.exp(g_last[h] - gcol[h]), v_new[h])
        o_ref[:, hsl[h]] = _gated_rms(o[h], nw_ref[...], gt_ref[:, hsl[h]]).astype(o_ref.dtype)


def _mixer_b_kernel(q_ref, f_ref, i_ref, gt_ref, lb_ref, nw_ref, o_ref, st_ref, kbuf, bbuf, obuf,
                    *, n_heads, front, t_valid):
    c = pl.program_id(1)

    @pl.when(c == 0)
    def _():
        st_ref[...] = jnp.zeros_like(st_ref)

    row = lax.broadcasted_iota(jnp.int32, (CHUNK, 1), 0)
    valid = _valid_rows(c * CHUNK, CHUNK, front, t_valid).astype(F32)
    row16 = lax.broadcasted_iota(jnp.int32, (SUB, 1), 0)

    lb = lb_ref[...]
    f = lb + (1.0 - lb) * _sigmoid(f_ref[...])
    kbuf[...] = (1.0 - f) * valid
    bbuf[...] = _prefix_rows(jnp.log(f) * valid, row, SUB)

    heads = range(n_heads)
    hsl = [slice(h * HEAD_DIM, (h + 1) * HEAD_DIM) for h in heads]
    st = [st_ref[h] for h in heads]
    for blk in range(CHUNK // SUB):
        rs = slice(blk * SUB, (blk + 1) * SUB)
        for h in heads:
            hs = hsl[h]
            b16 = bbuf[rs, hs]
            q16 = _silu(q_ref[rs, hs])
            b_end = b16[SUB - 1:SUB, :]
            acc = _dot_nt((q16 * jnp.exp(b16)).astype(BF16), st[h].astype(BF16))
            parts = []
            for t0 in range(0, SUB, SUBLANES):
                ts = slice(t0, t0 + SUBLANES)
                bq, qq, rowq = b16[ts], q16[ts], row16[ts]
                part = acc[ts]
                for j in range(min(t0 + SUBLANES, SUB)):
                    jr = pl.ds(blk * SUB + j, 1)
                    sc = jnp.sum(qq * kbuf[jr, hs] * jnp.exp(bq - bbuf[jr, hs]), axis=-1, keepdims=True)
                    if j > t0:
                        sc = jnp.where(rowq >= j, sc, 0.0)
                    part = part + sc * i_ref[jr, hs]
                parts.append(part)
            acc = jnp.concatenate(parts, axis=0)
            st[h] = st[h] * jnp.exp(b_end) + _dot_tn(i_ref[rs, hs], kbuf[rs, hs] * jnp.exp(b_end - b16))
            obuf[rs, hs] = acc
    for h in heads:
        st_ref[h] = st[h]
        o_ref[:, hsl[h]] = _gated_rms(obuf[:, hsl[h]], nw_ref[...], gt_ref[:, hsl[h]]).astype(o_ref.dtype)


def _rope_partial(x, cos, sin_lo, sin_hi):
    half = ROPE_DIMS // 2
    return x * cos + pltpu.roll(x, LANE - half, 1) * sin_lo + pltpu.roll(x, half, 1) * sin_hi


def _rope_c_kernel(q_ref, k_ref, v_ref, cos_ref, slo_ref, shi_ref, qo_ref, ko_ref, vo_ref, *, n_maps):
    cos, slo, shi = cos_ref[...], slo_ref[...], shi_ref[...]
    for mp in range(n_maps):
        ms = slice(mp * HEAD_DIM, (mp + 1) * HEAD_DIM)
        qo_ref[:, ms] = (_rope_partial(q_ref[:, ms], cos, slo, shi) * HEAD_DIM ** -0.5).astype(BF16)
        ko_ref[:, ms] = _rope_partial(k_ref[:, ms], cos, slo, shi).astype(BF16)
    vo_ref[...] = v_ref[...].astype(BF16)


def _mixer_c_kernel(qi_ref, kj_ref, q_ref, k_ref, v_ref, lam_ref, nw_ref, o_ref, m_ref, l_ref, acc_ref,
                    *, tq, front, out_scale):
    p = pl.program_id(2)
    i = qi_ref[p]
    j = kj_ref[p]

    @pl.when(j == 0)
    def _():
        m_ref[...] = jnp.full_like(m_ref, NEG)
        l_ref[...] = jnp.zeros_like(l_ref)
        acc_ref[...] = jnp.zeros_like(acc_ref)

    def step(masked):
        v = v_ref[...]
        if masked:
            rq = i * tq + lax.broadcasted_iota(jnp.int32, (tq, 1), 0)
            rk = j * tq + lax.broadcasted_iota(jnp.int32, (1, tq), 1)
            msk = (rk <= rq) & (rk >= front)
        for mp in range(2):
            ms = slice(mp * HEAD_DIM, (mp + 1) * HEAD_DIM)
            s = _dot_nt(q_ref[:, ms], k_ref[:, ms])
            if masked:
                s = jnp.where(msk, s, NEG)
            m_prev = m_ref[mp]
            m_new = jnp.maximum(m_prev, jnp.max(s, axis=-1, keepdims=True))
            pr = jnp.exp(s - jnp.tile(m_new, (1, tq // LANE)))
            a = jnp.exp(m_prev - m_new)
            l_ref[mp] = a * l_ref[mp] + jnp.sum(pr, axis=-1, keepdims=True)
            acc_ref[mp] = jnp.tile(a, (1, 2 * HEAD_DIM // LANE)) * acc_ref[mp] + _dot(pr.astype(BF16), v)
            m_ref[mp] = m_new

    needs_mask = (j == i) | (j == 0)

    @pl.when(needs_mask)
    def _():
        step(True)

    @pl.when(jnp.logical_not(needs_mask))
    def _():
        step(False)

    @pl.when(j == i)
    def _():
        reps = (1, 2 * HEAD_DIM // LANE)
        o = acc_ref[0] / jnp.tile(l_ref[0], reps) - lam_ref[...] * (acc_ref[1] / jnp.tile(l_ref[1], reps))
        o = o * lax.rsqrt(jnp.mean(o * o, axis=-1, keepdims=True) + NORM_EPS) * nw_ref[...] * out_scale
        o_ref[...] = o.astype(o_ref.dtype)


def _mixer_d_kernel(q_ref, k_ref, v_ref, gt_ref, cos_ref, sin_ref, dm_ref, qd_ref, kd_ref, cd_ref, o_ref, s_ref,
                    *, n_heads):
    c = pl.program_id(1)

    @pl.when(c == 0)
    def _():
        s_ref[...] = jnp.zeros_like(s_ref)

    cos = cos_ref[...]
    sin = sin_ref[...]
    heads = range(n_heads)
    hsl = [slice(h * HEAD_DIM, (h + 1) * HEAD_DIM) for h in heads]
    q = [q_ref[:, s] * cos + pltpu.roll(q_ref[:, s], HEAD_DIM // 2, 1) * sin for s in hsl]
    k = [(k_ref[:, s] * cos + pltpu.roll(k_ref[:, s], HEAD_DIM // 2, 1) * sin) * HEAD_DIM ** -0.5 for s in hsl]
    s_old = [s_ref[h] for h in heads]
    a = [_dot_nt(q[h].astype(BF16), k[h].astype(BF16)) * dm_ref[h] for h in heads]
    o = [_dot(a[h].astype(BF16), v_ref[:, hsl[h]].astype(BF16))
         + _dot((q[h] * qd_ref[h]).astype(BF16), s_old[h].astype(BF16)) for h in heads]
    for h in heads:
        s_ref[h] = cd_ref[h] * s_old[h] + _dot_tn(k[h] * kd_ref[h], v_ref[:, hsl[h]])
        o_ref[:, hsl[h]] = _gated_rms(o[h], None, gt_ref[:, hsl[h]]).astype(o_ref.dtype)


def _route(lg):
    n_e = N_GROUPS * EXP_PER_GROUP
    lane_i = lax.broadcasted_iota(jnp.int32, lg.shape, 1)
    lane = lane_i.astype(F32)
    grp = (lane_i >> GROUP_SHIFT).astype(F32)
    big = float(4 * LANE)
    is_g = (lane_i >= n_e) & (lane_i < n_e + N_GROUPS)
    gl = jnp.where(is_g, lg, NEG)
    gmax = jnp.max(gl, axis=-1, keepdims=True)
    gsum = jnp.sum(jnp.where(is_g, jnp.exp(gl - gmax), 0.0), axis=-1, keepdims=True)
    g_w = 1.0 / gsum
    g_idx = jnp.min(jnp.where(is_g & (gl == gmax), lane - n_e, big), axis=-1, keepdims=True)
    in_grp = (lane_i < n_e) & (grp == g_idx)
    el = jnp.where(in_grp, lg, NEG)
    emax = jnp.max(el, axis=-1, keepdims=True)
    eexp = jnp.where(in_grp, jnp.exp(el - emax), 0.0)
    pe = eexp / jnp.sum(eexp, axis=-1, keepdims=True)
    p1 = jnp.max(jnp.where(in_grp, pe, -1.0), axis=-1, keepdims=True)
    i1 = jnp.min(jnp.where(in_grp & (pe == p1), lane, big), axis=-1, keepdims=True)
    rest = in_grp & (lane != i1)
    p2 = jnp.max(jnp.where(rest, pe, -1.0), axis=-1, keepdims=True)
    i2 = jnp.min(jnp.where(rest & (pe == p2), lane, big), axis=-1, keepdims=True)
    den = p1 + p2
    return (jnp.where(lane_i == 0, i1, 0.0) + jnp.where(lane_i == 1, i2, 0.0)
            + jnp.where(lane_i == 2, g_w * (p1 / den), 0.0) + jnp.where(lane_i == 3, g_w * (p2 / den), 0.0))


def _row_copy(src, src_row, dst, dst_row, sem):
    return pltpu.make_async_copy(src.at[pl.ds(src_row, 1)], dst.at[pl.ds(dst_row, 1)], sem)


def _moe_kernel(te_ref, cnt_ref, first_ref, src_ref, srcn_ref, dst_ref, x_hbm, wg_ref, wu_ref, wd_ref, y_hbm,
                xbuf, ybuf, wgb, wub, wdb, sem_in, sem_out, *, tm, n_tiles):
    del te_ref
    t = pl.program_id(0)
    slot = t % 2
    cnt = cnt_ref[t]

    def pairs(rows):
        return (rows + 1) // 2

    def gather_start(idx_ref, s, rows):
        def body(g, c):
            for u in range(2):
                r = 2 * g + u
                _row_copy(x_hbm, idx_ref[0, 0, r], xbuf.at[s], r, sem_in.at[s]).start(priority=u)
            return c
        lax.fori_loop(0, pairs(rows), body, 0)

    def wait_rows(src, dst, sem, rows):
        def tile_body(g, c):
            pltpu.make_async_copy(src.at[pl.ds(0, SUBLANES)], dst.at[pl.ds(g * SUBLANES, SUBLANES)], sem).wait()
            return c
        lax.fori_loop(0, rows // SUBLANES, tile_body, 0)

        def row_body(r, c):
            _row_copy(src, 0, dst, r, sem).wait()
            return c
        lax.fori_loop(rows - rows % SUBLANES, rows, row_body, 0)

    def gather_wait(s, rows):
        wait_rows(x_hbm, xbuf.at[s], sem_in.at[s], 2 * pairs(rows))

    def scatter_start(s, rows):
        def body(g, c):
            for u in range(2):
                r = 2 * g + u
                _row_copy(ybuf.at[s], r, y_hbm, dst_ref[0, 0, r], sem_out.at[s]).start(priority=u)
            return c
        lax.fori_loop(0, rows // 2, body, 0)

        @pl.when(rows % 2 == 1)
        def _():
            _row_copy(ybuf.at[s], rows - 1, y_hbm, dst_ref[0, 0, rows - 1], sem_out.at[s]).start()

    def scatter_wait(s, rows):
        wait_rows(ybuf.at[s], y_hbm, sem_out.at[s], rows)

    @pl.when(t == 0)
    def _():
        xbuf[...] = jnp.zeros_like(xbuf)

        @pl.when(cnt > 0)
        def _():
            gather_start(src_ref, 0, cnt)

    nxt = jnp.minimum(t + 1, n_tiles - 1)

    @pl.when((t + 1 < n_tiles) & (cnt_ref[nxt] > 0))
    def _():
        gather_start(srcn_ref, 1 - slot, cnt_ref[nxt])

    old = jnp.maximum(t - 2, 0)

    @pl.when((t >= 2) & (cnt_ref[old] > 0))
    def _():
        scatter_wait(slot, cnt_ref[old])

    @pl.when(cnt > 0)
    def _():
        @pl.when(first_ref[t] == 1)
        def _():
            wgb[...] = wg_ref[...].astype(BF16)
            wub[...] = wu_ref[...].astype(BF16)
            wdb[...] = wd_ref[...].astype(BF16)

        gather_wait(slot, cnt)
        x = xbuf[slot].astype(BF16)
        hg = _dot(x, wgb[...])
        hu = _dot(x, wub[...])
        act = (_silu(hg) * hu).astype(BF16)
        ybuf[slot] = _dot(act, wdb[...])
        scatter_start(slot, cnt)

    @pl.when(t == n_tiles - 1)
    def _():
        prev = jnp.maximum(t - 1, 0)

        @pl.when((t >= 1) & (cnt_ref[prev] > 0))
        def _():
            scatter_wait(1 - slot, cnt_ref[prev])

        @pl.when(cnt > 0)
        def _():
            scatter_wait(slot, cnt)


def _dispatch(route, n, tm, n_e):
    ids = jnp.concatenate([route[:, 0], route[:, 1]]).astype(jnp.int32)
    n_pairs = 2 * n
    n_tiles = n_pairs // tm + n_e
    order = jnp.argsort(ids, stable=True).astype(jnp.int32)
    ids_s = ids[order]
    counts = jnp.sum((ids[:, None] == jnp.arange(n_e, dtype=jnp.int32)[None, :]).astype(jnp.int32), axis=0)
    tiles_e = (counts + tm - 1) // tm
    tile_end = jnp.cumsum(tiles_e)
    tile_start = tile_end - tiles_e
    first = jnp.cumsum(counts) - counts
    pos = tile_start[ids_s] * tm + (jnp.arange(n_pairs, dtype=jnp.int32) - first[ids_s])
    dst = jnp.zeros((n_tiles * tm,), jnp.int32).at[pos].set(order)
    src = dst % n
    tile = jnp.arange(n_tiles, dtype=jnp.int32)
    te = jnp.minimum(jnp.searchsorted(tile_end, tile, side="right").astype(jnp.int32), n_e - 1)
    cnt = jnp.clip(counts[te] - (tile - tile_start[te]) * tm, 0, tm)
    cnt = jnp.where(tile < tile_end[n_e - 1], cnt, 0).astype(jnp.int32)
    te = jnp.where(cnt > 0, te, jnp.max(jnp.where(cnt > 0, te, 0)))
    new_expert = jnp.concatenate([jnp.ones((1,), jnp.int32), (te[1:] != te[:-1]).astype(jnp.int32)])
    return te, cnt, new_expert, src.reshape(n_tiles, 1, tm), dst.reshape(n_tiles, 1, tm)


def _moe_routed(h, route, wg, wu, wd, layer, tm):
    n, d = h.shape
    _, n_e, _, f = wg.shape
    te, cnt, new_expert, src, dst = _dispatch(route, n, tm, n_e)
    n_tiles = te.shape[0]

    def idx_spec(shift):
        return pl.BlockSpec((1, 1, tm), lambda t, *_: (jnp.minimum(t + shift, n_tiles - 1), 0, 0),
                            memory_space=pltpu.SMEM)

    grid_spec = pltpu.PrefetchScalarGridSpec(
        num_scalar_prefetch=3, grid=(n_tiles,),
        in_specs=[idx_spec(0), idx_spec(1), idx_spec(0),
                  pl.BlockSpec(memory_space=pl.ANY),
                  pl.BlockSpec((None, None, d, f), lambda t, te, *_: (layer, te[t], 0, 0)),
                  pl.BlockSpec((None, None, d, f), lambda t, te, *_: (layer, te[t], 0, 0)),
                  pl.BlockSpec((None, None, f, d), lambda t, te, *_: (layer, te[t], 0, 0))],
        out_specs=pl.BlockSpec(memory_space=pl.ANY),
        scratch_shapes=[pltpu.VMEM((2, tm, d), F32), pltpu.VMEM((2, tm, d), F32),
                        pltpu.VMEM((d, f), BF16), pltpu.VMEM((d, f), BF16), pltpu.VMEM((f, d), BF16),
                        pltpu.SemaphoreType.DMA((2,)), pltpu.SemaphoreType.DMA((2,))])
    return pl.pallas_call(
        functools.partial(_moe_kernel, tm=tm, n_tiles=n_tiles),
        out_shape=jax.ShapeDtypeStruct((2 * n, d), F32),
        grid_spec=grid_spec,
        compiler_params=_cparams(("arbitrary",)),
        name="moe_routed",
    )(te, cnt, new_expert, src, src, dst, h, wg, wu, wd)


def _mixers(z, zs, lay, bsz, w, prm):
    front, t_valid, tpp = lay
    n = z.shape[0]
    nc = tpp // CHUNK
    heads = w // HEAD_DIM
    c_heads = w // (2 * HEAD_DIM)

    def cspec(region):
        return pl.BlockSpec((CHUNK, w), lambda b, c, region=region: (b * nc + c, region))

    def fixed(shape):
        nd = len(shape)
        return pl.BlockSpec(shape, lambda b, c: (0,) * nd)

    ospec = pl.BlockSpec((CHUNK, w), lambda b, c: (b * nc + c, 0))
    oshape = jax.ShapeDtypeStruct((n, w), BF16)
    grid = (bsz, nc)
    sem = ("parallel", "arbitrary")
    state = pltpu.VMEM((heads, HEAD_DIM, HEAD_DIM), F32)

    oa = pl.pallas_call(
        functools.partial(_mixer_a_kernel, n_heads=heads, front=front, t_valid=t_valid),
        out_shape=oshape, grid=grid,
        in_specs=[cspec(0), cspec(1), cspec(2), cspec(3),
                  pl.BlockSpec((CHUNK, LANE), lambda b, c: (b * nc + c, 0)),
                  fixed((3, 4, w)), fixed((1, LANE)), fixed((1, LANE)), fixed((1, HEAD_DIM))],
        out_specs=ospec,
        scratch_shapes=[state, pltpu.VMEM((3, SUBLANES, w), F32)],
        compiler_params=_cparams(sem), name="mixer_a",
    )(z, z, z, z, zs, prm["conv_w"], prm["a_log_row"], prm["dt_bias_row"], prm["norm_a"])

    ob = pl.pallas_call(
        functools.partial(_mixer_b_kernel, n_heads=heads, front=front, t_valid=t_valid),
        out_shape=oshape, grid=grid,
        in_specs=[cspec(4), cspec(5), cspec(6), cspec(7), fixed((1, w)), fixed((1, HEAD_DIM))],
        out_specs=ospec,
        scratch_shapes=[state] + [pltpu.VMEM((CHUNK, w), F32)] * 3,
        compiler_params=_cparams(sem), name="mixer_b",
    )(z, z, z, z, prm["lb"], prm["norm_b"])

    tq = _divisor_tile(tpp, 384, LANE)
    nq = tpp // tq
    vw = 2 * HEAD_DIM
    rc = prm["rope_c"]
    rspec = pl.BlockSpec((tq, w), lambda i: (i, 0))
    tab = pl.BlockSpec((tq, HEAD_DIM), lambda i: (i % nq, 0))
    qr, kr, vr = pl.pallas_call(
        functools.partial(_rope_c_kernel, n_maps=heads),
        out_shape=(oshape, oshape, oshape), grid=(n // tq,),
        in_specs=[pl.BlockSpec((tq, w), lambda i: (i, 8)), pl.BlockSpec((tq, w), lambda i: (i, 9)),
                  pl.BlockSpec((tq, w), lambda i: (i, 10)), tab, tab, tab],
        out_specs=(rspec, rspec, rspec),
        compiler_params=_cparams(("parallel",)), name="rope_c",
    )(z, z, z, rc[0], rc[1], rc[2])

    pairs = [(i, j) for i in range(nq) for j in range(i + 1)]
    qi = jnp.asarray([p[0] for p in pairs], jnp.int32)
    kj = jnp.asarray([p[1] for p in pairs], jnp.int32)
    qspec = pl.BlockSpec((tq, vw), lambda b, h, p, qi, kj: (b * nq + qi[p], h))
    kspec = pl.BlockSpec((tq, vw), lambda b, h, p, qi, kj: (b * nq + kj[p], h))
    row_vw = pl.BlockSpec((1, vw), lambda b, h, p, qi, kj: (0, 0))
    oc = pl.pallas_call(
        functools.partial(_mixer_c_kernel, tq=tq, front=front, out_scale=prm["c_out_scale"]),
        out_shape=oshape,
        grid_spec=pltpu.PrefetchScalarGridSpec(
            num_scalar_prefetch=2, grid=(bsz, c_heads, len(pairs)),
            in_specs=[qspec, kspec, kspec, row_vw, row_vw],
            out_specs=qspec,
            scratch_shapes=[pltpu.VMEM((2, tq, LANE), F32), pltpu.VMEM((2, tq, LANE), F32),
                            pltpu.VMEM((2, tq, vw), F32)]),
        compiler_params=_cparams(("parallel", "parallel", "arbitrary")), name="mixer_c",
    )(qi, kj, qr, kr, vr, prm["lam_row"], prm["subln_c"])

    tabd = pl.BlockSpec((CHUNK, HEAD_DIM), lambda b, c: (c, 0))
    rd = prm["rope_d"]
    od = pl.pallas_call(
        functools.partial(_mixer_d_kernel, n_heads=heads),
        out_shape=oshape, grid=grid,
        in_specs=[cspec(11), cspec(12), cspec(13), cspec(14), tabd, tabd,
                  fixed((heads, CHUNK, CHUNK)), fixed((heads, CHUNK, HEAD_DIM)), fixed((heads, CHUNK, HEAD_DIM)),
                  fixed((heads, 1, HEAD_DIM))],
        out_specs=ospec,
        scratch_shapes=[state],
        compiler_params=_cparams(sem), name="mixer_d",
    )(z, z, z, z, rd[0], rd[1], prm["ret_dmask"], prm["ret_qdec"], prm["ret_kdec"], prm["ret_cdec"])
    return oa, ob, oc, od


def _rope_tables(front, tpp):
    pos = (jnp.arange(tpp) - front).astype(F32)[:, None]
    half = ROPE_DIMS // 2
    inv = 1.0 / (ROPE_THETA ** (jnp.arange(half, dtype=F32) / half))
    ang = pos * inv[None, :]
    z_rest = jnp.zeros((tpp, HEAD_DIM - ROPE_DIMS), F32)
    z_half = jnp.zeros((tpp, half), F32)
    cos_c = jnp.concatenate([jnp.cos(ang), jnp.cos(ang), jnp.ones_like(z_rest)], axis=1)
    sin_lo = jnp.concatenate([-jnp.sin(ang), z_half, z_rest], axis=1)
    sin_hi = jnp.concatenate([z_half, jnp.sin(ang), z_rest], axis=1)
    half_d = HEAD_DIM // 2
    inv_d = 1.0 / (RET_THETA ** (jnp.arange(half_d, dtype=F32) / half_d))
    ang_d = pos * inv_d[None, :]
    cos_d = jnp.concatenate([jnp.cos(ang_d), jnp.cos(ang_d)], axis=1)
    sin_d = jnp.concatenate([-jnp.sin(ang_d), jnp.sin(ang_d)], axis=1)
    return (cos_c, sin_lo, sin_hi), (cos_d, sin_d)


def _retention_tables(n_heads):
    lg = jnp.log(1.0 - 2.0 ** (-5.0 - jnp.arange(n_heads, dtype=F32)))
    idx = jnp.arange(CHUNK, dtype=F32)
    rel = idx[:, None] - idx[None, :]
    causal = idx[:, None] >= idx[None, :]
    dmask = jnp.exp(jnp.where(causal[None], rel[None] * lg[:, None, None], -jnp.inf))
    qdec = jnp.exp((idx[None, :] + 1.0) * lg[:, None])
    kdec = jnp.exp((CHUNK - 1.0 - idx[None, :]) * lg[:, None])
    cdec = jnp.exp(CHUNK * lg)
    bc = lambda t: jnp.broadcast_to(t[..., None], t.shape + (HEAD_DIM,))
    return dmask, bc(qdec), bc(kdec), bc(cdec[:, None])


def _row(v, width=LANE, offset=0):
    out = jnp.zeros((1, width), F32)
    return out.at[0, offset:offset + v.shape[0]].set(v.astype(F32))


def kernel(x, meta_tokens, emb_ln_g, emb_ln_b, w_in, conv_a, a_log, dt_bias, norm_a, hgrn_lb, norm_b, lam_q1, lam_k1,
           lam_q2, lam_k2, subln_c, w_out, ln1_g, ln1_b, w_rg, b_rg, w_re, b_re, w_gate, w_up, w_down, ln2_g, ln2_b):
    bsz, seq, d = x.shape
    depth = w_in.shape[0]
    n_meta = meta_tokens.shape[0]
    w = d // 4
    a_heads = w // HEAD_DIM
    t_valid = n_meta + seq
    front = (-t_valid) % CHUNK
    tpp = -(-(front + t_valid) // ROW_ALIGN) * ROW_ALIGN
    lay = (front, t_valid, tpp)
    alpha = (2 * depth) ** 0.25

    h, hb = _embed_ln(x, meta_tokens, emb_ln_g, emb_ln_b, lay)

    lbs = jax.nn.softmax(hgrn_lb.astype(F32), axis=0)
    lbs = jnp.cumsum(lbs, axis=0) - lbs[0:1]
    rope_c, rope_d = _rope_tables(front, tpp)
    dmask, qdec, kdec, cdec = _retention_tables(a_heads)

    for l in range(depth):
        lam_init = 0.8 - 0.6 * math.exp(-0.3 * l)
        lam = (jnp.exp(jnp.sum(lam_q1[l].astype(F32) * lam_k1[l].astype(F32)))
               - jnp.exp(jnp.sum(lam_q2[l].astype(F32) * lam_k2[l].astype(F32))) + lam_init)
        prm = {
            "conv_w": conv_a[l].astype(F32).reshape(3, w, -1).transpose(0, 2, 1),
            "a_log_row": _row(a_log[l], offset=a_heads),
            "dt_bias_row": _row(dt_bias[l], offset=a_heads),
            "norm_a": norm_a[l].astype(F32).reshape(1, HEAD_DIM),
            "lb": lbs[l].reshape(1, w),
            "norm_b": norm_b[l].astype(F32).reshape(1, HEAD_DIM),
            "rope_c": rope_c, "rope_d": rope_d,
            "lam_row": jnp.full((1, 2 * HEAD_DIM), lam, F32),
            "subln_c": subln_c[l].astype(F32).reshape(1, 2 * HEAD_DIM),
            "c_out_scale": 1.0 - lam_init,
            "ret_dmask": dmask, "ret_qdec": qdec, "ret_kdec": kdec, "ret_cdec": cdec,
        }
        z, zs = _in_proj(hb, w_in, l, w)
        o_parts = _mixers(z, zs, lay, bsz, w, prm)
        y = _matmul(list(o_parts), w_out[l].astype(BF16), F32, 768, 1024)
        n_e = w_re.shape[2]
        w_r = jnp.concatenate([w_re[l], w_rg[l], jnp.zeros((d, LANE - n_e - N_GROUPS), F32)], axis=1).astype(F32)
        b_r = jnp.concatenate([b_re[l], b_rg[l], jnp.zeros((LANE - n_e - N_GROUPS,), F32)]).reshape(1, LANE)
        w_r_hi = w_r.astype(BF16)
        w_r_lo = (w_r - w_r_hi.astype(F32)).astype(BF16)
        h, hb, route = _add_ln(h, y, ln1_g[l], ln1_b[l], alpha, lay, router=(w_r_hi, w_r_lo, b_r))
        y2 = _moe_routed(h, route, w_gate, w_up, w_down, l, MOE_TILE)
        if l + 1 < depth:
            h, hb = _add_ln(h, y2, ln2_g[l], ln2_b[l], alpha, lay, route=route)

    return _final_ln(h, y2, route, ln2_g[depth - 1], ln2_b[depth - 1], alpha, lay, bsz, seq)
```

```python
import functools
import math

import jax
import jax.numpy as jnp
from jax import lax
from jax.experimental import pallas as pl
from jax.experimental.pallas import tpu as pltpu

HEAD_DIM = 128
CHUNK = 64
SUB = 16
GROUP_SHIFT = 3
ROPE_THETA = 500000.0
ROPE_DIMS = HEAD_DIM // 4
RET_THETA = 10000.0
N_GROUPS = 8
EXP_PER_GROUP = 8
NORM_EPS = 1e-6
LN_EPS = 1e-5
LANE = 128
SUBLANES = 8
ROW_ALIGN = 2 * CHUNK
V7X_VMEM_LIMIT_BYTES = 56 * 1024 * 1024
MOE_TILE = 256
NEG = -1e30

F32 = jnp.float32
BF16 = jnp.bfloat16
HI = lax.Precision.HIGHEST


def _cparams(sem):
    return pltpu.CompilerParams(dimension_semantics=sem, vmem_limit_bytes=V7X_VMEM_LIMIT_BYTES)


def _divisor_tile(n, cap, align):
    best = None
    for t in range(align, min(n, cap) + 1, align):
        if n % t == 0:
            best = t
    assert best is not None, (n, cap, align)
    return best


def _dot(a, b):
    return jnp.dot(a, b, preferred_element_type=F32)


def _dot_nt(a, b):
    return lax.dot_general(a, b, (((1,), (1,)), ((), ())), preferred_element_type=F32)


def _dot_tn(a, b):
    return lax.dot_general(a, b, (((0,), (0,)), ((), ())), preferred_element_type=F32)


def _sigmoid(x):
    return 1.0 / (1.0 + jnp.exp(-x))


def _silu(x):
    return x * _sigmoid(x)


def _softplus(x):
    return jnp.maximum(x, 0.0) + jnp.log(1.0 + jnp.exp(-jnp.abs(x)))


def _valid_rows(row0, n, front, t_valid):
    rb = row0 + lax.broadcasted_iota(jnp.int32, (n, 1), 0)
    return (rb >= front) & (rb < front + t_valid)


def _layer_norm_rows(x, g_ref, b_ref):
    mu = jnp.mean(x, axis=-1, keepdims=True)
    xc = x - mu
    var = jnp.mean(xc * xc, axis=-1, keepdims=True)
    return xc * lax.rsqrt(var + LN_EPS) * g_ref[...] + b_ref[...]


def _ln_kernel(*refs, alpha, mode, front, t_valid, tpp, tm):
    h_ref = refs[0]
    if mode == "experts":
        y0_ref, y1_ref, rt_ref, g_ref, b_ref, o_ref, ob_ref = refs[1:]
        x = alpha * h_ref[...] + rt_ref[:, 2:3] * y0_ref[...] + rt_ref[:, 3:4] * y1_ref[...]
    elif mode == "router":
        y_ref, g_ref, b_ref, wh_ref, wl_ref, br_ref, o_ref, ob_ref, ro_ref = refs[1:]
        x = alpha * h_ref[...] + y_ref[...]
    else:
        y_ref, g_ref, b_ref, o_ref, ob_ref = refs[1:]
        x = alpha * h_ref[...] + y_ref[...]
    y = _layer_norm_rows(x, g_ref, b_ref)
    row0 = (pl.program_id(0) * tm) % tpp
    y = jnp.where(_valid_rows(row0, tm, front, t_valid), y, 0.0)
    o_ref[...] = y
    yb = y.astype(BF16)
    ob_ref[...] = yb
    if mode == "router":
        yl = (y - yb.astype(F32)).astype(BF16)
        lg = _dot(yb, wh_ref[...]) + _dot(yl, wh_ref[...]) + _dot(yb, wl_ref[...]) + br_ref[...]
        ro_ref[...] = _route(lg)


def _add_ln(h, y, g, b, alpha, lay, route=None, router=None):
    front, t_valid, tpp = lay
    n, d = h.shape
    tm = _divisor_tile(tpp, 128, 16)
    spec = pl.BlockSpec((tm, d), lambda i: (i, 0))
    vec = pl.BlockSpec((1, d), lambda i: (0, 0))
    lane_rows = pl.BlockSpec((tm, LANE), lambda i: (i, 0))
    out_shape = [jax.ShapeDtypeStruct((n, d), F32), jax.ShapeDtypeStruct((n, d), BF16)]
    out_specs = [spec, spec]
    if route is not None:
        mode = "experts"
        args = (h, y, y, route, g.reshape(1, d), b.reshape(1, d))
        in_specs = [spec, spec, pl.BlockSpec((tm, d), lambda i: (i + n // tm, 0)), lane_rows, vec, vec]
    elif router is not None:
        mode = "router"
        wmat = pl.BlockSpec((d, LANE), lambda i: (0, 0))
        args = (h, y, g.reshape(1, d), b.reshape(1, d)) + tuple(router)
        in_specs = [spec, spec, vec, vec, wmat, wmat, pl.BlockSpec((1, LANE), lambda i: (0, 0))]
        out_shape.append(jax.ShapeDtypeStruct((n, LANE), F32))
        out_specs.append(lane_rows)
    else:
        mode = "plain"
        args = (h, y, g.reshape(1, d), b.reshape(1, d))
        in_specs = [spec, spec, vec, vec]
    kern = functools.partial(_ln_kernel, alpha=alpha, mode=mode, front=front, t_valid=t_valid, tpp=tpp, tm=tm)
    return pl.pallas_call(
        kern, out_shape=tuple(out_shape), grid=(n // tm,), in_specs=in_specs, out_specs=tuple(out_specs),
        compiler_params=_cparams(("parallel",)), name="add_ln_" + mode,
    )(*args)


def _embed_ln_kernel(x_ref, meta_ref, g_ref, b_ref, o_ref, ob_ref, *, front, n_meta, n_blocks):
    j = pl.program_id(1)

    @pl.when((j > 0) & (j <= n_blocks))
    def _():
        y = _layer_norm_rows(x_ref[...], g_ref, b_ref)
        o_ref[...] = y
        ob_ref[...] = y.astype(BF16)

    @pl.when(j == 0)
    def _():
        y = _layer_norm_rows(meta_ref[...], g_ref, b_ref)
        o_ref[...] = jnp.zeros_like(o_ref)
        ob_ref[...] = jnp.zeros_like(ob_ref)
        o_ref[front:front + n_meta, :] = y
        ob_ref[front:front + n_meta, :] = y.astype(BF16)

    @pl.when(j > n_blocks)
    def _():
        o_ref[...] = jnp.zeros_like(o_ref)
        ob_ref[...] = jnp.zeros_like(ob_ref)


def _embed_ln(x, meta_tokens, g, b, lay):
    front, t_valid, tpp = lay
    bsz, seq, d = x.shape
    n_meta = meta_tokens.shape[0]
    assert front + n_meta == CHUNK and seq % CHUNK == 0 and tpp == seq + 2 * CHUNK
    nb = seq // CHUNK
    blocks = tpp // CHUNK
    spec = pl.BlockSpec((CHUNK, d), lambda bb, j: (bb * blocks + j, 0))
    vec = pl.BlockSpec((1, d), lambda bb, j: (0, 0))
    return pl.pallas_call(
        functools.partial(_embed_ln_kernel, front=front, n_meta=n_meta, n_blocks=nb),
        out_shape=(jax.ShapeDtypeStruct((bsz * tpp, d), F32), jax.ShapeDtypeStruct((bsz * tpp, d), BF16)),
        grid=(bsz, blocks),
        in_specs=[pl.BlockSpec((None, CHUNK, d), lambda bb, j: (bb, jnp.clip(j - 1, 0, nb - 1), 0)),
                  pl.BlockSpec((n_meta, d), lambda bb, j: (0, 0)), vec, vec],
        out_specs=(spec, spec),
        compiler_params=_cparams(("parallel", "parallel")), name="embed_ln",
    )(x, meta_tokens.astype(x.dtype), g.reshape(1, d), b.reshape(1, d))


def _final_ln_kernel(h_ref, y0_ref, y1_ref, rt_ref, g_ref, b_ref, o_ref, *, alpha):
    x = alpha * h_ref[...] + rt_ref[:, 2:3] * y0_ref[...] + rt_ref[:, 3:4] * y1_ref[...]
    o_ref[...] = _layer_norm_rows(x, g_ref, b_ref)


def _final_ln(h, y2, route, g, b, alpha, lay, bsz, seq):
    front, t_valid, tpp = lay
    n, d = h.shape
    blocks = tpp // CHUNK
    first = (front + t_valid - seq) // CHUNK
    row = lambda bb, j: bb * blocks + first + j
    return pl.pallas_call(
        functools.partial(_final_ln_kernel, alpha=alpha),
        out_shape=jax.ShapeDtypeStruct((bsz, seq, d), F32),
        grid=(bsz, seq // CHUNK),
        in_specs=[pl.BlockSpec((CHUNK, d), lambda bb, j: (row(bb, j), 0)),
                  pl.BlockSpec((CHUNK, d), lambda bb, j: (row(bb, j), 0)),
                  pl.BlockSpec((CHUNK, d), lambda bb, j: (row(bb, j) + n // CHUNK, 0)),
                  pl.BlockSpec((CHUNK, LANE), lambda bb, j: (row(bb, j), 0)),
                  pl.BlockSpec((1, d), lambda bb, j: (0, 0)), pl.BlockSpec((1, d), lambda bb, j: (0, 0))],
        out_specs=pl.BlockSpec((None, CHUNK, d), lambda bb, j: (bb, j, 0)),
        compiler_params=_cparams(("parallel", "parallel")), name="final_ln",
    )(h, y2, y2, route, g.reshape(1, d), b.reshape(1, d))


def _mm_kernel(*refs, ks):
    a_refs = refs[:len(ks)]
    w_ref, o_ref = refs[len(ks):]
    acc = None
    off = 0
    for a_ref, k in zip(a_refs, ks):
        p = _dot(a_ref[...], w_ref[off:off + k, :])
        acc = p if acc is None else acc + p
        off += k
    o_ref[...] = acc.astype(o_ref.dtype)


def _matmul(a_list, w, out_dtype, tm_cap, tn_cap):
    n = a_list[0].shape[0]
    ks = tuple(a.shape[1] for a in a_list)
    kdim, ndim = w.shape
    assert sum(ks) == kdim
    tm = _divisor_tile(n, tm_cap, 16)
    tn = _divisor_tile(ndim, tn_cap, LANE)
    in_specs = [pl.BlockSpec((tm, k), lambda j, i: (i, 0)) for k in ks]
    in_specs.append(pl.BlockSpec((kdim, tn), lambda j, i: (0, j)))
    return pl.pallas_call(
        functools.partial(_mm_kernel, ks=ks),
        out_shape=jax.ShapeDtypeStruct((n, ndim), out_dtype),
        grid=(ndim // tn, n // tm),
        in_specs=in_specs,
        out_specs=pl.BlockSpec((tm, tn), lambda j, i: (i, j)),
        compiler_params=_cparams(("parallel", "parallel")),
        name="matmul",
    )(*a_list, w)


def _in_proj_kernel(a_ref, w_ref, wn_ref, o_ref, wb_ref, *, n_plain, shift, rows_per_cast):
    j = pl.program_id(0)

    @pl.when(pl.program_id(1) == 0)
    def _():
        kdim = w_ref.shape[0]
        for r0 in range(0, kdim, rows_per_cast):
            rs = slice(r0, r0 + rows_per_cast)

            @pl.when(j < n_plain)
            def _():
                wb_ref[rs, :] = w_ref[rs, :].astype(BF16)

            @pl.when(j >= n_plain)
            def _():
                wb_ref[rs, :] = jnp.concatenate([w_ref[rs, shift:].astype(BF16), wn_ref[rs, :shift].astype(BF16)],
                                                axis=1)

    o_ref[...] = _dot(a_ref[...], wb_ref[...])


def _small_proj_kernel(a_ref, w_ref, o_ref):
    o_ref[...] = _dot(a_ref[...], w_ref[...].astype(BF16))


def _in_proj(hb, w_in, layer, w):
    n, d = hb.shape
    n_small = w_in.shape[2] - 15 * w
    tm = _divisor_tile(n, 1408, 16)
    tn = _divisor_tile(w, 512, LANE)
    z = pl.pallas_call(
        functools.partial(_in_proj_kernel, n_plain=4 * w // tn, shift=n_small, rows_per_cast=min(d, 512)),
        out_shape=jax.ShapeDtypeStruct((n, 15 * w), F32),
        grid=(15 * w // tn, n // tm),
        in_specs=[pl.BlockSpec((tm, d), lambda j, i: (i, 0)),
                  pl.BlockSpec((None, d, tn), lambda j, i: (layer, 0, j)),
                  pl.BlockSpec((None, d, LANE), lambda j, i: (layer, 0, (j + 1) * (tn // LANE)))],
        out_specs=pl.BlockSpec((tm, tn), lambda j, i: (i, j)),
        scratch_shapes=[pltpu.VMEM((d, tn), BF16)],
        compiler_params=_cparams(("parallel", "arbitrary")),
        name="in_proj",
    )(hb, w_in, w_in)
    zs = pl.pallas_call(
        _small_proj_kernel,
        out_shape=jax.ShapeDtypeStruct((n, LANE), F32),
        grid=(n // tm,),
        in_specs=[pl.BlockSpec((tm, d), lambda i: (i, 0)),
                  pl.BlockSpec((None, d, LANE), lambda i: (layer, 0, 4 * w // LANE))],
        out_specs=pl.BlockSpec((tm, LANE), lambda i: (i, 0)),
        compiler_params=_cparams(("parallel",)),
        name="small_proj",
    )(hb, w_in)
    return z, zs


def _tri_masks(n):
    r = lax.broadcasted_iota(jnp.int32, (n, n), 0)
    c = lax.broadcasted_iota(jnp.int32, (n, n), 1)
    return r, c


def _gated_rms(o, w_row, gate):
    o = o * lax.rsqrt(jnp.mean(o * o, axis=-1, keepdims=True) + NORM_EPS)
    if w_row is not None:
        o = o * w_row
    return o * _silu(gate)


def _prefix_rows(x, row, seg):
    pos = row & (seg - 1)
    s = 1
    while s < seg:
        x = x + jnp.where(pos >= s, pltpu.roll(x, s, 0), 0.0)
        s *= 2
    return x


def _mixer_a_kernel(q_ref, k_ref, v_ref, gt_ref, sm_ref, cw_ref, alog_ref, dtb_ref, nw_ref, o_ref,
                    s_ref, prev_ref, *, n_heads, front, t_valid):
    c = pl.program_id(1)

    @pl.when(c == 0)
    def _():
        s_ref[...] = jnp.zeros_like(s_ref)
        prev_ref[...] = jnp.zeros_like(prev_ref)

    row = lax.broadcasted_iota(jnp.int32, (CHUNK, 1), 0)
    row8 = lax.broadcasted_iota(jnp.int32, (SUBLANES, 1), 0)
    valid = _valid_rows(c * CHUNK, CHUNK, front, t_valid).astype(F32)

    def conv_silu(x_ref, p):
        cur = x_ref[...]
        prev8 = prev_ref[p]
        acc = cur * cw_ref[p, 3:4, :]
        for s in (1, 2, 3):
            rolled = pltpu.roll(cur, s, 0)
            top = jnp.where(row8 >= s, rolled[:SUBLANES], pltpu.roll(prev8, s, 0))
            acc = acc + jnp.concatenate([top, rolled[SUBLANES:]], axis=0) * cw_ref[p, 3 - s:4 - s, :]
        prev_ref[p] = cur[CHUNK - SUBLANES:]
        return _silu(acc)

    q_all = conv_silu(q_ref, 0)
    k_all = conv_silu(k_ref, 1) * valid
    v_all = conv_silu(v_ref, 2) * valid

    sm = sm_ref[...]
    beta_all = _sigmoid(sm) * valid
    g_all = -jnp.exp(alog_ref[...]) * _softplus(sm + dtb_ref[...]) * valid
    gcum_all = _prefix_rows(g_all, row, CHUNK)
    r, cc = _tri_masks(CHUNK)
    causal = r >= cc
    strict = r > cc
    eye = (r == cc).astype(F32)
    gcum_t = lax.dot_general(gcum_all, eye, (((0,), (0,)), ((), ())), preferred_element_type=F32,
                             precision=HI)

    heads = range(n_heads)
    hsl = [slice(h * HEAD_DIM, (h + 1) * HEAD_DIM) for h in heads]
    beta = [beta_all[:, h:h + 1] for h in heads]
    gcol = [gcum_all[:, n_heads + h:n_heads + h + 1] for h in heads]
    decay = [jnp.where(causal, jnp.exp(jnp.minimum(gcol[h] - gcum_t[n_heads + h:n_heads + h + 1, :], 0.0)), 0.0)
             for h in heads]
    eg = [jnp.exp(g) for g in gcol]
    g_last = [g[CHUNK - 1:CHUNK, :] for g in gcol]
    q = [q_all[:, s] * lax.rsqrt(jnp.sum(q_all[:, s] * q_all[:, s], axis=-1, keepdims=True) + NORM_EPS)
         * HEAD_DIM ** -0.5 for s in hsl]
    k = [k_all[:, s] * lax.rsqrt(jnp.sum(k_all[:, s] * k_all[:, s], axis=-1, keepdims=True) + NORM_EPS) for s in hsl]
    kb = [k[h] * beta[h] for h in heads]
    k16 = [x.astype(BF16) for x in k]
    m = [jnp.where(strict, -(_dot_nt(kb[h].astype(BF16), k16[h]) * decay[h]), 0.0) for h in heads]
    attn = [_dot_nt(q[h].astype(BF16), k16[h]) * decay[h] for h in heads]
    inv = [eye + x for x in m]
    mp = [_dot(x.astype(BF16), x.astype(BF16)) for x in m]
    for _ in range(4):
        both = [_dot(jnp.concatenate([inv[h], mp[h]], axis=0).astype(BF16), mp[h].astype(BF16)) for h in heads]
        inv = [inv[h] + both[h][:CHUNK] for h in heads]
        mp = [both[h][CHUNK:] for h in heads]
    inv = [inv[h] + _dot(inv[h].astype(BF16), mp[h].astype(BF16)) for h in heads]
    sol = [_dot(inv[h].astype(BF16),
                jnp.concatenate([v_all[:, hsl[h]] * beta[h], kb[h] * eg[h]], axis=1).astype(BF16)) for h in heads]
    s_old = [s_ref[h] for h in heads]
    s16 = [x.astype(BF16) for x in s_old]
    v_new = [sol[h][:, :HEAD_DIM] - _dot(sol[h][:, HEAD_DIM:].astype(BF16), s16[h]) for h in heads]
    o = [_dot((q[h] * eg[h]).astype(BF16), s16[h]) + _dot(attn[h].astype(BF16), v_new[h].astype(BF16)) for h in heads]
    for h in heads:
        s_ref[h] = s_old[h] * jnp.exp(g_last[h]) + _dot_tn(k[h] * jnp.exp(g_last[h] - gcol[h]), v_new[h])
        o_ref[:, hsl[h]] = _gated_rms(o[h], nw_ref[...], gt_ref[:, hsl[h]]).astype(o_ref.dtype)


def _mixer_b_kernel(q_ref, f_ref, i_ref, gt_ref, lb_ref, nw_ref, o_ref, st_ref, kbuf, bbuf, obuf,
                    *, n_heads, front, t_valid):
    c = pl.program_id(1)

    @pl.when(c == 0)
    def _():
        st_ref[...] = jnp.zeros_like(st_ref)

    row = lax.broadcasted_iota(jnp.int32, (CHUNK, 1), 0)
    valid = _valid_rows(c * CHUNK, CHUNK, front, t_valid).astype(F32)
    row16 = lax.broadcasted_iota(jnp.int32, (SUB, 1), 0)

    lb = lb_ref[...]
    f = lb + (1.0 - lb) * _sigmoid(f_ref[...])
    kbuf[...] = (1.0 - f) * valid
    bbuf[...] = _prefix_rows(jnp.log(f) * valid, row, SUB)

    heads = range(n_heads)
    hsl = [slice(h * HEAD_DIM, (h + 1) * HEAD_DIM) for h in heads]
    st = [st_ref[h] for h in heads]
    for blk in range(CHUNK // SUB):
        rs = slice(blk * SUB, (blk + 1) * SUB)
        for h in heads:
            hs = hsl[h]
            b16 = bbuf[rs, hs]
            q16 = _silu(q_ref[rs, hs])
            b_end = b16[SUB - 1:SUB, :]
            acc = _dot_nt((q16 * jnp.exp(b16)).astype(BF16), st[h].astype(BF16))
            parts = []
            for t0 in range(0, SUB, SUBLANES):
                ts = slice(t0, t0 + SUBLANES)
                bq, qq, rowq = b16[ts], q16[ts], row16[ts]
                part = acc[ts]
                for j in range(min(t0 + SUBLANES, SUB)):
                    jr = pl.ds(blk * SUB + j, 1)
                    sc = jnp.sum(qq * kbuf[jr, hs] * jnp.exp(bq - bbuf[jr, hs]), axis=-1, keepdims=True)
                    if j > t0:
                        sc = jnp.where(rowq >= j, sc, 0.0)
                    part = part + sc * i_ref[jr, hs]
                parts.append(part)
            acc = jnp.concatenate(parts, axis=0)
            st[h] = st[h] * jnp.exp(b_end) + _dot_tn(i_ref[rs, hs], kbuf[rs, hs] * jnp.exp(b_end - b16))
            obuf[rs, hs] = acc
    for h in heads:
        st_ref[h] = st[h]
        o_ref[:, hsl[h]] = _gated_rms(obuf[:, hsl[h]], nw_ref[...], gt_ref[:, hsl[h]]).astype(o_ref.dtype)


def _rope_partial(x, cos, sin_lo, sin_hi):
    half = ROPE_DIMS // 2
    return x * cos + pltpu.roll(x, LANE - half, 1) * sin_lo + pltpu.roll(x, half, 1) * sin_hi


def _rope_c_kernel(q_ref, k_ref, v_ref, cos_ref, slo_ref, shi_ref, qo_ref, ko_ref, vo_ref, *, n_maps):
    cos, slo, shi = cos_ref[...], slo_ref[...], shi_ref[...]
    for mp in range(n_maps):
        ms = slice(mp * HEAD_DIM, (mp + 1) * HEAD_DIM)
        qo_ref[:, ms] = (_rope_partial(q_ref[:, ms], cos, slo, shi) * HEAD_DIM ** -0.5).astype(BF16)
        ko_ref[:, ms] = _rope_partial(k_ref[:, ms], cos, slo, shi).astype(BF16)
    vo_ref[...] = v_ref[...].astype(BF16)


def _mixer_c_kernel(qi_ref, kj_ref, q_ref, k_ref, v_ref, lam_ref, nw_ref, o_ref, m_ref, l_ref, acc_ref,
                    *, tq, front, out_scale):
    p = pl.program_id(2)
    i = qi_ref[p]
    j = kj_ref[p]

    @pl.when(j == 0)
    def _():
        m_ref[...] = jnp.full_like(m_ref, NEG)
        l_ref[...] = jnp.zeros_like(l_ref)
        acc_ref[...] = jnp.zeros_like(acc_ref)

    def step(masked):
        v = v_ref[...]
        if masked:
            rq = i * tq + lax.broadcasted_iota(jnp.int32, (tq, 1), 0)
            rk = j * tq + lax.broadcasted_iota(jnp.int32, (1, tq), 1)
            msk = (rk <= rq) & (rk >= front)
        maps = range(2)
        msl = [slice(mp * HEAD_DIM, (mp + 1) * HEAD_DIM) for mp in maps]
        s = [_dot_nt(q_ref[:, ms], k_ref[:, ms]) for ms in msl]
        if masked:
            s = [jnp.where(msk, x, NEG) for x in s]
        m_prev = [m_ref[mp] for mp in maps]
        m_new = [jnp.maximum(m_prev[mp], jnp.max(s[mp], axis=-1, keepdims=True)) for mp in maps]
        pr = [jnp.exp(s[mp] - jnp.tile(m_new[mp], (1, tq // LANE))) for mp in maps]
        pv = [_dot(pr[mp].astype(BF16), v) for mp in maps]
        for mp in maps:
            a = jnp.exp(m_prev[mp] - m_new[mp])
            l_ref[mp] = a * l_ref[mp] + jnp.sum(pr[mp], axis=-1, keepdims=True)
            acc_ref[mp] = jnp.tile(a, (1, 2 * HEAD_DIM // LANE)) * acc_ref[mp] + pv[mp]
            m_ref[mp] = m_new[mp]

    needs_mask = (j == i) | (j == 0)

    @pl.when(needs_mask)
    def _():
        step(True)

    @pl.when(jnp.logical_not(needs_mask))
    def _():
        step(False)

    @pl.when(j == i)
    def _():
        reps = (1, 2 * HEAD_DIM // LANE)
        o = acc_ref[0] / jnp.tile(l_ref[0], reps) - lam_ref[...] * (acc_ref[1] / jnp.tile(l_ref[1], reps))
        o = o * lax.rsqrt(jnp.mean(o * o, axis=-1, keepdims=True) + NORM_EPS) * nw_ref[...] * out_scale
        o_ref[...] = o.astype(o_ref.dtype)


def _mixer_d_kernel(q_ref, k_ref, v_ref, gt_ref, cos_ref, sin_ref, dm_ref, qd_ref, kd_ref, cd_ref, o_ref, s_ref,
                    *, n_heads):
    c = pl.program_id(1)

    @pl.when(c == 0)
    def _():
        s_ref[...] = jnp.zeros_like(s_ref)

    cos = cos_ref[...]
    sin = sin_ref[...]
    heads = range(n_heads)
    hsl = [slice(h * HEAD_DIM, (h + 1) * HEAD_DIM) for h in heads]
    q = [q_ref[:, s] * cos + pltpu.roll(q_ref[:, s], HEAD_DIM // 2, 1) * sin for s in hsl]
    k = [(k_ref[:, s] * cos + pltpu.roll(k_ref[:, s], HEAD_DIM // 2, 1) * sin) * HEAD_DIM ** -0.5 for s in hsl]
    s_old = [s_ref[h] for h in heads]
    a = [_dot_nt(q[h].astype(BF16), k[h].astype(BF16)) * dm_ref[h] for h in heads]
    o = [_dot(a[h].astype(BF16), v_ref[:, hsl[h]].astype(BF16))
         + _dot((q[h] * qd_ref[h]).astype(BF16), s_old[h].astype(BF16)) for h in heads]
    for h in heads:
        s_ref[h] = cd_ref[h] * s_old[h] + _dot_tn(k[h] * kd_ref[h], v_ref[:, hsl[h]])
        o_ref[:, hsl[h]] = _gated_rms(o[h], None, gt_ref[:, hsl[h]]).astype(o_ref.dtype)


def _route(lg):
    n_e = N_GROUPS * EXP_PER_GROUP
    lane_i = lax.broadcasted_iota(jnp.int32, lg.shape, 1)
    lane = lane_i.astype(F32)
    grp = (lane_i >> GROUP_SHIFT).astype(F32)
    big = float(4 * LANE)
    is_g = (lane_i >= n_e) & (lane_i < n_e + N_GROUPS)
    gl = jnp.where(is_g, lg, NEG)
    gmax = jnp.max(gl, axis=-1, keepdims=True)
    gsum = jnp.sum(jnp.where(is_g, jnp.exp(gl - gmax), 0.0), axis=-1, keepdims=True)
    g_w = 1.0 / gsum
    g_idx = jnp.min(jnp.where(is_g & (gl == gmax), lane - n_e, big), axis=-1, keepdims=True)
    in_grp = (lane_i < n_e) & (grp == g_idx)
    el = jnp.where(in_grp, lg, NEG)
    emax = jnp.max(el, axis=-1, keepdims=True)
    eexp = jnp.where(in_grp, jnp.exp(el - emax), 0.0)
    pe = eexp / jnp.sum(eexp, axis=-1, keepdims=True)
    p1 = jnp.max(jnp.where(in_grp, pe, -1.0), axis=-1, keepdims=True)
    i1 = jnp.min(jnp.where(in_grp & (pe == p1), lane, big), axis=-1, keepdims=True)
    rest = in_grp & (lane != i1)
    p2 = jnp.max(jnp.where(rest, pe, -1.0), axis=-1, keepdims=True)
    i2 = jnp.min(jnp.where(rest & (pe == p2), lane, big), axis=-1, keepdims=True)
    den = p1 + p2
    return (jnp.where(lane_i == 0, i1, 0.0) + jnp.where(lane_i == 1, i2, 0.0)
            + jnp.where(lane_i == 2, g_w * (p1 / den), 0.0) + jnp.where(lane_i == 3, g_w * (p2 / den), 0.0))


def _row_copy(src, src_row, dst, dst_row, sem):
    return pltpu.make_async_copy(src.at[pl.ds(src_row, 1)], dst.at[pl.ds(dst_row, 1)], sem)


def _moe_kernel(te_ref, cnt_ref, first_ref, src_ref, srcn_ref, dst_ref, x_hbm, wg_ref, wu_ref, wd_ref, y_hbm,
                xbuf, ybuf, wgb, wub, wdb, sem_in, sem_out, *, tm, n_tiles):
    del te_ref
    t = pl.program_id(0)
    slot = t % 2
    cnt = cnt_ref[t]

    def pairs(rows):
        return (rows + 1) // 2

    def gather_start(idx_ref, s, rows):
        def body(g, c):
            for u in range(2):
                r = 2 * g + u
                _row_copy(x_hbm, idx_ref[0, 0, r], xbuf.at[s], r, sem_in.at[s]).start(priority=u)
            return c
        lax.fori_loop(0, pairs(rows), body, 0)

    def wait_rows(src, dst, sem, rows):
        def tile_body(g, c):
            pltpu.make_async_copy(src.at[pl.ds(0, SUBLANES)], dst.at[pl.ds(g * SUBLANES, SUBLANES)], sem).wait()
            return c
        lax.fori_loop(0, rows // SUBLANES, tile_body, 0)

        def row_body(r, c):
            _row_copy(src, 0, dst, r, sem).wait()
            return c
        lax.fori_loop(rows - rows % SUBLANES, rows, row_body, 0)

    def gather_wait(s, rows):
        wait_rows(x_hbm, xbuf.at[s], sem_in.at[s], 2 * pairs(rows))

    def scatter_start(s, rows):
        def body(g, c):
            for u in range(2):
                r = 2 * g + u
                _row_copy(ybuf.at[s], r, y_hbm, dst_ref[0, 0, r], sem_out.at[s]).start(priority=u)
            return c
        lax.fori_loop(0, rows // 2, body, 0)

        @pl.when(rows % 2 == 1)
        def _():
            _row_copy(ybuf.at[s], rows - 1, y_hbm, dst_ref[0, 0, rows - 1], sem_out.at[s]).start()

    def scatter_wait(s, rows):
        wait_rows(ybuf.at[s], y_hbm, sem_out.at[s], rows)

    @pl.when(t == 0)
    def _():
        xbuf[...] = jnp.zeros_like(xbuf)

        @pl.when(cnt > 0)
        def _():
            gather_start(src_ref, 0, cnt)

    nxt = jnp.minimum(t + 1, n_tiles - 1)

    @pl.when((t + 1 < n_tiles) & (cnt_ref[nxt] > 0))
    def _():
        gather_start(srcn_ref, 1 - slot, cnt_ref[nxt])

    old = jnp.maximum(t - 2, 0)

    @pl.when((t >= 2) & (cnt_ref[old] > 0))
    def _():
        scatter_wait(slot, cnt_ref[old])

    @pl.when(cnt > 0)
    def _():
        @pl.when(first_ref[t] == 1)
        def _():
            wgb[...] = wg_ref[...].astype(BF16)
            wub[...] = wu_ref[...].astype(BF16)
            wdb[...] = wd_ref[...].astype(BF16)

        gather_wait(slot, cnt)
        x = xbuf[slot].astype(BF16)
        hg = _dot(x, wgb[...])
        hu = _dot(x, wub[...])
        act = (_silu(hg) * hu).astype(BF16)
        ybuf[slot] = _dot(act, wdb[...])
        scatter_start(slot, cnt)

    @pl.when(t == n_tiles - 1)
    def _():
        prev = jnp.maximum(t - 1, 0)

        @pl.when((t >= 1) & (cnt_ref[prev] > 0))
        def _():
            scatter_wait(1 - slot, cnt_ref[prev])

        @pl.when(cnt > 0)
        def _():
            scatter_wait(slot, cnt)


def _dispatch(route, n, tm, n_e):
    ids = jnp.concatenate([route[:, 0], route[:, 1]]).astype(jnp.int32)
    n_pairs = 2 * n
    n_tiles = n_pairs // tm + n_e
    order = jnp.argsort(ids, stable=True).astype(jnp.int32)
    counts = jnp.sum((ids[:, None] == jnp.arange(n_e, dtype=jnp.int32)[None, :]).astype(jnp.int32), axis=0)
    tiles_e = (counts + tm - 1) // tm
    tile_end = jnp.cumsum(tiles_e)
    tile_start = tile_end - tiles_e
    first = jnp.cumsum(counts) - counts
    tile = jnp.arange(n_tiles, dtype=jnp.int32)
    te = jnp.minimum(jnp.searchsorted(tile_end, tile, side="right").astype(jnp.int32), n_e - 1)
    cnt = jnp.clip(counts[te] - (tile - tile_start[te]) * tm, 0, tm)
    cnt = jnp.where(tile < tile_end[n_e - 1], cnt, 0).astype(jnp.int32)
    row = jnp.arange(tm, dtype=jnp.int32)[None, :]
    spos = (first[te] + (tile - tile_start[te]) * tm)[:, None] + row
    dst = jnp.where(row < cnt[:, None], order[jnp.clip(spos, 0, n_pairs - 1)], 0)
    src = dst % n
    te = jnp.where(cnt > 0, te, jnp.max(jnp.where(cnt > 0, te, 0)))
    new_expert = jnp.concatenate([jnp.ones((1,), jnp.int32), (te[1:] != te[:-1]).astype(jnp.int32)])
    return te, cnt, new_expert, src.reshape(n_tiles, 1, tm), dst.reshape(n_tiles, 1, tm)


def _moe_routed(h, route, wg, wu, wd, layer, tm):
    n, d = h.shape
    _, n_e, _, f = wg.shape
    te, cnt, new_expert, src, dst = _dispatch(route, n, tm, n_e)
    n_tiles = te.shape[0]

    def idx_spec(shift):
        return pl.BlockSpec((1, 1, tm), lambda t, *_: (jnp.minimum(t + shift, n_tiles - 1), 0, 0),
                            memory_space=pltpu.SMEM)

    grid_spec = pltpu.PrefetchScalarGridSpec(
        num_scalar_prefetch=3, grid=(n_tiles,),
        in_specs=[idx_spec(0), idx_spec(1), idx_spec(0),
                  pl.BlockSpec(memory_space=pl.ANY),
                  pl.BlockSpec((None, None, d, f), lambda t, te, *_: (layer, te[t], 0, 0)),
                  pl.BlockSpec((None, None, d, f), lambda t, te, *_: (layer, te[t], 0, 0)),
                  pl.BlockSpec((None, None, f, d), lambda t, te, *_: (layer, te[t], 0, 0))],
        out_specs=pl.BlockSpec(memory_space=pl.ANY),
        scratch_shapes=[pltpu.VMEM((2, tm, d), F32), pltpu.VMEM((2, tm, d), F32),
                        pltpu.VMEM((d, f), BF16), pltpu.VMEM((d, f), BF16), pltpu.VMEM((f, d), BF16),
                        pltpu.SemaphoreType.DMA((2,)), pltpu.SemaphoreType.DMA((2,))])
    return pl.pallas_call(
        functools.partial(_moe_kernel, tm=tm, n_tiles=n_tiles),
        out_shape=jax.ShapeDtypeStruct((2 * n, d), F32),
        grid_spec=grid_spec,
        compiler_params=_cparams(("arbitrary",)),
        name="moe_routed",
    )(te, cnt, new_expert, src, src, dst, h, wg, wu, wd)


def _mixers(z, zs, lay, bsz, w, prm):
    front, t_valid, tpp = lay
    n = z.shape[0]
    nc = tpp // CHUNK
    heads = w // HEAD_DIM
    c_heads = w // (2 * HEAD_DIM)

    def cspec(region):
        return pl.BlockSpec((CHUNK, w), lambda b, c, region=region: (b * nc + c, region))

    def fixed(shape):
        nd = len(shape)
        return pl.BlockSpec(shape, lambda b, c: (0,) * nd)

    ospec = pl.BlockSpec((CHUNK, w), lambda b, c: (b * nc + c, 0))
    oshape = jax.ShapeDtypeStruct((n, w), BF16)
    grid = (bsz, nc)
    sem = ("parallel", "arbitrary")
    state = pltpu.VMEM((heads, HEAD_DIM, HEAD_DIM), F32)

    oa = pl.pallas_call(
        functools.partial(_mixer_a_kernel, n_heads=heads, front=front, t_valid=t_valid),
        out_shape=oshape, grid=grid,
        in_specs=[cspec(0), cspec(1), cspec(2), cspec(3),
                  pl.BlockSpec((CHUNK, LANE), lambda b, c: (b * nc + c, 0)),
                  fixed((3, 4, w)), fixed((1, LANE)), fixed((1, LANE)), fixed((1, HEAD_DIM))],
        out_specs=ospec,
        scratch_shapes=[state, pltpu.VMEM((3, SUBLANES, w), F32)],
        compiler_params=_cparams(sem), name="mixer_a",
    )(z, z, z, z, zs, prm["conv_w"], prm["a_log_row"], prm["dt_bias_row"], prm["norm_a"])

    ob = pl.pallas_call(
        functools.partial(_mixer_b_kernel, n_heads=heads, front=front, t_valid=t_valid),
        out_shape=oshape, grid=grid,
        in_specs=[cspec(4), cspec(5), cspec(6), cspec(7), fixed((1, w)), fixed((1, HEAD_DIM))],
        out_specs=ospec,
        scratch_shapes=[state] + [pltpu.VMEM((CHUNK, w), F32)] * 3,
        compiler_params=_cparams(sem), name="mixer_b",
    )(z, z, z, z, prm["lb"], prm["norm_b"])

    tq = _divisor_tile(tpp, 384, LANE)
    nq = tpp // tq
    vw = 2 * HEAD_DIM
    rc = prm["rope_c"]
    rspec = pl.BlockSpec((tq, w), lambda i: (i, 0))
    tab = pl.BlockSpec((tq, HEAD_DIM), lambda i: (i % nq, 0))
    qr, kr, vr = pl.pallas_call(
        functools.partial(_rope_c_kernel, n_maps=heads),
        out_shape=(oshape, oshape, oshape), grid=(n // tq,),
        in_specs=[pl.BlockSpec((tq, w), lambda i: (i, 8)), pl.BlockSpec((tq, w), lambda i: (i, 9)),
                  pl.BlockSpec((tq, w), lambda i: (i, 10)), tab, tab, tab],
        out_specs=(rspec, rspec, rspec),
        compiler_params=_cparams(("parallel",)), name="rope_c",
    )(z, z, z, rc[0], rc[1], rc[2])

    pairs = [(i, j) for i in range(nq) for j in range(i + 1)]
    qi = jnp.asarray([p[0] for p in pairs], jnp.int32)
    kj = jnp.asarray([p[1] for p in pairs], jnp.int32)
    qspec = pl.BlockSpec((tq, vw), lambda b, h, p, qi, kj: (b * nq + qi[p], h))
    kspec = pl.BlockSpec((tq, vw), lambda b, h, p, qi, kj: (b * nq + kj[p], h))
    row_vw = pl.BlockSpec((1, vw), lambda b, h, p, qi, kj: (0, 0))
    oc = pl.pallas_call(
        functools.partial(_mixer_c_kernel, tq=tq, front=front, out_scale=prm["c_out_scale"]),
        out_shape=oshape,
        grid_spec=pltpu.PrefetchScalarGridSpec(
            num_scalar_prefetch=2, grid=(bsz, c_heads, len(pairs)),
            in_specs=[qspec, kspec, kspec, row_vw, row_vw],
            out_specs=qspec,
            scratch_shapes=[pltpu.VMEM((2, tq, LANE), F32), pltpu.VMEM((2, tq, LANE), F32),
                            pltpu.VMEM((2, tq, vw), F32)]),
        compiler_params=_cparams(("parallel", "parallel", "arbitrary")), name="mixer_c",
    )(qi, kj, qr, kr, vr, prm["lam_row"], prm["subln_c"])

    tabd = pl.BlockSpec((CHUNK, HEAD_DIM), lambda b, c: (c, 0))
    rd = prm["rope_d"]
    od = pl.pallas_call(
        functools.partial(_mixer_d_kernel, n_heads=heads),
        out_shape=oshape, grid=grid,
        in_specs=[cspec(11), cspec(12), cspec(13), cspec(14), tabd, tabd,
                  fixed((heads, CHUNK, CHUNK)), fixed((heads, CHUNK, HEAD_DIM)), fixed((heads, CHUNK, HEAD_DIM)),
                  fixed((heads, 1, HEAD_DIM))],
        out_specs=ospec,
        scratch_shapes=[state],
        compiler_params=_cparams(sem), name="mixer_d",
    )(z, z, z, z, rd[0], rd[1], prm["ret_dmask"], prm["ret_qdec"], prm["ret_kdec"], prm["ret_cdec"])
    return oa, ob, oc, od


def _rope_tables(front, tpp):
    pos = (jnp.arange(tpp) - front).astype(F32)[:, None]
    half = ROPE_DIMS // 2
    inv = 1.0 / (ROPE_THETA ** (jnp.arange(half, dtype=F32) / half))
    ang = pos * inv[None, :]
    z_rest = jnp.zeros((tpp, HEAD_DIM - ROPE_DIMS), F32)
    z_half = jnp.zeros((tpp, half), F32)
    cos_c = jnp.concatenate([jnp.cos(ang), jnp.cos(ang), jnp.ones_like(z_rest)], axis=1)
    sin_lo = jnp.concatenate([-jnp.sin(ang), z_half, z_rest], axis=1)
    sin_hi = jnp.concatenate([z_half, jnp.sin(ang), z_rest], axis=1)
    half_d = HEAD_DIM // 2
    inv_d = 1.0 / (RET_THETA ** (jnp.arange(half_d, dtype=F32) / half_d))
    ang_d = pos * inv_d[None, :]
    cos_d = jnp.concatenate([jnp.cos(ang_d), jnp.cos(ang_d)], axis=1)
    sin_d = jnp.concatenate([-jnp.sin(ang_d), jnp.sin(ang_d)], axis=1)
    return (cos_c, sin_lo, sin_hi), (cos_d, sin_d)


def _retention_tables(n_heads):
    lg = jnp.log(1.0 - 2.0 ** (-5.0 - jnp.arange(n_heads, dtype=F32)))
    idx = jnp.arange(CHUNK, dtype=F32)
    rel = idx[:, None] - idx[None, :]
    causal = idx[:, None] >= idx[None, :]
    dmask = jnp.exp(jnp.where(causal[None], rel[None] * lg[:, None, None], -jnp.inf))
    qdec = jnp.exp((idx[None, :] + 1.0) * lg[:, None])
    kdec = jnp.exp((CHUNK - 1.0 - idx[None, :]) * lg[:, None])
    cdec = jnp.exp(CHUNK * lg)
    bc = lambda t: jnp.broadcast_to(t[..., None], t.shape + (HEAD_DIM,))
    return dmask, bc(qdec), bc(kdec), bc(cdec[:, None])


def _row(v, width=LANE, offset=0):
    out = jnp.zeros((1, width), F32)
    return out.at[0, offset:offset + v.shape[0]].set(v.astype(F32))


def kernel(x, meta_tokens, emb_ln_g, emb_ln_b, w_in, conv_a, a_log, dt_bias, norm_a, hgrn_lb, norm_b, lam_q1, lam_k1,
           lam_q2, lam_k2, subln_c, w_out, ln1_g, ln1_b, w_rg, b_rg, w_re, b_re, w_gate, w_up, w_down, ln2_g, ln2_b):
    bsz, seq, d = x.shape
    depth = w_in.shape[0]
    n_meta = meta_tokens.shape[0]
    w = d // 4
    a_heads = w // HEAD_DIM
    t_valid = n_meta + seq
    front = (-t_valid) % CHUNK
    tpp = -(-(front + t_valid) // ROW_ALIGN) * ROW_ALIGN
    lay = (front, t_valid, tpp)
    alpha = (2 * depth) ** 0.25

    h, hb = _embed_ln(x, meta_tokens, emb_ln_g, emb_ln_b, lay)

    lbs = jax.nn.softmax(hgrn_lb.astype(F32), axis=0)
    lbs = jnp.cumsum(lbs, axis=0) - lbs[0:1]
    rope_c, rope_d = _rope_tables(front, tpp)
    dmask, qdec, kdec, cdec = _retention_tables(a_heads)

    for l in range(depth):
        lam_init = 0.8 - 0.6 * math.exp(-0.3 * l)
        lam = (jnp.exp(jnp.sum(lam_q1[l].astype(F32) * lam_k1[l].astype(F32)))
               - jnp.exp(jnp.sum(lam_q2[l].astype(F32) * lam_k2[l].astype(F32))) + lam_init)
        prm = {
            "conv_w": conv_a[l].astype(F32).reshape(3, w, -1).transpose(0, 2, 1),
            "a_log_row": _row(a_log[l], offset=a_heads),
            "dt_bias_row": _row(dt_bias[l], offset=a_heads),
            "norm_a": norm_a[l].astype(F32).reshape(1, HEAD_DIM),
            "lb": lbs[l].reshape(1, w),
            "norm_b": norm_b[l].astype(F32).reshape(1, HEAD_DIM),
            "rope_c": rope_c, "rope_d": rope_d,
            "lam_row": jnp.full((1, 2 * HEAD_DIM), lam, F32),
            "subln_c": subln_c[l].astype(F32).reshape(1, 2 * HEAD_DIM),
            "c_out_scale": 1.0 - lam_init,
            "ret_dmask": dmask, "ret_qdec": qdec, "ret_kdec": kdec, "ret_cdec": cdec,
        }
        z, zs = _in_proj(hb, w_in, l, w)
        o_parts = _mixers(z, zs, lay, bsz, w, prm)
        y = _matmul(list(o_parts), w_out[l].astype(BF16), F32, 768, 1024)
        n_e = w_re.shape[2]
        w_r = jnp.concatenate([w_re[l], w_rg[l], jnp.zeros((d, LANE - n_e - N_GROUPS), F32)], axis=1).astype(F32)
        b_r = jnp.concatenate([b_re[l], b_rg[l], jnp.zeros((LANE - n_e - N_GROUPS,), F32)]).reshape(1, LANE)
        w_r_hi = w_r.astype(BF16)
        w_r_lo = (w_r - w_r_hi.astype(F32)).astype(BF16)
        h, hb, route = _add_ln(h, y, ln1_g[l], ln1_b[l], alpha, lay, router=(w_r_hi, w_r_lo, b_r))
        y2 = _moe_routed(h, route, w_gate, w_up, w_down, l, MOE_TILE)
        if l + 1 < depth:
            h, hb = _add_ln(h, y2, ln2_g[l], ln2_b[l], alpha, lay, route=route)

    return _final_ln(h, y2, route, ln2_g[depth - 1], ln2_b[depth - 1], alpha, lay, bsz, seq)
```

```python
import functools
import math

import jax
import jax.numpy as jnp
from jax import lax
from jax.experimental import pallas as pl
from jax.experimental.pallas import tpu as pltpu

HEAD_DIM = 128
CHUNK = 64
SUB = 16
GROUP_SHIFT = 3
ROPE_THETA = 500000.0
ROPE_DIMS = HEAD_DIM // 4
RET_THETA = 10000.0
N_GROUPS = 8
EXP_PER_GROUP = 8
NORM_EPS = 1e-6
LN_EPS = 1e-5
LANE = 128
SUBLANES = 8
ROW_ALIGN = 2 * CHUNK
V7X_VMEM_LIMIT_BYTES = 56 * 1024 * 1024
MOE_TILE = 256
NEG = -1e30

F32 = jnp.float32
BF16 = jnp.bfloat16
HI = lax.Precision.HIGHEST


def _cparams(sem):
    return pltpu.CompilerParams(dimension_semantics=sem, vmem_limit_bytes=V7X_VMEM_LIMIT_BYTES)


def _divisor_tile(n, cap, align):
    best = None
    for t in range(align, min(n, cap) + 1, align):
        if n % t == 0:
            best = t
    assert best is not None, (n, cap, align)
    return best


def _dot(a, b):
    return jnp.dot(a, b, preferred_element_type=F32)


def _dot_nt(a, b):
    return lax.dot_general(a, b, (((1,), (1,)), ((), ())), preferred_element_type=F32)


def _dot_tn(a, b):
    return lax.dot_general(a, b, (((0,), (0,)), ((), ())), preferred_element_type=F32)


def _sigmoid(x):
    return 1.0 / (1.0 + jnp.exp(-x))


def _silu(x):
    return x * _sigmoid(x)


def _softplus(x):
    return jnp.maximum(x, 0.0) + jnp.log(1.0 + jnp.exp(-jnp.abs(x)))


def _valid_rows(row0, n, front, t_valid):
    rb = row0 + lax.broadcasted_iota(jnp.int32, (n, 1), 0)
    return (rb >= front) & (rb < front + t_valid)


def _layer_norm_rows(x, g_ref, b_ref):
    mu = jnp.mean(x, axis=-1, keepdims=True)
    xc = x - mu
    var = jnp.mean(xc * xc, axis=-1, keepdims=True)
    return xc * lax.rsqrt(var + LN_EPS) * g_ref[...] + b_ref[...]


def _ln_kernel(*refs, alpha, mode, front, t_valid, tpp, tm):
    h_ref = refs[0]
    if mode == "experts":
        y0_ref, y1_ref, rt_ref, g_ref, b_ref, o_ref, ob_ref = refs[1:]
        x = alpha * h_ref[...] + rt_ref[:, 2:3] * y0_ref[...] + rt_ref[:, 3:4] * y1_ref[...]
    elif mode == "router":
        y_ref, g_ref, b_ref, wh_ref, wl_ref, br_ref, o_ref, ob_ref, ro_ref = refs[1:]
        x = alpha * h_ref[...] + y_ref[...]
    else:
        y_ref, g_ref, b_ref, o_ref, ob_ref = refs[1:]
        x = alpha * h_ref[...] + y_ref[...]
    y = _layer_norm_rows(x, g_ref, b_ref)
    row0 = (pl.program_id(0) * tm) % tpp
    y = jnp.where(_valid_rows(row0, tm, front, t_valid), y, 0.0)
    o_ref[...] = y
    yb = y.astype(BF16)
    ob_ref[...] = yb
    if mode == "router":
        yl = (y - yb.astype(F32)).astype(BF16)
        lg = _dot(yb, wh_ref[...]) + _dot(yl, wh_ref[...]) + _dot(yb, wl_ref[...]) + br_ref[...]
        ro_ref[...] = _route(lg)


def _add_ln(h, y, g, b, alpha, lay, route=None, router=None):
    front, t_valid, tpp = lay
    n, d = h.shape
    tm = _divisor_tile(tpp, 128, 16)
    spec = pl.BlockSpec((tm, d), lambda i: (i, 0))
    vec = pl.BlockSpec((1, d), lambda i: (0, 0))
    lane_rows = pl.BlockSpec((tm, LANE), lambda i: (i, 0))
    out_shape = [jax.ShapeDtypeStruct((n, d), F32), jax.ShapeDtypeStruct((n, d), BF16)]
    out_specs = [spec, spec]
    if route is not None:
        mode = "experts"
        args = (h, y, y, route, g.reshape(1, d), b.reshape(1, d))
        in_specs = [spec, spec, pl.BlockSpec((tm, d), lambda i: (i + n // tm, 0)), lane_rows, vec, vec]
    elif router is not None:
        mode = "router"
        wmat = pl.BlockSpec((d, LANE), lambda i: (0, 0))
        args = (h, y, g.reshape(1, d), b.reshape(1, d)) + tuple(router)
        in_specs = [spec, spec, vec, vec, wmat, wmat, pl.BlockSpec((1, LANE), lambda i: (0, 0))]
        out_shape.append(jax.ShapeDtypeStruct((n, LANE), F32))
        out_specs.append(lane_rows)
    else:
        mode = "plain"
        args = (h, y, g.reshape(1, d), b.reshape(1, d))
        in_specs = [spec, spec, vec, vec]
    kern = functools.partial(_ln_kernel, alpha=alpha, mode=mode, front=front, t_valid=t_valid, tpp=tpp, tm=tm)
    return pl.pallas_call(
        kern, out_shape=tuple(out_shape), grid=(n // tm,), in_specs=in_specs, out_specs=tuple(out_specs),
        compiler_params=_cparams(("parallel",)), name="add_ln_" + mode,
    )(*args)


def _embed_ln_kernel(x_ref, meta_ref, g_ref, b_ref, o_ref, ob_ref, *, front, n_meta, n_blocks):
    j = pl.program_id(1)

    @pl.when((j > 0) & (j <= n_blocks))
    def _():
        y = _layer_norm_rows(x_ref[...], g_ref, b_ref)
        o_ref[...] = y
        ob_ref[...] = y.astype(BF16)

    @pl.when(j == 0)
    def _():
        y = _layer_norm_rows(meta_ref[...], g_ref, b_ref)
        o_ref[...] = jnp.zeros_like(o_ref)
        ob_ref[...] = jnp.zeros_like(ob_ref)
        o_ref[front:front + n_meta, :] = y
        ob_ref[front:front + n_meta, :] = y.astype(BF16)

    @pl.when(j > n_blocks)
    def _():
        o_ref[...] = jnp.zeros_like(o_ref)
        ob_ref[...] = jnp.zeros_like(ob_ref)


def _embed_ln(x, meta_tokens, g, b, lay):
    front, t_valid, tpp = lay
    bsz, seq, d = x.shape
    n_meta = meta_tokens.shape[0]
    assert front + n_meta == CHUNK and seq % CHUNK == 0 and tpp == seq + 2 * CHUNK
    nb = seq // CHUNK
    blocks = tpp // CHUNK
    spec = pl.BlockSpec((CHUNK, d), lambda bb, j: (bb * blocks + j, 0))
    vec = pl.BlockSpec((1, d), lambda bb, j: (0, 0))
    return pl.pallas_call(
        functools.partial(_embed_ln_kernel, front=front, n_meta=n_meta, n_blocks=nb),
        out_shape=(jax.ShapeDtypeStruct((bsz * tpp, d), F32), jax.ShapeDtypeStruct((bsz * tpp, d), BF16)),
        grid=(bsz, blocks),
        in_specs=[pl.BlockSpec((None, CHUNK, d), lambda bb, j: (bb, jnp.clip(j - 1, 0, nb - 1), 0)),
                  pl.BlockSpec((n_meta, d), lambda bb, j: (0, 0)), vec, vec],
        out_specs=(spec, spec),
        compiler_params=_cparams(("parallel", "parallel")), name="embed_ln",
    )(x, meta_tokens.astype(x.dtype), g.reshape(1, d), b.reshape(1, d))


def _final_ln_kernel(h_ref, y0_ref, y1_ref, rt_ref, g_ref, b_ref, o_ref, *, alpha):
    x = alpha * h_ref[...] + rt_ref[:, 2:3] * y0_ref[...] + rt_ref[:, 3:4] * y1_ref[...]
    o_ref[...] = _layer_norm_rows(x, g_ref, b_ref)


def _final_ln(h, y2, route, g, b, alpha, lay, bsz, seq):
    front, t_valid, tpp = lay
    n, d = h.shape
    blocks = tpp // CHUNK
    first = (front + t_valid - seq) // CHUNK
    row = lambda bb, j: bb * blocks + first + j
    return pl.pallas_call(
        functools.partial(_final_ln_kernel, alpha=alpha),
        out_shape=jax.ShapeDtypeStruct((bsz, seq, d), F32),
        grid=(bsz, seq // CHUNK),
        in_specs=[pl.BlockSpec((CHUNK, d), lambda bb, j: (row(bb, j), 0)),
                  pl.BlockSpec((CHUNK, d), lambda bb, j: (row(bb, j), 0)),
                  pl.BlockSpec((CHUNK, d), lambda bb, j: (row(bb, j) + n // CHUNK, 0)),
                  pl.BlockSpec((CHUNK, LANE), lambda bb, j: (row(bb, j), 0)),
                  pl.BlockSpec((1, d), lambda bb, j: (0, 0)), pl.BlockSpec((1, d), lambda bb, j: (0, 0))],
        out_specs=pl.BlockSpec((None, CHUNK, d), lambda bb, j: (bb, j, 0)),
        compiler_params=_cparams(("parallel", "parallel")), name="final_ln",
    )(h, y2, y2, route, g.reshape(1, d), b.reshape(1, d))


def _out_proj_kernel(*refs, ks):
    a_refs = refs[:len(ks)]
    w_ref, o_ref, wb_ref = refs[len(ks):]

    @pl.when(pl.program_id(1) == 0)
    def _():
        step = min(w_ref.shape[0], 512)
        for r0 in range(0, w_ref.shape[0], step):
            wb_ref[r0:r0 + step, :] = w_ref[r0:r0 + step, :].astype(BF16)

    acc = None
    off = 0
    for a_ref, k in zip(a_refs, ks):
        p = _dot(a_ref[...], wb_ref[off:off + k, :])
        acc = p if acc is None else acc + p
        off += k
    o_ref[...] = acc


def _out_proj(a_list, w_all, layer):
    n = a_list[0].shape[0]
    ks = tuple(a.shape[1] for a in a_list)
    _, kdim, ndim = w_all.shape
    assert sum(ks) == kdim
    tm = _divisor_tile(n, 768, 16)
    tn = _divisor_tile(ndim, 512, LANE)
    in_specs = [pl.BlockSpec((tm, k), lambda j, i: (i, 0)) for k in ks]
    in_specs.append(pl.BlockSpec((None, kdim, tn), lambda j, i: (layer, 0, j)))
    return pl.pallas_call(
        functools.partial(_out_proj_kernel, ks=ks),
        out_shape=jax.ShapeDtypeStruct((n, ndim), F32),
        grid=(ndim // tn, n // tm),
        in_specs=in_specs,
        out_specs=pl.BlockSpec((tm, tn), lambda j, i: (i, j)),
        scratch_shapes=[pltpu.VMEM((kdim, tn), BF16)],
        compiler_params=_cparams(("parallel", "arbitrary")),
        name="out_proj",
    )(*a_list, w_all)


def _in_proj_kernel(a_ref, w_ref, wn_ref, o_ref, wb_ref, *, n_plain, shift):
    j = pl.program_id(0)
    tn = w_ref.shape[0]
    step = min(tn, LANE)

    @pl.when(pl.program_id(1) == 0)
    def _():
        @pl.when(j < n_plain)
        def _():
            for r0 in range(0, tn, step):
                wb_ref[r0:r0 + step, :] = w_ref[r0:r0 + step, :].astype(BF16)

        @pl.when(j >= n_plain)
        def _():
            for r0 in range(0, tn - step, step):
                wb_ref[r0:r0 + step, :] = w_ref[r0 + shift:r0 + shift + step, :].astype(BF16)
            wb_ref[tn - step:tn - shift, :] = w_ref[tn - step + shift:, :].astype(BF16)
            wb_ref[tn - shift:, :] = wn_ref[:shift, :].astype(BF16)

    o_ref[...] = _dot_nt(a_ref[...], wb_ref[...])


def _small_proj_kernel(a_ref, w_ref, o_ref):
    o_ref[...] = _dot_nt(a_ref[...], w_ref[...].astype(BF16))


def _in_proj(hb, w_in_t, layer, w):
    n, d = hb.shape
    n_small = w_in_t.shape[1] - 15 * w
    tm = _divisor_tile(n, 1408, 16)
    tn = _divisor_tile(w, 512, LANE)
    z = pl.pallas_call(
        functools.partial(_in_proj_kernel, n_plain=4 * w // tn, shift=n_small),
        out_shape=jax.ShapeDtypeStruct((n, 15 * w), F32),
        grid=(15 * w // tn, n // tm),
        in_specs=[pl.BlockSpec((tm, d), lambda j, i: (i, 0)),
                  pl.BlockSpec((None, tn, d), lambda j, i: (layer, j, 0)),
                  pl.BlockSpec((None, LANE, d), lambda j, i: (layer, (j + 1) * (tn // LANE), 0))],
        out_specs=pl.BlockSpec((tm, tn), lambda j, i: (i, j)),
        scratch_shapes=[pltpu.VMEM((tn, d), BF16)],
        compiler_params=_cparams(("parallel", "arbitrary")),
        name="in_proj",
    )(hb, w_in_t, w_in_t)
    zs = pl.pallas_call(
        _small_proj_kernel,
        out_shape=jax.ShapeDtypeStruct((n, LANE), F32),
        grid=(n // tm,),
        in_specs=[pl.BlockSpec((tm, d), lambda i: (i, 0)),
                  pl.BlockSpec((None, LANE, d), lambda i: (layer, 4 * w // LANE, 0))],
        out_specs=pl.BlockSpec((tm, LANE), lambda i: (i, 0)),
        compiler_params=_cparams(("parallel",)),
        name="small_proj",
    )(hb, w_in_t)
    return z, zs


def _tri_masks(n):
    r = lax.broadcasted_iota(jnp.int32, (n, n), 0)
    c = lax.broadcasted_iota(jnp.int32, (n, n), 1)
    return r, c


def _gated_rms(o, w_row, gate):
    o = o * lax.rsqrt(jnp.mean(o * o, axis=-1, keepdims=True) + NORM_EPS)
    if w_row is not None:
        o = o * w_row
    return o * _silu(gate)


def _prefix_rows(x, row, seg):
    pos = row & (seg - 1)
    s = 1
    while s < seg:
        x = x + jnp.where(pos >= s, pltpu.roll(x, s, 0), 0.0)
        s *= 2
    return x


def _mixer_a_kernel(q_ref, k_ref, v_ref, gt_ref, sm_ref, cw_ref, alog_ref, dtb_ref, nw_ref, o_ref,
                    s_ref, prev_ref, *, n_heads, front, t_valid):
    c = pl.program_id(1)

    @pl.when(c == 0)
    def _():
        s_ref[...] = jnp.zeros_like(s_ref)
        prev_ref[...] = jnp.zeros_like(prev_ref)

    row = lax.broadcasted_iota(jnp.int32, (CHUNK, 1), 0)
    row8 = lax.broadcasted_iota(jnp.int32, (SUBLANES, 1), 0)
    valid = _valid_rows(c * CHUNK, CHUNK, front, t_valid).astype(F32)

    def conv_silu(x_ref, p):
        cur = x_ref[...]
        prev8 = prev_ref[p]
        acc = cur * cw_ref[p, 3:4, :]
        for s in (1, 2, 3):
            rolled = pltpu.roll(cur, s, 0)
            top = jnp.where(row8 >= s, rolled[:SUBLANES], pltpu.roll(prev8, s, 0))
            acc = acc + jnp.concatenate([top, rolled[SUBLANES:]], axis=0) * cw_ref[p, 3 - s:4 - s, :]
        prev_ref[p] = cur[CHUNK - SUBLANES:]
        return _silu(acc)

    q_all = conv_silu(q_ref, 0)
    k_all = conv_silu(k_ref, 1) * valid
    v_all = conv_silu(v_ref, 2) * valid

    sm = sm_ref[...]
    beta_all = _sigmoid(sm) * valid
    g_all = -jnp.exp(alog_ref[...]) * _softplus(sm + dtb_ref[...]) * valid
    gcum_all = _prefix_rows(g_all, row, CHUNK)
    r, cc = _tri_masks(CHUNK)
    causal = r >= cc
    strict = r > cc
    eye = (r == cc).astype(F32)
    gcum_t = lax.dot_general(gcum_all, eye, (((0,), (0,)), ((), ())), preferred_element_type=F32,
                             precision=HI)

    heads = range(n_heads)
    hsl = [slice(h * HEAD_DIM, (h + 1) * HEAD_DIM) for h in heads]
    beta = [beta_all[:, h:h + 1] for h in heads]
    gcol = [gcum_all[:, n_heads + h:n_heads + h + 1] for h in heads]
    decay = [jnp.where(causal, jnp.exp(jnp.minimum(gcol[h] - gcum_t[n_heads + h:n_heads + h + 1, :], 0.0)), 0.0)
             for h in heads]
    eg = [jnp.exp(g) for g in gcol]
    g_last = [g[CHUNK - 1:CHUNK, :] for g in gcol]
    q = [q_all[:, s] * lax.rsqrt(jnp.sum(q_all[:, s] * q_all[:, s], axis=-1, keepdims=True) + NORM_EPS)
         * HEAD_DIM ** -0.5 for s in hsl]
    k = [k_all[:, s] * lax.rsqrt(jnp.sum(k_all[:, s] * k_all[:, s], axis=-1, keepdims=True) + NORM_EPS) for s in hsl]
    kb = [k[h] * beta[h] for h in heads]
    k16 = [x.astype(BF16) for x in k]
    m = [jnp.where(strict, -(_dot_nt(kb[h].astype(BF16), k16[h]) * decay[h]), 0.0) for h in heads]
    attn = [_dot_nt(q[h].astype(BF16), k16[h]) * decay[h] for h in heads]
    inv = [eye + x for x in m]
    mp = [_dot(x.astype(BF16), x.astype(BF16)) for x in m]
    for _ in range(4):
        both = [_dot(jnp.concatenate([inv[h], mp[h]], axis=0).astype(BF16), mp[h].astype(BF16)) for h in heads]
        inv = [inv[h] + both[h][:CHUNK] for h in heads]
        mp = [both[h][CHUNK:] for h in heads]
    inv = [inv[h] + _dot(inv[h].astype(BF16), mp[h].astype(BF16)) for h in heads]
    sol = [_dot(inv[h].astype(BF16),
                jnp.concatenate([v_all[:, hsl[h]] * beta[h], kb[h] * eg[h]], axis=1).astype(BF16)) for h in heads]
    s_old = [s_ref[h] for h in heads]
    s16 = [x.astype(BF16) for x in s_old]
    v_new = [sol[h][:, :HEAD_DIM] - _dot(sol[h][:, HEAD_DIM:].astype(BF16), s16[h]) for h in heads]
    o = [_dot((q[h] * eg[h]).astype(BF16), s16[h]) + _dot(attn[h].astype(BF16), v_new[h].astype(BF16)) for h in heads]
    for h in heads:
        s_ref[h] = s_old[h] * jnp.exp(g_last[h]) + _dot_tn(k[h] * jnp.exp(g_last[h] - gcol[h]), v_new[h])
        o_ref[:, hsl[h]] = _gated_rms(o[h], nw_ref[...], gt_ref[:, hsl[h]]).astype(o_ref.dtype)


def _mixer_b_kernel(q_ref, f_ref, i_ref, gt_ref, lb_ref, nw_ref, o_ref, st_ref, kbuf, bbuf, obuf,
                    *, n_heads, front, t_valid):
    c = pl.program_id(1)

    @pl.when(c == 0)
    def _():
        st_ref[...] = jnp.zeros_like(st_ref)

    row = lax.broadcasted_iota(jnp.int32, (CHUNK, 1), 0)
    valid = _valid_rows(c * CHUNK, CHUNK, front, t_valid).astype(F32)
    row16 = lax.broadcasted_iota(jnp.int32, (SUB, 1), 0)

    lb = lb_ref[...]
    f = lb + (1.0 - lb) * _sigmoid(f_ref[...])
    kbuf[...] = (1.0 - f) * valid
    bbuf[...] = _prefix_rows(jnp.log(f) * valid, row, SUB)

    heads = range(n_heads)
    hsl = [slice(h * HEAD_DIM, (h + 1) * HEAD_DIM) for h in heads]
    st = [st_ref[h] for h in heads]
    for blk in range(CHUNK // SUB):
        rs = slice(blk * SUB, (blk + 1) * SUB)
        for h in heads:
            hs = hsl[h]
            b16 = bbuf[rs, hs]
            q16 = _silu(q_ref[rs, hs])
            b_end = b16[SUB - 1:SUB, :]
            acc = _dot_nt((q16 * jnp.exp(b16)).astype(BF16), st[h].astype(BF16))
            parts = []
            for t0 in range(0, SUB, SUBLANES):
                ts = slice(t0, t0 + SUBLANES)
                bq, qq, rowq = b16[ts], q16[ts], row16[ts]
                part = acc[ts]
                for j in range(min(t0 + SUBLANES, SUB)):
                    jr = pl.ds(blk * SUB + j, 1)
                    sc = jnp.sum(qq * kbuf[jr, hs] * jnp.exp(bq - bbuf[jr, hs]), axis=-1, keepdims=True)
                    if j > t0:
                        sc = jnp.where(rowq >= j, sc, 0.0)
                    part = part + sc * i_ref[jr, hs]
                parts.append(part)
            acc = jnp.concatenate(parts, axis=0)
            st[h] = st[h] * jnp.exp(b_end) + _dot_tn(i_ref[rs, hs], kbuf[rs, hs] * jnp.exp(b_end - b16))
            obuf[rs, hs] = acc
    for h in heads:
        st_ref[h] = st[h]
        o_ref[:, hsl[h]] = _gated_rms(obuf[:, hsl[h]], nw_ref[...], gt_ref[:, hsl[h]]).astype(o_ref.dtype)


def _rope_partial(x, cos, sin_lo, sin_hi):
    half = ROPE_DIMS // 2
    return x * cos + pltpu.roll(x, LANE - half, 1) * sin_lo + pltpu.roll(x, half, 1) * sin_hi


def _rope_c_kernel(q_ref, k_ref, v_ref, cos_ref, slo_ref, shi_ref, qo_ref, ko_ref, vo_ref, *, n_maps):
    cos, slo, shi = cos_ref[...], slo_ref[...], shi_ref[...]
    for mp in range(n_maps):
        ms = slice(mp * HEAD_DIM, (mp + 1) * HEAD_DIM)
        qo_ref[:, ms] = (_rope_partial(q_ref[:, ms], cos, slo, shi) * HEAD_DIM ** -0.5).astype(BF16)
        ko_ref[:, ms] = _rope_partial(k_ref[:, ms], cos, slo, shi).astype(BF16)
    vo_ref[...] = v_ref[...].astype(BF16)


def _mixer_c_kernel(qi_ref, kj_ref, q_ref, k_ref, v_ref, lam_ref, nw_ref, o_ref, m_ref, l_ref, acc_ref,
                    *, tq, front, out_scale):
    p = pl.program_id(2)
    i = qi_ref[p]
    j = kj_ref[p]

    @pl.when(j == 0)
    def _():
        m_ref[...] = jnp.full_like(m_ref, NEG)
        l_ref[...] = jnp.zeros_like(l_ref)
        acc_ref[...] = jnp.zeros_like(acc_ref)

    def step(masked):
        v = v_ref[...]
        if masked:
            rq = i * tq + lax.broadcasted_iota(jnp.int32, (tq, 1), 0)
            rk = j * tq + lax.broadcasted_iota(jnp.int32, (1, tq), 1)
            msk = (rk <= rq) & (rk >= front)
        maps = range(2)
        msl = [slice(mp * HEAD_DIM, (mp + 1) * HEAD_DIM) for mp in maps]
        s = [_dot_nt(q_ref[:, ms], k_ref[:, ms]) for ms in msl]
        if masked:
            s = [jnp.where(msk, x, NEG) for x in s]
        m_prev = [m_ref[mp] for mp in maps]
        m_new = [jnp.maximum(m_prev[mp], jnp.max(s[mp], axis=-1, keepdims=True)) for mp in maps]
        pr = [jnp.exp(s[mp] - jnp.tile(m_new[mp], (1, tq // LANE))) for mp in maps]
        pv = [_dot(pr[mp].astype(BF16), v) for mp in maps]
        for mp in maps:
            a = jnp.exp(m_prev[mp] - m_new[mp])
            l_ref[mp] = a * l_ref[mp] + jnp.sum(pr[mp], axis=-1, keepdims=True)
            acc_ref[mp] = jnp.tile(a, (1, 2 * HEAD_DIM // LANE)) * acc_ref[mp] + pv[mp]
            m_ref[mp] = m_new[mp]

    needs_mask = (j == i) | (j == 0)

    @pl.when(needs_mask)
    def _():
        step(True)

    @pl.when(jnp.logical_not(needs_mask))
    def _():
        step(False)

    @pl.when(j == i)
    def _():
        reps = (1, 2 * HEAD_DIM // LANE)
        o = acc_ref[0] / jnp.tile(l_ref[0], reps) - lam_ref[...] * (acc_ref[1] / jnp.tile(l_ref[1], reps))
        o = o * lax.rsqrt(jnp.mean(o * o, axis=-1, keepdims=True) + NORM_EPS) * nw_ref[...] * out_scale
        o_ref[...] = o.astype(o_ref.dtype)


def _mixer_d_kernel(q_ref, k_ref, v_ref, gt_ref, cos_ref, sin_ref, dm_ref, qd_ref, kd_ref, cd_ref, o_ref, s_ref,
                    *, n_heads):
    c = pl.program_id(1)

    @pl.when(c == 0)
    def _():
        s_ref[...] = jnp.zeros_like(s_ref)

    cos = cos_ref[...]
    sin = sin_ref[...]
    heads = range(n_heads)
    hsl = [slice(h * HEAD_DIM, (h + 1) * HEAD_DIM) for h in heads]
    q = [q_ref[:, s] * cos + pltpu.roll(q_ref[:, s], HEAD_DIM // 2, 1) * sin for s in hsl]
    k = [(k_ref[:, s] * cos + pltpu.roll(k_ref[:, s], HEAD_DIM // 2, 1) * sin) * HEAD_DIM ** -0.5 for s in hsl]
    s_old = [s_ref[h] for h in heads]
    a = [_dot_nt(q[h].astype(BF16), k[h].astype(BF16)) * dm_ref[h] for h in heads]
    o = [_dot(a[h].astype(BF16), v_ref[:, hsl[h]].astype(BF16))
         + _dot((q[h] * qd_ref[h]).astype(BF16), s_old[h].astype(BF16)) for h in heads]
    for h in heads:
        s_ref[h] = cd_ref[h] * s_old[h] + _dot_tn(k[h] * kd_ref[h], v_ref[:, hsl[h]])
        o_ref[:, hsl[h]] = _gated_rms(o[h], None, gt_ref[:, hsl[h]]).astype(o_ref.dtype)


def _route(lg):
    n_e = N_GROUPS * EXP_PER_GROUP
    lane_i = lax.broadcasted_iota(jnp.int32, lg.shape, 1)
    lane = lane_i.astype(F32)
    grp = (lane_i >> GROUP_SHIFT).astype(F32)
    big = float(4 * LANE)
    is_g = (lane_i >= n_e) & (lane_i < n_e + N_GROUPS)
    gl = jnp.where(is_g, lg, NEG)
    gmax = jnp.max(gl, axis=-1, keepdims=True)
    gsum = jnp.sum(jnp.where(is_g, jnp.exp(gl - gmax), 0.0), axis=-1, keepdims=True)
    g_w = 1.0 / gsum
    g_idx = jnp.min(jnp.where(is_g & (gl == gmax), lane - n_e, big), axis=-1, keepdims=True)
    in_grp = (lane_i < n_e) & (grp == g_idx)
    el = jnp.where(in_grp, lg, NEG)
    emax = jnp.max(el, axis=-1, keepdims=True)
    eexp = jnp.where(in_grp, jnp.exp(el - emax), 0.0)
    pe = eexp / jnp.sum(eexp, axis=-1, keepdims=True)
    p1 = jnp.max(jnp.where(in_grp, pe, -1.0), axis=-1, keepdims=True)
    i1 = jnp.min(jnp.where(in_grp & (pe == p1), lane, big), axis=-1, keepdims=True)
    rest = in_grp & (lane != i1)
    p2 = jnp.max(jnp.where(rest, pe, -1.0), axis=-1, keepdims=True)
    i2 = jnp.min(jnp.where(rest & (pe == p2), lane, big), axis=-1, keepdims=True)
    den = p1 + p2
    return (jnp.where(lane_i == 0, i1, 0.0) + jnp.where(lane_i == 1, i2, 0.0)
            + jnp.where(lane_i == 2, g_w * (p1 / den), 0.0) + jnp.where(lane_i == 3, g_w * (p2 / den), 0.0))


def _row_copy(src, src_row, dst, dst_row, sem):
    return pltpu.make_async_copy(src.at[pl.ds(src_row, 1)], dst.at[pl.ds(dst_row, 1)], sem)


def _moe_kernel(te_ref, cnt_ref, first_ref, src_ref, srcn_ref, dst_ref, x_hbm, wg_ref, wu_ref, wd_ref, y_hbm,
                xbuf, ybuf, wgb, wub, wdb, sem_in, sem_out, *, tm, n_tiles):
    del te_ref
    t = pl.program_id(0)
    slot = t % 2
    cnt = cnt_ref[t]

    def pairs(rows):
        return (rows + 1) // 2

    def gather_start(idx_ref, s, rows):
        def body(g, c):
            for u in range(2):
                r = 2 * g + u
                _row_copy(x_hbm, idx_ref[0, 0, r], xbuf.at[s], r, sem_in.at[s]).start(priority=u)
            return c
        lax.fori_loop(0, pairs(rows), body, 0)

    def wait_rows(src, dst, sem, rows):
        def tile_body(g, c):
            pltpu.make_async_copy(src.at[pl.ds(0, SUBLANES)], dst.at[pl.ds(g * SUBLANES, SUBLANES)], sem).wait()
            return c
        lax.fori_loop(0, rows // SUBLANES, tile_body, 0)

        def row_body(r, c):
            _row_copy(src, 0, dst, r, sem).wait()
            return c
        lax.fori_loop(rows - rows % SUBLANES, rows, row_body, 0)

    def gather_wait(s, rows):
        wait_rows(x_hbm, xbuf.at[s], sem_in.at[s], 2 * pairs(rows))

    def scatter_start(s, rows):
        def body(g, c):
            for u in range(2):
                r = 2 * g + u
                _row_copy(ybuf.at[s], r, y_hbm, dst_ref[0, 0, r], sem_out.at[s]).start(priority=u)
            return c
        lax.fori_loop(0, rows // 2, body, 0)

        @pl.when(rows % 2 == 1)
        def _():
            _row_copy(ybuf.at[s], rows - 1, y_hbm, dst_ref[0, 0, rows - 1], sem_out.at[s]).start()

    def scatter_wait(s, rows):
        wait_rows(ybuf.at[s], y_hbm, sem_out.at[s], rows)

    @pl.when(t == 0)
    def _():
        xbuf[...] = jnp.zeros_like(xbuf)

        @pl.when(cnt > 0)
        def _():
            gather_start(src_ref, 0, cnt)

    nxt = jnp.minimum(t + 1, n_tiles - 1)

    @pl.when((t + 1 < n_tiles) & (cnt_ref[nxt] > 0))
    def _():
        gather_start(srcn_ref, 1 - slot, cnt_ref[nxt])

    old = jnp.maximum(t - 2, 0)

    @pl.when((t >= 2) & (cnt_ref[old] > 0))
    def _():
        scatter_wait(slot, cnt_ref[old])

    @pl.when(cnt > 0)
    def _():
        @pl.when(first_ref[t] == 1)
        def _():
            wgb[...] = wg_ref[...].astype(BF16)
            wub[...] = wu_ref[...].astype(BF16)
            wdb[...] = wd_ref[...].astype(BF16)

        gather_wait(slot, cnt)
        x = xbuf[slot].astype(BF16)
        hg = _dot(x, wgb[...])
        hu = _dot(x, wub[...])
        act = (_silu(hg) * hu).astype(BF16)
        ybuf[slot] = _dot(act, wdb[...])
        scatter_start(slot, cnt)

    @pl.when(t == n_tiles - 1)
    def _():
        prev = jnp.maximum(t - 1, 0)

        @pl.when((t >= 1) & (cnt_ref[prev] > 0))
        def _():
            scatter_wait(1 - slot, cnt_ref[prev])

        @pl.when(cnt > 0)
        def _():
            scatter_wait(slot, cnt)


def _dispatch(route, n, tm, n_e):
    ids = jnp.concatenate([route[:, 0], route[:, 1]]).astype(jnp.int32)
    n_pairs = 2 * n
    n_tiles = n_pairs // tm + n_e
    order = jnp.argsort(ids, stable=True).astype(jnp.int32)
    counts = jnp.sum((ids[:, None] == jnp.arange(n_e, dtype=jnp.int32)[None, :]).astype(jnp.int32), axis=0)
    tiles_e = (counts + tm - 1) // tm
    tile_end = jnp.cumsum(tiles_e)
    tile_start = tile_end - tiles_e
    first = jnp.cumsum(counts) - counts
    tile = jnp.arange(n_tiles, dtype=jnp.int32)
    te = jnp.minimum(jnp.searchsorted(tile_end, tile, side="right").astype(jnp.int32), n_e - 1)
    cnt = jnp.clip(counts[te] - (tile - tile_start[te]) * tm, 0, tm)
    cnt = jnp.where(tile < tile_end[n_e - 1], cnt, 0).astype(jnp.int32)
    row = jnp.arange(tm, dtype=jnp.int32)[None, :]
    spos = (first[te] + (tile - tile_start[te]) * tm)[:, None] + row
    dst = jnp.where(row < cnt[:, None], order[jnp.clip(spos, 0, n_pairs - 1)], 0)
    src = dst % n
    te = jnp.where(cnt > 0, te, jnp.max(jnp.where(cnt > 0, te, 0)))
    new_expert = jnp.concatenate([jnp.ones((1,), jnp.int32), (te[1:] != te[:-1]).astype(jnp.int32)])
    return te, cnt, new_expert, src.reshape(n_tiles, 1, tm), dst.reshape(n_tiles, 1, tm)


def _moe_routed(h, route, wg, wu, wd, layer, tm):
    n, d = h.shape
    _, n_e, _, f = wg.shape
    te, cnt, new_expert, src, dst = _dispatch(route, n, tm, n_e)
    n_tiles = te.shape[0]

    def idx_spec(shift):
        return pl.BlockSpec((1, 1, tm), lambda t, *_: (jnp.minimum(t + shift, n_tiles - 1), 0, 0),
                            memory_space=pltpu.SMEM)

    grid_spec = pltpu.PrefetchScalarGridSpec(
        num_scalar_prefetch=3, grid=(n_tiles,),
        in_specs=[idx_spec(0), idx_spec(1), idx_spec(0),
                  pl.BlockSpec(memory_space=pl.ANY),
                  pl.BlockSpec((None, None, d, f), lambda t, te, *_: (layer, te[t], 0, 0)),
                  pl.BlockSpec((None, None, d, f), lambda t, te, *_: (layer, te[t], 0, 0)),
                  pl.BlockSpec((None, None, f, d), lambda t, te, *_: (layer, te[t], 0, 0))],
        out_specs=pl.BlockSpec(memory_space=pl.ANY),
        scratch_shapes=[pltpu.VMEM((2, tm, d), F32), pltpu.VMEM((2, tm, d), F32),
                        pltpu.VMEM((d, f), BF16), pltpu.VMEM((d, f), BF16), pltpu.VMEM((f, d), BF16),
                        pltpu.SemaphoreType.DMA((2,)), pltpu.SemaphoreType.DMA((2,))])
    return pl.pallas_call(
        functools.partial(_moe_kernel, tm=tm, n_tiles=n_tiles),
        out_shape=jax.ShapeDtypeStruct((2 * n, d), F32),
        grid_spec=grid_spec,
        compiler_params=_cparams(("arbitrary",)),
        name="moe_routed",
    )(te, cnt, new_expert, src, src, dst, h, wg, wu, wd)


def _mixers(z, zs, lay, bsz, w, prm):
    front, t_valid, tpp = lay
    n = z.shape[0]
    nc = tpp // CHUNK
    heads = w // HEAD_DIM
    c_heads = w // (2 * HEAD_DIM)

    def cspec(region):
        return pl.BlockSpec((CHUNK, w), lambda b, c, region=region: (b * nc + c, region))

    def fixed(shape):
        nd = len(shape)
        return pl.BlockSpec(shape, lambda b, c: (0,) * nd)

    ospec = pl.BlockSpec((CHUNK, w), lambda b, c: (b * nc + c, 0))
    oshape = jax.ShapeDtypeStruct((n, w), BF16)
    grid = (bsz, nc)
    sem = ("parallel", "arbitrary")
    state = pltpu.VMEM((heads, HEAD_DIM, HEAD_DIM), F32)

    oa = pl.pallas_call(
        functools.partial(_mixer_a_kernel, n_heads=heads, front=front, t_valid=t_valid),
        out_shape=oshape, grid=grid,
        in_specs=[cspec(0), cspec(1), cspec(2), cspec(3),
                  pl.BlockSpec((CHUNK, LANE), lambda b, c: (b * nc + c, 0)),
                  fixed((3, 4, w)), fixed((1, LANE)), fixed((1, LANE)), fixed((1, HEAD_DIM))],
        out_specs=ospec,
        scratch_shapes=[state, pltpu.VMEM((3, SUBLANES, w), F32)],
        compiler_params=_cparams(sem), name="mixer_a",
    )(z, z, z, z, zs, prm["conv_w"], prm["a_log_row"], prm["dt_bias_row"], prm["norm_a"])

    ob = pl.pallas_call(
        functools.partial(_mixer_b_kernel, n_heads=heads, front=front, t_valid=t_valid),
        out_shape=oshape, grid=grid,
        in_specs=[cspec(4), cspec(5), cspec(6), cspec(7), fixed((1, w)), fixed((1, HEAD_DIM))],
        out_specs=ospec,
        scratch_shapes=[state] + [pltpu.VMEM((CHUNK, w), F32)] * 3,
        compiler_params=_cparams(sem), name="mixer_b",
    )(z, z, z, z, prm["lb"], prm["norm_b"])

    tq = _divisor_tile(tpp, 384, LANE)
    nq = tpp // tq
    vw = 2 * HEAD_DIM
    rc = prm["rope_c"]
    rspec = pl.BlockSpec((tq, w), lambda i: (i, 0))
    tab = pl.BlockSpec((tq, HEAD_DIM), lambda i: (i % nq, 0))
    qr, kr, vr = pl.pallas_call(
        functools.partial(_rope_c_kernel, n_maps=heads),
        out_shape=(oshape, oshape, oshape), grid=(n // tq,),
        in_specs=[pl.BlockSpec((tq, w), lambda i: (i, 8)), pl.BlockSpec((tq, w), lambda i: (i, 9)),
                  pl.BlockSpec((tq, w), lambda i: (i, 10)), tab, tab, tab],
        out_specs=(rspec, rspec, rspec),
        compiler_params=_cparams(("parallel",)), name="rope_c",
    )(z, z, z, rc[0], rc[1], rc[2])

    pairs = [(i, j) for i in range(nq) for j in range(i + 1)]
    qi = jnp.asarray([p[0] for p in pairs], jnp.int32)
    kj = jnp.asarray([p[1] for p in pairs], jnp.int32)
    qspec = pl.BlockSpec((tq, vw), lambda b, h, p, qi, kj: (b * nq + qi[p], h))
    kspec = pl.BlockSpec((tq, vw), lambda b, h, p, qi, kj: (b * nq + kj[p], h))
    row_vw = pl.BlockSpec((1, vw), lambda b, h, p, qi, kj: (0, 0))
    oc = pl.pallas_call(
        functools.partial(_mixer_c_kernel, tq=tq, front=front, out_scale=prm["c_out_scale"]),
        out_shape=oshape,
        grid_spec=pltpu.PrefetchScalarGridSpec(
            num_scalar_prefetch=2, grid=(bsz, c_heads, len(pairs)),
            in_specs=[qspec, kspec, kspec, row_vw, row_vw],
            out_specs=qspec,
            scratch_shapes=[pltpu.VMEM((2, tq, LANE), F32), pltpu.VMEM((2, tq, LANE), F32),
                            pltpu.VMEM((2, tq, vw), F32)]),
        compiler_params=_cparams(("parallel", "parallel", "arbitrary")), name="mixer_c",
    )(qi, kj, qr, kr, vr, prm["lam_row"], prm["subln_c"])

    tabd = pl.BlockSpec((CHUNK, HEAD_DIM), lambda b, c: (c, 0))
    rd = prm["rope_d"]
    od = pl.pallas_call(
        functools.partial(_mixer_d_kernel, n_heads=heads),
        out_shape=oshape, grid=grid,
        in_specs=[cspec(11), cspec(12), cspec(13), cspec(14), tabd, tabd,
                  fixed((heads, CHUNK, CHUNK)), fixed((heads, CHUNK, HEAD_DIM)), fixed((heads, CHUNK, HEAD_DIM)),
                  fixed((heads, 1, HEAD_DIM))],
        out_specs=ospec,
        scratch_shapes=[state],
        compiler_params=_cparams(sem), name="mixer_d",
    )(z, z, z, z, rd[0], rd[1], prm["ret_dmask"], prm["ret_qdec"], prm["ret_kdec"], prm["ret_cdec"])
    return oa, ob, oc, od


def _rope_tables(front, tpp):
    pos = (jnp.arange(tpp) - front).astype(F32)[:, None]
    half = ROPE_DIMS // 2
    inv = 1.0 / (ROPE_THETA ** (jnp.arange(half, dtype=F32) / half))
    ang = pos * inv[None, :]
    z_rest = jnp.zeros((tpp, HEAD_DIM - ROPE_DIMS), F32)
    z_half = jnp.zeros((tpp, half), F32)
    cos_c = jnp.concatenate([jnp.cos(ang), jnp.cos(ang), jnp.ones_like(z_rest)], axis=1)
    sin_lo = jnp.concatenate([-jnp.sin(ang), z_half, z_rest], axis=1)
    sin_hi = jnp.concatenate([z_half, jnp.sin(ang), z_rest], axis=1)
    half_d = HEAD_DIM // 2
    inv_d = 1.0 / (RET_THETA ** (jnp.arange(half_d, dtype=F32) / half_d))
    ang_d = pos * inv_d[None, :]
    cos_d = jnp.concatenate([jnp.cos(ang_d), jnp.cos(ang_d)], axis=1)
    sin_d = jnp.concatenate([-jnp.sin(ang_d), jnp.sin(ang_d)], axis=1)
    return (cos_c, sin_lo, sin_hi), (cos_d, sin_d)


def _retention_tables(n_heads):
    lg = jnp.log(1.0 - 2.0 ** (-5.0 - jnp.arange(n_heads, dtype=F32)))
    idx = jnp.arange(CHUNK, dtype=F32)
    rel = idx[:, None] - idx[None, :]
    causal = idx[:, None] >= idx[None, :]
    dmask = jnp.exp(jnp.where(causal[None], rel[None] * lg[:, None, None], -jnp.inf))
    qdec = jnp.exp((idx[None, :] + 1.0) * lg[:, None])
    kdec = jnp.exp((CHUNK - 1.0 - idx[None, :]) * lg[:, None])
    cdec = jnp.exp(CHUNK * lg)
    bc = lambda t: jnp.broadcast_to(t[..., None], t.shape + (HEAD_DIM,))
    return dmask, bc(qdec), bc(kdec), bc(cdec[:, None])


def _row(v, width=LANE, offset=0):
    out = jnp.zeros((1, width), F32)
    return out.at[0, offset:offset + v.shape[0]].set(v.astype(F32))


def kernel(x, meta_tokens, emb_ln_g, emb_ln_b, w_in, conv_a, a_log, dt_bias, norm_a, hgrn_lb, norm_b, lam_q1, lam_k1,
           lam_q2, lam_k2, subln_c, w_out, ln1_g, ln1_b, w_rg, b_rg, w_re, b_re, w_gate, w_up, w_down, ln2_g, ln2_b):
    bsz, seq, d = x.shape
    depth = w_in.shape[0]
    n_meta = meta_tokens.shape[0]
    w = d // 4
    a_heads = w // HEAD_DIM
    t_valid = n_meta + seq
    front = (-t_valid) % CHUNK
    tpp = -(-(front + t_valid) // ROW_ALIGN) * ROW_ALIGN
    lay = (front, t_valid, tpp)
    alpha = (2 * depth) ** 0.25

    h, hb = _embed_ln(x, meta_tokens, emb_ln_g, emb_ln_b, lay)

    lbs = jax.nn.softmax(hgrn_lb.astype(F32), axis=0)
    lbs = jnp.cumsum(lbs, axis=0) - lbs[0:1]
    rope_c, rope_d = _rope_tables(front, tpp)
    dmask, qdec, kdec, cdec = _retention_tables(a_heads)

    w_in_t = jnp.swapaxes(w_in, 1, 2)

    for l in range(depth):
        lam_init = 0.8 - 0.6 * math.exp(-0.3 * l)
        lam = (jnp.exp(jnp.sum(lam_q1[l].astype(F32) * lam_k1[l].astype(F32)))
               - jnp.exp(jnp.sum(lam_q2[l].astype(F32) * lam_k2[l].astype(F32))) + lam_init)
        prm = {
            "conv_w": conv_a[l].astype(F32).reshape(3, w, -1).transpose(0, 2, 1),
            "a_log_row": _row(a_log[l], offset=a_heads),
            "dt_bias_row": _row(dt_bias[l], offset=a_heads),
            "norm_a": norm_a[l].astype(F32).reshape(1, HEAD_DIM),
            "lb": lbs[l].reshape(1, w),
            "norm_b": norm_b[l].astype(F32).reshape(1, HEAD_DIM),
            "rope_c": rope_c, "rope_d": rope_d,
            "lam_row": jnp.full((1, 2 * HEAD_DIM), lam, F32),
            "subln_c": subln_c[l].astype(F32).reshape(1, 2 * HEAD_DIM),
            "c_out_scale": 1.0 - lam_init,
            "ret_dmask": dmask, "ret_qdec": qdec, "ret_kdec": kdec, "ret_cdec": cdec,
        }
        z, zs = _in_proj(hb, w_in_t, l, w)
        o_parts = _mixers(z, zs, lay, bsz, w, prm)
        y = _out_proj(list(o_parts), w_out, l)
        n_e = w_re.shape[2]
        w_r = jnp.concatenate([w_re[l], w_rg[l], jnp.zeros((d, LANE - n_e - N_GROUPS), F32)], axis=1).astype(F32)
        b_r = jnp.concatenate([b_re[l], b_rg[l], jnp.zeros((LANE - n_e - N_GROUPS,), F32)]).reshape(1, LANE)
        w_r_hi = w_r.astype(BF16)
        w_r_lo = (w_r - w_r_hi.astype(F32)).astype(BF16)
        h, hb, route = _add_ln(h, y, ln1_g[l], ln1_b[l], alpha, lay, router=(w_r_hi, w_r_lo, b_r))
        y2 = _moe_routed(h, route, w_gate, w_up, w_down, l, MOE_TILE)
        if l + 1 < depth:
            h, hb = _add_ln(h, y2, ln2_g[l], ln2_b[l], alpha, lay, route=route)

    return _final_ln(h, y2, route, ln2_g[depth - 1], ln2_b[depth - 1], alpha, lay, bsz, seq)
```

```python
import functools
import math

import jax
import jax.numpy as jnp
from jax import lax
from jax.experimental import pallas as pl
from jax.experimental.pallas import tpu as pltpu

HEAD_DIM = 128
CHUNK = 64
SUB = 16
GROUP_SHIFT = 3
ROPE_THETA = 500000.0
ROPE_DIMS = HEAD_DIM // 4
RET_THETA = 10000.0
N_GROUPS = 8
EXP_PER_GROUP = 8
NORM_EPS = 1e-6
LN_EPS = 1e-5
LANE = 128
SUBLANES = 8
ROW_ALIGN = 2 * CHUNK
V7X_VMEM_LIMIT_BYTES = 56 * 1024 * 1024
MOE_TILE = 256
NEG = -1e30

F32 = jnp.float32
BF16 = jnp.bfloat16
HI = lax.Precision.HIGHEST


def _cparams(sem):
    return pltpu.CompilerParams(dimension_semantics=sem, vmem_limit_bytes=V7X_VMEM_LIMIT_BYTES)


def _divisor_tile(n, cap, align):
    best = None
    for t in range(align, min(n, cap) + 1, align):
        if n % t == 0:
            best = t
    assert best is not None, (n, cap, align)
    return best


def _dot(a, b):
    return jnp.dot(a, b, preferred_element_type=F32)


def _dot_nt(a, b):
    return lax.dot_general(a, b, (((1,), (1,)), ((), ())), preferred_element_type=F32)


def _dot_tn(a, b):
    return lax.dot_general(a, b, (((0,), (0,)), ((), ())), preferred_element_type=F32)


def _sigmoid(x):
    return 1.0 / (1.0 + jnp.exp(-x))


def _silu(x):
    return x * _sigmoid(x)


def _softplus(x):
    return jnp.maximum(x, 0.0) + jnp.log(1.0 + jnp.exp(-jnp.abs(x)))


def _valid_rows(row0, n, front, t_valid):
    rb = row0 + lax.broadcasted_iota(jnp.int32, (n, 1), 0)
    return (rb >= front) & (rb < front + t_valid)


def _layer_norm_rows(x, g_ref, b_ref):
    mu = jnp.mean(x, axis=-1, keepdims=True)
    xc = x - mu
    var = jnp.mean(xc * xc, axis=-1, keepdims=True)
    return xc * lax.rsqrt(var + LN_EPS) * g_ref[...] + b_ref[...]


def _ln_kernel(*refs, alpha, mode, front, t_valid, tpp, tm):
    h_ref = refs[0]
    if mode == "experts":
        y0_ref, y1_ref, rt_ref, g_ref, b_ref, o_ref, ob_ref = refs[1:]
        x = alpha * h_ref[...] + rt_ref[:, 2:3] * y0_ref[...] + rt_ref[:, 3:4] * y1_ref[...]
    elif mode == "router":
        y_ref, g_ref, b_ref, wh_ref, wl_ref, br_ref, o_ref, ob_ref, ro_ref = refs[1:]
        x = alpha * h_ref[...] + y_ref[...].astype(F32)
    else:
        y_ref, g_ref, b_ref, o_ref, ob_ref = refs[1:]
        x = alpha * h_ref[...] + y_ref[...]
    y = _layer_norm_rows(x, g_ref, b_ref)
    row0 = (pl.program_id(0) * tm) % tpp
    y = jnp.where(_valid_rows(row0, tm, front, t_valid), y, 0.0)
    o_ref[...] = y
    yb = y.astype(BF16)
    ob_ref[...] = yb
    if mode == "router":
        yl = (y - yb.astype(F32)).astype(BF16)
        lg = _dot(yb, wh_ref[...]) + _dot(yl, wh_ref[...]) + _dot(yb, wl_ref[...]) + br_ref[...]
        ro_ref[...] = _route(lg)


def _add_ln(h, y, g, b, alpha, lay, route=None, router=None):
    front, t_valid, tpp = lay
    n, d = h.shape
    tm = _divisor_tile(tpp, 128, 16)
    spec = pl.BlockSpec((tm, d), lambda i: (i, 0))
    vec = pl.BlockSpec((1, d), lambda i: (0, 0))
    lane_rows = pl.BlockSpec((tm, LANE), lambda i: (i, 0))
    out_shape = [jax.ShapeDtypeStruct((n, d), F32), jax.ShapeDtypeStruct((n, d), BF16)]
    out_specs = [spec, spec]
    if route is not None:
        mode = "experts"
        args = (h, y, y, route, g.reshape(1, d), b.reshape(1, d))
        in_specs = [spec, spec, pl.BlockSpec((tm, d), lambda i: (i + n // tm, 0)), lane_rows, vec, vec]
    elif router is not None:
        mode = "router"
        wmat = pl.BlockSpec((d, LANE), lambda i: (0, 0))
        args = (h, y, g.reshape(1, d), b.reshape(1, d)) + tuple(router)
        in_specs = [spec, spec, vec, vec, wmat, wmat, pl.BlockSpec((1, LANE), lambda i: (0, 0))]
        out_shape.append(jax.ShapeDtypeStruct((n, LANE), F32))
        out_specs.append(lane_rows)
    else:
        mode = "plain"
        args = (h, y, g.reshape(1, d), b.reshape(1, d))
        in_specs = [spec, spec, vec, vec]
    kern = functools.partial(_ln_kernel, alpha=alpha, mode=mode, front=front, t_valid=t_valid, tpp=tpp, tm=tm)
    return pl.pallas_call(
        kern, out_shape=tuple(out_shape), grid=(n // tm,), in_specs=in_specs, out_specs=tuple(out_specs),
        compiler_params=_cparams(("parallel",)), name="add_ln_" + mode,
    )(*args)


def _embed_ln_kernel(x_ref, meta_ref, g_ref, b_ref, o_ref, ob_ref, *, front, n_meta, n_blocks):
    j = pl.program_id(1)

    @pl.when((j > 0) & (j <= n_blocks))
    def _():
        y = _layer_norm_rows(x_ref[...], g_ref, b_ref)
        o_ref[...] = y
        ob_ref[...] = y.astype(BF16)

    @pl.when(j == 0)
    def _():
        y = _layer_norm_rows(meta_ref[...], g_ref, b_ref)
        o_ref[...] = jnp.zeros_like(o_ref)
        ob_ref[...] = jnp.zeros_like(ob_ref)
        o_ref[front:front + n_meta, :] = y
        ob_ref[front:front + n_meta, :] = y.astype(BF16)

    @pl.when(j > n_blocks)
    def _():
        o_ref[...] = jnp.zeros_like(o_ref)
        ob_ref[...] = jnp.zeros_like(ob_ref)


def _embed_ln(x, meta_tokens, g, b, lay):
    front, t_valid, tpp = lay
    bsz, seq, d = x.shape
    n_meta = meta_tokens.shape[0]
    assert front + n_meta == CHUNK and seq % CHUNK == 0 and tpp == seq + 2 * CHUNK
    nb = seq // CHUNK
    blocks = tpp // CHUNK
    spec = pl.BlockSpec((CHUNK, d), lambda bb, j: (bb * blocks + j, 0))
    vec = pl.BlockSpec((1, d), lambda bb, j: (0, 0))
    return pl.pallas_call(
        functools.partial(_embed_ln_kernel, front=front, n_meta=n_meta, n_blocks=nb),
        out_shape=(jax.ShapeDtypeStruct((bsz * tpp, d), F32), jax.ShapeDtypeStruct((bsz * tpp, d), BF16)),
        grid=(bsz, blocks),
        in_specs=[pl.BlockSpec((None, CHUNK, d), lambda bb, j: (bb, jnp.clip(j - 1, 0, nb - 1), 0)),
                  pl.BlockSpec((n_meta, d), lambda bb, j: (0, 0)), vec, vec],
        out_specs=(spec, spec),
        compiler_params=_cparams(("parallel", "parallel")), name="embed_ln",
    )(x, meta_tokens.astype(x.dtype), g.reshape(1, d), b.reshape(1, d))


def _final_ln_kernel(h_ref, y0_ref, y1_ref, rt_ref, g_ref, b_ref, o_ref, *, alpha):
    x = alpha * h_ref[...] + rt_ref[:, 2:3] * y0_ref[...] + rt_ref[:, 3:4] * y1_ref[...]
    o_ref[...] = _layer_norm_rows(x, g_ref, b_ref)


def _final_ln(h, y2, route, g, b, alpha, lay, bsz, seq):
    front, t_valid, tpp = lay
    n, d = h.shape
    blocks = tpp // CHUNK
    first = (front + t_valid - seq) // CHUNK
    row = lambda bb, j: bb * blocks + first + j
    return pl.pallas_call(
        functools.partial(_final_ln_kernel, alpha=alpha),
        out_shape=jax.ShapeDtypeStruct((bsz, seq, d), F32),
        grid=(bsz, seq // CHUNK),
        in_specs=[pl.BlockSpec((CHUNK, d), lambda bb, j: (row(bb, j), 0)),
                  pl.BlockSpec((CHUNK, d), lambda bb, j: (row(bb, j), 0)),
                  pl.BlockSpec((CHUNK, d), lambda bb, j: (row(bb, j) + n // CHUNK, 0)),
                  pl.BlockSpec((CHUNK, LANE), lambda bb, j: (row(bb, j), 0)),
                  pl.BlockSpec((1, d), lambda bb, j: (0, 0)), pl.BlockSpec((1, d), lambda bb, j: (0, 0))],
        out_specs=pl.BlockSpec((None, CHUNK, d), lambda bb, j: (bb, j, 0)),
        compiler_params=_cparams(("parallel", "parallel")), name="final_ln",
    )(h, y2, y2, route, g.reshape(1, d), b.reshape(1, d))


def _out_proj_kernel(*refs, ks):
    a_refs = refs[:len(ks)]
    w_ref, o_ref, wb_ref = refs[len(ks):]

    @pl.when(pl.program_id(1) == 0)
    def _():
        step = min(w_ref.shape[0], 512)
        for r0 in range(0, w_ref.shape[0], step):
            wb_ref[r0:r0 + step, :] = w_ref[r0:r0 + step, :].astype(BF16)

    acc = None
    off = 0
    for a_ref, k in zip(a_refs, ks):
        p = _dot(a_ref[...], wb_ref[off:off + k, :])
        acc = p if acc is None else acc + p
        off += k
    o_ref[...] = acc.astype(o_ref.dtype)


def _out_proj(a_list, w_all, layer):
    n = a_list[0].shape[0]
    ks = tuple(a.shape[1] for a in a_list)
    _, kdim, ndim = w_all.shape
    assert sum(ks) == kdim
    tm = _divisor_tile(n, 768, 16)
    tn = _divisor_tile(ndim, 512, LANE)
    in_specs = [pl.BlockSpec((tm, k), lambda j, i: (i, 0)) for k in ks]
    in_specs.append(pl.BlockSpec((None, kdim, tn), lambda j, i: (layer, 0, j)))
    return pl.pallas_call(
        functools.partial(_out_proj_kernel, ks=ks),
        out_shape=jax.ShapeDtypeStruct((n, ndim), BF16),
        grid=(ndim // tn, n // tm),
        in_specs=in_specs,
        out_specs=pl.BlockSpec((tm, tn), lambda j, i: (i, j)),
        scratch_shapes=[pltpu.VMEM((kdim, tn), BF16)],
        compiler_params=_cparams(("parallel", "arbitrary")),
        name="out_proj",
    )(*a_list, w_all)


def _in_proj_kernel(a_ref, w_ref, wn_ref, o_ref, wb_ref, *, n_plain, shift):
    j = pl.program_id(0)
    tn = w_ref.shape[0]
    step = min(tn, LANE)

    @pl.when(pl.program_id(1) == 0)
    def _():
        @pl.when(j < n_plain)
        def _():
            for r0 in range(0, tn, step):
                wb_ref[r0:r0 + step, :] = w_ref[r0:r0 + step, :].astype(BF16)

        @pl.when(j >= n_plain)
        def _():
            for r0 in range(0, tn - step, step):
                wb_ref[r0:r0 + step, :] = w_ref[r0 + shift:r0 + shift + step, :].astype(BF16)
            wb_ref[tn - step:tn - shift, :] = w_ref[tn - step + shift:, :].astype(BF16)
            wb_ref[tn - shift:, :] = wn_ref[:shift, :].astype(BF16)

    o_ref[...] = _dot_nt(a_ref[...], wb_ref[...])


def _small_proj_kernel(a_ref, w_ref, o_ref):
    o_ref[...] = _dot_nt(a_ref[...], w_ref[...].astype(BF16))


def _in_proj(hb, w_in_t, layer, w):
    n, d = hb.shape
    n_small = w_in_t.shape[1] - 15 * w
    tm = _divisor_tile(n, 1408, 16)
    tn = _divisor_tile(w, 512, LANE)
    z = pl.pallas_call(
        functools.partial(_in_proj_kernel, n_plain=4 * w // tn, shift=n_small),
        out_shape=jax.ShapeDtypeStruct((n, 15 * w), F32),
        grid=(15 * w // tn, n // tm),
        in_specs=[pl.BlockSpec((tm, d), lambda j, i: (i, 0)),
                  pl.BlockSpec((None, tn, d), lambda j, i: (layer, j, 0)),
                  pl.BlockSpec((None, LANE, d), lambda j, i: (layer, (j + 1) * (tn // LANE), 0))],
        out_specs=pl.BlockSpec((tm, tn), lambda j, i: (i, j)),
        scratch_shapes=[pltpu.VMEM((tn, d), BF16)],
        compiler_params=_cparams(("parallel", "arbitrary")),
        name="in_proj",
    )(hb, w_in_t, w_in_t)
    zs = pl.pallas_call(
        _small_proj_kernel,
        out_shape=jax.ShapeDtypeStruct((n, LANE), F32),
        grid=(n // tm,),
        in_specs=[pl.BlockSpec((tm, d), lambda i: (i, 0)),
                  pl.BlockSpec((None, LANE, d), lambda i: (layer, 4 * w // LANE, 0))],
        out_specs=pl.BlockSpec((tm, LANE), lambda i: (i, 0)),
        compiler_params=_cparams(("parallel",)),
        name="small_proj",
    )(hb, w_in_t)
    return z, zs


def _tri_masks(n):
    r = lax.broadcasted_iota(jnp.int32, (n, n), 0)
    c = lax.broadcasted_iota(jnp.int32, (n, n), 1)
    return r, c


def _gated_rms(o, w_row, gate):
    o = o * lax.rsqrt(jnp.mean(o * o, axis=-1, keepdims=True) + NORM_EPS)
    if w_row is not None:
        o = o * w_row
    return o * _silu(gate)


def _prefix_rows(x, row, seg):
    pos = row & (seg - 1)
    s = 1
    while s < seg:
        x = x + jnp.where(pos >= s, pltpu.roll(x, s, 0), 0.0)
        s *= 2
    return x


def _mixer_a_kernel(q_ref, k_ref, v_ref, gt_ref, sm_ref, cw_ref, alog_ref, dtb_ref, nw_ref, o_ref,
                    s_ref, prev_ref, *, n_heads, front, t_valid):
    c = pl.program_id(1)

    @pl.when(c == 0)
    def _():
        s_ref[...] = jnp.zeros_like(s_ref)
        prev_ref[...] = jnp.zeros_like(prev_ref)

    row = lax.broadcasted_iota(jnp.int32, (CHUNK, 1), 0)
    row8 = lax.broadcasted_iota(jnp.int32, (SUBLANES, 1), 0)
    valid = _valid_rows(c * CHUNK, CHUNK, front, t_valid).astype(F32)

    def conv_silu(x_ref, p):
        cur = x_ref[...]
        prev8 = prev_ref[p]
        acc = cur * cw_ref[p, 3:4, :]
        for s in (1, 2, 3):
            rolled = pltpu.roll(cur, s, 0)
            top = jnp.where(row8 >= s, rolled[:SUBLANES], pltpu.roll(prev8, s, 0))
            acc = acc + jnp.concatenate([top, rolled[SUBLANES:]], axis=0) * cw_ref[p, 3 - s:4 - s, :]
        prev_ref[p] = cur[CHUNK - SUBLANES:]
        return _silu(acc)

    q_all = conv_silu(q_ref, 0)
    k_all = conv_silu(k_ref, 1) * valid
    v_all = conv_silu(v_ref, 2) * valid

    sm = sm_ref[...]
    beta_all = _sigmoid(sm) * valid
    g_all = -jnp.exp(alog_ref[...]) * _softplus(sm + dtb_ref[...]) * valid
    gcum_all = _prefix_rows(g_all, row, CHUNK)
    r, cc = _tri_masks(CHUNK)
    causal = r >= cc
    strict = r > cc
    eye = (r == cc).astype(F32)
    gcum_t = lax.dot_general(gcum_all, eye, (((0,), (0,)), ((), ())), preferred_element_type=F32,
                             precision=HI)

    heads = range(n_heads)
    hsl = [slice(h * HEAD_DIM, (h + 1) * HEAD_DIM) for h in heads]
    beta = [beta_all[:, h:h + 1] for h in heads]
    gcol = [gcum_all[:, n_heads + h:n_heads + h + 1] for h in heads]
    decay = [jnp.where(causal, jnp.exp(jnp.minimum(gcol[h] - gcum_t[n_heads + h:n_heads + h + 1, :], 0.0)), 0.0)
             for h in heads]
    eg = [jnp.exp(g) for g in gcol]
    g_last = [g[CHUNK - 1:CHUNK, :] for g in gcol]
    q = [q_all[:, s] * lax.rsqrt(jnp.sum(q_all[:, s] * q_all[:, s], axis=-1, keepdims=True) + NORM_EPS)
         * HEAD_DIM ** -0.5 for s in hsl]
    k = [k_all[:, s] * lax.rsqrt(jnp.sum(k_all[:, s] * k_all[:, s], axis=-1, keepdims=True) + NORM_EPS) for s in hsl]
    kb = [k[h] * beta[h] for h in heads]
    k16 = [x.astype(BF16) for x in k]
    m = [jnp.where(strict, -(_dot_nt(kb[h].astype(BF16), k16[h]) * decay[h]), 0.0) for h in heads]
    attn = [_dot_nt(q[h].astype(BF16), k16[h]) * decay[h] for h in heads]
    inv = [eye + x for x in m]
    mp = [_dot(x.astype(BF16), x.astype(BF16)) for x in m]
    for _ in range(4):
        both = [_dot(jnp.concatenate([inv[h], mp[h]], axis=0).astype(BF16), mp[h].astype(BF16)) for h in heads]
        inv = [inv[h] + both[h][:CHUNK] for h in heads]
        mp = [both[h][CHUNK:] for h in heads]
    inv = [inv[h] + _dot(inv[h].astype(BF16), mp[h].astype(BF16)) for h in heads]
    sol = [_dot(inv[h].astype(BF16),
                jnp.concatenate([v_all[:, hsl[h]] * beta[h], kb[h] * eg[h]], axis=1).astype(BF16)) for h in heads]
    s_old = [s_ref[h] for h in heads]
    s16 = [x.astype(BF16) for x in s_old]
    v_new = [sol[h][:, :HEAD_DIM] - _dot(sol[h][:, HEAD_DIM:].astype(BF16), s16[h]) for h in heads]
    o = [_dot((q[h] * eg[h]).astype(BF16), s16[h]) + _dot(attn[h].astype(BF16), v_new[h].astype(BF16)) for h in heads]
    for h in heads:
        s_ref[h] = s_old[h] * jnp.exp(g_last[h]) + _dot_tn(k[h] * jnp.exp(g_last[h] - gcol[h]), v_new[h])
        o_ref[:, hsl[h]] = _gated_rms(o[h], nw_ref[...], gt_ref[:, hsl[h]]).astype(o_ref.dtype)


def _mixer_b_kernel(q_ref, f_ref, i_ref, gt_ref, lb_ref, nw_ref, o_ref, st_ref, kbuf, bbuf, obuf,
                    *, n_heads, front, t_valid):
    c = pl.program_id(1)

    @pl.when(c == 0)
    def _():
        st_ref[...] = jnp.zeros_like(st_ref)

    row = lax.broadcasted_iota(jnp.int32, (CHUNK, 1), 0)
    valid = _valid_rows(c * CHUNK, CHUNK, front, t_valid).astype(F32)
    row16 = lax.broadcasted_iota(jnp.int32, (SUB, 1), 0)

    lb = lb_ref[...]
    f = lb + (1.0 - lb) * _sigmoid(f_ref[...])
    kbuf[...] = (1.0 - f) * valid
    bbuf[...] = _prefix_rows(jnp.log(f) * valid, row, SUB)

    heads = range(n_heads)
    hsl = [slice(h * HEAD_DIM, (h + 1) * HEAD_DIM) for h in heads]
    st = [st_ref[h] for h in heads]
    for blk in range(CHUNK // SUB):
        rs = slice(blk * SUB, (blk + 1) * SUB)
        for h in heads:
            hs = hsl[h]
            b16 = bbuf[rs, hs]
            q16 = _silu(q_ref[rs, hs])
            b_end = b16[SUB - 1:SUB, :]
            acc = _dot_nt((q16 * jnp.exp(b16)).astype(BF16), st[h].astype(BF16))
            parts = []
            for t0 in range(0, SUB, SUBLANES):
                ts = slice(t0, t0 + SUBLANES)
                bq, qq, rowq = b16[ts], q16[ts], row16[ts]
                part = acc[ts]
                for j in range(min(t0 + SUBLANES, SUB)):
                    jr = pl.ds(blk * SUB + j, 1)
                    sc = jnp.sum(qq * kbuf[jr, hs] * jnp.exp(bq - bbuf[jr, hs]), axis=-1, keepdims=True)
                    if j > t0:
                        sc = jnp.where(rowq >= j, sc, 0.0)
                    part = part + sc * i_ref[jr, hs]
                parts.append(part)
            acc = jnp.concatenate(parts, axis=0)
            st[h] = st[h] * jnp.exp(b_end) + _dot_tn(i_ref[rs, hs], kbuf[rs, hs] * jnp.exp(b_end - b16))
            obuf[rs, hs] = acc
    for h in heads:
        st_ref[h] = st[h]
        o_ref[:, hsl[h]] = _gated_rms(obuf[:, hsl[h]], nw_ref[...], gt_ref[:, hsl[h]]).astype(o_ref.dtype)


def _rope_partial(x, cos, sin_lo, sin_hi):
    half = ROPE_DIMS // 2
    return x * cos + pltpu.roll(x, LANE - half, 1) * sin_lo + pltpu.roll(x, half, 1) * sin_hi


def _rope_c_kernel(q_ref, k_ref, v_ref, cos_ref, slo_ref, shi_ref, qo_ref, ko_ref, vo_ref, *, n_maps):
    cos, slo, shi = cos_ref[...], slo_ref[...], shi_ref[...]
    for mp in range(n_maps):
        ms = slice(mp * HEAD_DIM, (mp + 1) * HEAD_DIM)
        qo_ref[:, ms] = (_rope_partial(q_ref[:, ms], cos, slo, shi) * HEAD_DIM ** -0.5).astype(BF16)
        ko_ref[:, ms] = _rope_partial(k_ref[:, ms], cos, slo, shi).astype(BF16)
    vo_ref[...] = v_ref[...].astype(BF16)


def _mixer_c_kernel(qi_ref, kj_ref, q_ref, k_ref, v_ref, lam_ref, nw_ref, o_ref, m_ref, l_ref, acc_ref,
                    *, tq, front, out_scale, n_heads):
    p = pl.program_id(1)
    i = qi_ref[p]
    j = kj_ref[p]
    vw = 2 * HEAD_DIM

    @pl.when(j == 0)
    def _():
        m_ref[...] = jnp.full_like(m_ref, NEG)
        l_ref[...] = jnp.zeros_like(l_ref)
        acc_ref[...] = jnp.zeros_like(acc_ref)

    def step(masked):
        if masked:
            rq = i * tq + lax.broadcasted_iota(jnp.int32, (tq, 1), 0)
            rk = j * tq + lax.broadcasted_iota(jnp.int32, (1, tq), 1)
            msk = (rk <= rq) & (rk >= front)
        for hd in range(n_heads):
            v = v_ref[:, hd * vw:(hd + 1) * vw]
            rows = [2 * hd, 2 * hd + 1]
            msl = [slice(r * HEAD_DIM, (r + 1) * HEAD_DIM) for r in rows]
            s = [_dot_nt(q_ref[:, ms], k_ref[:, ms]) for ms in msl]
            if masked:
                s = [jnp.where(msk, x, NEG) for x in s]
            m_prev = [m_ref[r] for r in rows]
            m_new = [jnp.maximum(m_prev[t], jnp.max(s[t], axis=-1, keepdims=True)) for t in range(2)]
            pr = [jnp.exp(s[t] - jnp.tile(m_new[t], (1, tq // LANE))) for t in range(2)]
            pv = [_dot(pr[t].astype(BF16), v) for t in range(2)]
            for t, r in enumerate(rows):
                a = jnp.exp(m_prev[t] - m_new[t])
                l_ref[r] = a * l_ref[r] + jnp.sum(pr[t], axis=-1, keepdims=True)
                acc_ref[r] = jnp.tile(a, (1, vw // LANE)) * acc_ref[r] + pv[t]
                m_ref[r] = m_new[t]

    needs_mask = (j == i) | (j == 0)

    @pl.when(needs_mask)
    def _():
        step(True)

    @pl.when(jnp.logical_not(needs_mask))
    def _():
        step(False)

    @pl.when(j == i)
    def _():
        reps = (1, vw // LANE)
        for hd in range(n_heads):
            r0, r1 = 2 * hd, 2 * hd + 1
            o = acc_ref[r0] / jnp.tile(l_ref[r0], reps) - lam_ref[...] * (acc_ref[r1] / jnp.tile(l_ref[r1], reps))
            o = o * lax.rsqrt(jnp.mean(o * o, axis=-1, keepdims=True) + NORM_EPS) * nw_ref[...] * out_scale
            o_ref[:, hd * vw:(hd + 1) * vw] = o.astype(o_ref.dtype)


def _mixer_d_kernel(q_ref, k_ref, v_ref, gt_ref, cos_ref, sin_ref, dm_ref, qd_ref, kd_ref, cd_ref, o_ref, s_ref,
                    *, n_heads):
    c = pl.program_id(1)

    @pl.when(c == 0)
    def _():
        s_ref[...] = jnp.zeros_like(s_ref)

    cos = cos_ref[...]
    sin = sin_ref[...]
    heads = range(n_heads)
    hsl = [slice(h * HEAD_DIM, (h + 1) * HEAD_DIM) for h in heads]
    q = [q_ref[:, s] * cos + pltpu.roll(q_ref[:, s], HEAD_DIM // 2, 1) * sin for s in hsl]
    k = [(k_ref[:, s] * cos + pltpu.roll(k_ref[:, s], HEAD_DIM // 2, 1) * sin) * HEAD_DIM ** -0.5 for s in hsl]
    s_old = [s_ref[h] for h in heads]
    a = [_dot_nt(q[h].astype(BF16), k[h].astype(BF16)) * dm_ref[h] for h in heads]
    o = [_dot(a[h].astype(BF16), v_ref[:, hsl[h]].astype(BF16))
         + _dot((q[h] * qd_ref[h]).astype(BF16), s_old[h].astype(BF16)) for h in heads]
    for h in heads:
        s_ref[h] = cd_ref[h] * s_old[h] + _dot_tn(k[h] * kd_ref[h], v_ref[:, hsl[h]])
        o_ref[:, hsl[h]] = _gated_rms(o[h], None, gt_ref[:, hsl[h]]).astype(o_ref.dtype)


def _route(lg):
    n_e = N_GROUPS * EXP_PER_GROUP
    lane_i = lax.broadcasted_iota(jnp.int32, lg.shape, 1)
    lane = lane_i.astype(F32)
    grp = (lane_i >> GROUP_SHIFT).astype(F32)
    big = float(4 * LANE)
    is_g = (lane_i >= n_e) & (lane_i < n_e + N_GROUPS)
    gl = jnp.where(is_g, lg, NEG)
    gmax = jnp.max(gl, axis=-1, keepdims=True)
    gsum = jnp.sum(jnp.where(is_g, jnp.exp(gl - gmax), 0.0), axis=-1, keepdims=True)
    g_w = 1.0 / gsum
    g_idx = jnp.min(jnp.where(is_g & (gl == gmax), lane - n_e, big), axis=-1, keepdims=True)
    in_grp = (lane_i < n_e) & (grp == g_idx)
    el = jnp.where(in_grp, lg, NEG)
    emax = jnp.max(el, axis=-1, keepdims=True)
    eexp = jnp.where(in_grp, jnp.exp(el - emax), 0.0)
    pe = eexp / jnp.sum(eexp, axis=-1, keepdims=True)
    p1 = jnp.max(jnp.where(in_grp, pe, -1.0), axis=-1, keepdims=True)
    i1 = jnp.min(jnp.where(in_grp & (pe == p1), lane, big), axis=-1, keepdims=True)
    rest = in_grp & (lane != i1)
    p2 = jnp.max(jnp.where(rest, pe, -1.0), axis=-1, keepdims=True)
    i2 = jnp.min(jnp.where(rest & (pe == p2), lane, big), axis=-1, keepdims=True)
    den = p1 + p2
    return (jnp.where(lane_i == 0, i1, 0.0) + jnp.where(lane_i == 1, i2, 0.0)
            + jnp.where(lane_i == 2, g_w * (p1 / den), 0.0) + jnp.where(lane_i == 3, g_w * (p2 / den), 0.0))


def _row_copy(src, src_row, dst, dst_row, sem):
    return pltpu.make_async_copy(src.at[pl.ds(src_row, 1)], dst.at[pl.ds(dst_row, 1)], sem)


def _moe_kernel(te_ref, cnt_ref, first_ref, src_ref, srcn_ref, dst_ref, x_hbm, wg_ref, wu_ref, wd_ref, y_hbm,
                xbuf, ybuf, wgb, wub, wdb, sem_in, sem_out, *, tm, n_tiles):
    del te_ref
    t = pl.program_id(0)
    slot = t % 2
    cnt = cnt_ref[t]

    def pairs(rows):
        return (rows + 1) // 2

    def gather_start(idx_ref, s, rows):
        def body(g, c):
            for u in range(2):
                r = 2 * g + u
                _row_copy(x_hbm, idx_ref[0, 0, r], xbuf.at[s], r, sem_in.at[s]).start(priority=u)
            return c
        lax.fori_loop(0, pairs(rows), body, 0)

    def wait_rows(src, dst, sem, rows):
        def tile_body(g, c):
            pltpu.make_async_copy(src.at[pl.ds(0, SUBLANES)], dst.at[pl.ds(g * SUBLANES, SUBLANES)], sem).wait()
            return c
        lax.fori_loop(0, rows // SUBLANES, tile_body, 0)

        def row_body(r, c):
            _row_copy(src, 0, dst, r, sem).wait()
            return c
        lax.fori_loop(rows - rows % SUBLANES, rows, row_body, 0)

    def gather_wait(s, rows):
        wait_rows(x_hbm, xbuf.at[s], sem_in.at[s], 2 * pairs(rows))

    def scatter_start(s, rows):
        def body(g, c):
            for u in range(2):
                r = 2 * g + u
                _row_copy(ybuf.at[s], r, y_hbm, dst_ref[0, 0, r], sem_out.at[s]).start(priority=u)
            return c
        lax.fori_loop(0, rows // 2, body, 0)

        @pl.when(rows % 2 == 1)
        def _():
            _row_copy(ybuf.at[s], rows - 1, y_hbm, dst_ref[0, 0, rows - 1], sem_out.at[s]).start()

    def scatter_wait(s, rows):
        wait_rows(ybuf.at[s], y_hbm, sem_out.at[s], rows)

    @pl.when(t == 0)
    def _():
        xbuf[...] = jnp.zeros_like(xbuf)

        @pl.when(cnt > 0)
        def _():
            gather_start(src_ref, 0, cnt)

    nxt = jnp.minimum(t + 1, n_tiles - 1)

    @pl.when((t + 1 < n_tiles) & (cnt_ref[nxt] > 0))
    def _():
        gather_start(srcn_ref, 1 - slot, cnt_ref[nxt])

    old = jnp.maximum(t - 2, 0)

    @pl.when((t >= 2) & (cnt_ref[old] > 0))
    def _():
        scatter_wait(slot, cnt_ref[old])

    @pl.when(cnt > 0)
    def _():
        @pl.when(first_ref[t] == 1)
        def _():
            wgb[...] = wg_ref[...].astype(BF16)
            wub[...] = wu_ref[...].astype(BF16)
            wdb[...] = wd_ref[...].astype(BF16)

        gather_wait(slot, cnt)
        x = xbuf[slot].astype(BF16)
        hg = _dot(x, wgb[...])
        hu = _dot(x, wub[...])
        act = (_silu(hg) * hu).astype(BF16)
        ybuf[slot] = _dot(act, wdb[...])
        scatter_start(slot, cnt)

    @pl.when(t == n_tiles - 1)
    def _():
        prev = jnp.maximum(t - 1, 0)

        @pl.when((t >= 1) & (cnt_ref[prev] > 0))
        def _():
            scatter_wait(1 - slot, cnt_ref[prev])

        @pl.when(cnt > 0)
        def _():
            scatter_wait(slot, cnt)


def _dispatch(route, n, tm, n_e):
    ids = jnp.concatenate([route[:, 0], route[:, 1]]).astype(jnp.int32)
    n_pairs = 2 * n
    n_tiles = n_pairs // tm + n_e
    order = jnp.argsort(ids, stable=True).astype(jnp.int32)
    counts = jnp.sum((ids[:, None] == jnp.arange(n_e, dtype=jnp.int32)[None, :]).astype(jnp.int32), axis=0)
    tiles_e = (counts + tm - 1) // tm
    tile_end = jnp.cumsum(tiles_e)
    tile_start = tile_end - tiles_e
    first = jnp.cumsum(counts) - counts
    tile = jnp.arange(n_tiles, dtype=jnp.int32)
    te = jnp.minimum(jnp.searchsorted(tile_end, tile, side="right").astype(jnp.int32), n_e - 1)
    cnt = jnp.clip(counts[te] - (tile - tile_start[te]) * tm, 0, tm)
    cnt = jnp.where(tile < tile_end[n_e - 1], cnt, 0).astype(jnp.int32)
    row = jnp.arange(tm, dtype=jnp.int32)[None, :]
    spos = (first[te] + (tile - tile_start[te]) * tm)[:, None] + row
    dst = jnp.where(row < cnt[:, None], order[jnp.clip(spos, 0, n_pairs - 1)], 0)
    src = dst % n
    te = jnp.where(cnt > 0, te, jnp.max(jnp.where(cnt > 0, te, 0)))
    new_expert = jnp.concatenate([jnp.ones((1,), jnp.int32), (te[1:] != te[:-1]).astype(jnp.int32)])
    return te, cnt, new_expert, src.reshape(n_tiles, 1, tm), dst.reshape(n_tiles, 1, tm)


def _moe_routed(h, route, wg, wu, wd, layer, tm):
    n, d = h.shape
    _, n_e, _, f = wg.shape
    te, cnt, new_expert, src, dst = _dispatch(route, n, tm, n_e)
    n_tiles = te.shape[0]

    def idx_spec(shift):
        return pl.BlockSpec((1, 1, tm), lambda t, *_: (jnp.minimum(t + shift, n_tiles - 1), 0, 0),
                            memory_space=pltpu.SMEM)

    grid_spec = pltpu.PrefetchScalarGridSpec(
        num_scalar_prefetch=3, grid=(n_tiles,),
        in_specs=[idx_spec(0), idx_spec(1), idx_spec(0),
                  pl.BlockSpec(memory_space=pl.ANY),
                  pl.BlockSpec((None, None, d, f), lambda t, te, *_: (layer, te[t], 0, 0)),
                  pl.BlockSpec((None, None, d, f), lambda t, te, *_: (layer, te[t], 0, 0)),
                  pl.BlockSpec((None, None, f, d), lambda t, te, *_: (layer, te[t], 0, 0))],
        out_specs=pl.BlockSpec(memory_space=pl.ANY),
        scratch_shapes=[pltpu.VMEM((2, tm, d), F32), pltpu.VMEM((2, tm, d), F32),
                        pltpu.VMEM((d, f), BF16), pltpu.VMEM((d, f), BF16), pltpu.VMEM((f, d), BF16),
                        pltpu.SemaphoreType.DMA((2,)), pltpu.SemaphoreType.DMA((2,))])
    return pl.pallas_call(
        functools.partial(_moe_kernel, tm=tm, n_tiles=n_tiles),
        out_shape=jax.ShapeDtypeStruct((2 * n, d), F32),
        grid_spec=grid_spec,
        compiler_params=_cparams(("arbitrary",)),
        name="moe_routed",
    )(te, cnt, new_expert, src, src, dst, h, wg, wu, wd)


def _mixers(z, zs, lay, bsz, w, prm):
    front, t_valid, tpp = lay
    n = z.shape[0]
    nc = tpp // CHUNK
    heads = w // HEAD_DIM
    c_heads = w // (2 * HEAD_DIM)

    def cspec(region):
        return pl.BlockSpec((CHUNK, w), lambda b, c, region=region: (b * nc + c, region))

    def fixed(shape):
        nd = len(shape)
        return pl.BlockSpec(shape, lambda b, c: (0,) * nd)

    ospec = pl.BlockSpec((CHUNK, w), lambda b, c: (b * nc + c, 0))
    oshape = jax.ShapeDtypeStruct((n, w), BF16)
    grid = (bsz, nc)
    sem = ("parallel", "arbitrary")
    state = pltpu.VMEM((heads, HEAD_DIM, HEAD_DIM), F32)

    oa = pl.pallas_call(
        functools.partial(_mixer_a_kernel, n_heads=heads, front=front, t_valid=t_valid),
        out_shape=oshape, grid=grid,
        in_specs=[cspec(0), cspec(1), cspec(2), cspec(3),
                  pl.BlockSpec((CHUNK, LANE), lambda b, c: (b * nc + c, 0)),
                  fixed((3, 4, w)), fixed((1, LANE)), fixed((1, LANE)), fixed((1, HEAD_DIM))],
        out_specs=ospec,
        scratch_shapes=[state, pltpu.VMEM((3, SUBLANES, w), F32)],
        compiler_params=_cparams(sem), name="mixer_a",
    )(z, z, z, z, zs, prm["conv_w"], prm["a_log_row"], prm["dt_bias_row"], prm["norm_a"])

    ob = pl.pallas_call(
        functools.partial(_mixer_b_kernel, n_heads=heads, front=front, t_valid=t_valid),
        out_shape=oshape, grid=grid,
        in_specs=[cspec(4), cspec(5), cspec(6), cspec(7), fixed((1, w)), fixed((1, HEAD_DIM))],
        out_specs=ospec,
        scratch_shapes=[state] + [pltpu.VMEM((CHUNK, w), F32)] * 3,
        compiler_params=_cparams(sem), name="mixer_b",
    )(z, z, z, z, prm["lb"], prm["norm_b"])

    tq = _divisor_tile(tpp, 384, LANE)
    nq = tpp // tq
    vw = 2 * HEAD_DIM
    rc = prm["rope_c"]
    rspec = pl.BlockSpec((tq, w), lambda i: (i, 0))
    tab = pl.BlockSpec((tq, HEAD_DIM), lambda i: (i % nq, 0))
    qr, kr, vr = pl.pallas_call(
        functools.partial(_rope_c_kernel, n_maps=heads),
        out_shape=(oshape, oshape, oshape), grid=(n // tq,),
        in_specs=[pl.BlockSpec((tq, w), lambda i: (i, 8)), pl.BlockSpec((tq, w), lambda i: (i, 9)),
                  pl.BlockSpec((tq, w), lambda i: (i, 10)), tab, tab, tab],
        out_specs=(rspec, rspec, rspec),
        compiler_params=_cparams(("parallel",)), name="rope_c",
    )(z, z, z, rc[0], rc[1], rc[2])

    pairs = [(i, j) for i in range(nq) for j in range(i + 1)]
    qi = jnp.asarray([p[0] for p in pairs], jnp.int32)
    kj = jnp.asarray([p[1] for p in pairs], jnp.int32)
    qspec = pl.BlockSpec((tq, w), lambda b, p, qi, kj: (b * nq + qi[p], 0))
    kspec = pl.BlockSpec((tq, w), lambda b, p, qi, kj: (b * nq + kj[p], 0))
    row_vw = pl.BlockSpec((1, vw), lambda b, p, qi, kj: (0, 0))
    oc = pl.pallas_call(
        functools.partial(_mixer_c_kernel, tq=tq, front=front, out_scale=prm["c_out_scale"], n_heads=c_heads),
        out_shape=oshape,
        grid_spec=pltpu.PrefetchScalarGridSpec(
            num_scalar_prefetch=2, grid=(bsz, len(pairs)),
            in_specs=[qspec, kspec, kspec, row_vw, row_vw],
            out_specs=qspec,
            scratch_shapes=[pltpu.VMEM((2 * c_heads, tq, LANE), F32), pltpu.VMEM((2 * c_heads, tq, LANE), F32),
                            pltpu.VMEM((2 * c_heads, tq, vw), F32)]),
        compiler_params=_cparams(("parallel", "arbitrary")), name="mixer_c",
    )(qi, kj, qr, kr, vr, prm["lam_row"], prm["subln_c"])

    tabd = pl.BlockSpec((CHUNK, HEAD_DIM), lambda b, c: (c, 0))
    rd = prm["rope_d"]
    od = pl.pallas_call(
        functools.partial(_mixer_d_kernel, n_heads=heads),
        out_shape=oshape, grid=grid,
        in_specs=[cspec(11), cspec(12), cspec(13), cspec(14), tabd, tabd,
                  fixed((heads, CHUNK, CHUNK)), fixed((heads, CHUNK, HEAD_DIM)), fixed((heads, CHUNK, HEAD_DIM)),
                  fixed((heads, 1, HEAD_DIM))],
        out_specs=ospec,
        scratch_shapes=[state],
        compiler_params=_cparams(sem), name="mixer_d",
    )(z, z, z, z, rd[0], rd[1], prm["ret_dmask"], prm["ret_qdec"], prm["ret_kdec"], prm["ret_cdec"])
    return oa, ob, oc, od


def _rope_tables(front, tpp):
    pos = (jnp.arange(tpp) - front).astype(F32)[:, None]
    half = ROPE_DIMS // 2
    inv = 1.0 / (ROPE_THETA ** (jnp.arange(half, dtype=F32) / half))
    ang = pos * inv[None, :]
    z_rest = jnp.zeros((tpp, HEAD_DIM - ROPE_DIMS), F32)
    z_half = jnp.zeros((tpp, half), F32)
    cos_c = jnp.concatenate([jnp.cos(ang), jnp.cos(ang), jnp.ones_like(z_rest)], axis=1)
    sin_lo = jnp.concatenate([-jnp.sin(ang), z_half, z_rest], axis=1)
    sin_hi = jnp.concatenate([z_half, jnp.sin(ang), z_rest], axis=1)
    half_d = HEAD_DIM // 2
    inv_d = 1.0 / (RET_THETA ** (jnp.arange(half_d, dtype=F32) / half_d))
    ang_d = pos * inv_d[None, :]
    cos_d = jnp.concatenate([jnp.cos(ang_d), jnp.cos(ang_d)], axis=1)
    sin_d = jnp.concatenate([-jnp.sin(ang_d), jnp.sin(ang_d)], axis=1)
    return (cos_c, sin_lo, sin_hi), (cos_d, sin_d)


def _retention_tables(n_heads):
    lg = jnp.log(1.0 - 2.0 ** (-5.0 - jnp.arange(n_heads, dtype=F32)))
    idx = jnp.arange(CHUNK, dtype=F32)
    rel = idx[:, None] - idx[None, :]
    causal = idx[:, None] >= idx[None, :]
    dmask = jnp.exp(jnp.where(causal[None], rel[None] * lg[:, None, None], -jnp.inf))
    qdec = jnp.exp((idx[None, :] + 1.0) * lg[:, None])
    kdec = jnp.exp((CHUNK - 1.0 - idx[None, :]) * lg[:, None])
    cdec = jnp.exp(CHUNK * lg)
    bc = lambda t: jnp.broadcast_to(t[..., None], t.shape + (HEAD_DIM,))
    return dmask, bc(qdec), bc(kdec), bc(cdec[:, None])


def _row(v, width=LANE, offset=0):
    out = jnp.zeros((1, width), F32)
    return out.at[0, offset:offset + v.shape[0]].set(v.astype(F32))


def kernel(x, meta_tokens, emb_ln_g, emb_ln_b, w_in, conv_a, a_log, dt_bias, norm_a, hgrn_lb, norm_b, lam_q1, lam_k1,
           lam_q2, lam_k2, subln_c, w_out, ln1_g, ln1_b, w_rg, b_rg, w_re, b_re, w_gate, w_up, w_down, ln2_g, ln2_b):
    bsz, seq, d = x.shape
    depth = w_in.shape[0]
    n_meta = meta_tokens.shape[0]
    w = d // 4
    a_heads = w // HEAD_DIM
    t_valid = n_meta + seq
    front = (-t_valid) % CHUNK
    tpp = -(-(front + t_valid) // ROW_ALIGN) * ROW_ALIGN
    lay = (front, t_valid, tpp)
    alpha = (2 * depth) ** 0.25

    h, hb = _embed_ln(x, meta_tokens, emb_ln_g, emb_ln_b, lay)

    lbs = jax.nn.softmax(hgrn_lb.astype(F32), axis=0)
    lbs = jnp.cumsum(lbs, axis=0) - lbs[0:1]
    rope_c, rope_d = _rope_tables(front, tpp)
    dmask, qdec, kdec, cdec = _retention_tables(a_heads)

    w_in_t = jnp.swapaxes(w_in, 1, 2)

    for l in range(depth):
        lam_init = 0.8 - 0.6 * math.exp(-0.3 * l)
        lam = (jnp.exp(jnp.sum(lam_q1[l].astype(F32) * lam_k1[l].astype(F32)))
               - jnp.exp(jnp.sum(lam_q2[l].astype(F32) * lam_k2[l].astype(F32))) + lam_init)
        prm = {
            "conv_w": conv_a[l].astype(F32).reshape(3, w, -1).transpose(0, 2, 1),
            "a_log_row": _row(a_log[l], offset=a_heads),
            "dt_bias_row": _row(dt_bias[l], offset=a_heads),
            "norm_a": norm_a[l].astype(F32).reshape(1, HEAD_DIM),
            "lb": lbs[l].reshape(1, w),
            "norm_b": norm_b[l].astype(F32).reshape(1, HEAD_DIM),
            "rope_c": rope_c, "rope_d": rope_d,
            "lam_row": jnp.full((1, 2 * HEAD_DIM), lam, F32),
            "subln_c": subln_c[l].astype(F32).reshape(1, 2 * HEAD_DIM),
            "c_out_scale": 1.0 - lam_init,
            "ret_dmask": dmask, "ret_qdec": qdec, "ret_kdec": kdec, "ret_cdec": cdec,
        }
        z, zs = _in_proj(hb, w_in_t, l, w)
        o_parts = _mixers(z, zs, lay, bsz, w, prm)
        y = _out_proj(list(o_parts), w_out, l)
        n_e = w_re.shape[2]
        w_r = jnp.concatenate([w_re[l], w_rg[l], jnp.zeros((d, LANE - n_e - N_GROUPS), F32)], axis=1).astype(F32)
        b_r = jnp.concatenate([b_re[l], b_rg[l], jnp.zeros((LANE - n_e - N_GROUPS,), F32)]).reshape(1, LANE)
        w_r_hi = w_r.astype(BF16)
        w_r_lo = (w_r - w_r_hi.astype(F32)).astype(BF16)
        h, hb, route = _add_ln(h, y, ln1_g[l], ln1_b[l], alpha, lay, router=(w_r_hi, w_r_lo, b_r))
        y2 = _moe_routed(h, route, w_gate, w_up, w_down, l, MOE_TILE)
        if l + 1 < depth:
            h, hb = _add_ln(h, y2, ln2_g[l], ln2_b[l], alpha, lay, route=route)

    return _final_ln(h, y2, route, ln2_g[depth - 1], ln2_b[depth - 1], alpha, lay, bsz, seq)
```

```python
import functools
import math

import jax
import jax.numpy as jnp
from jax import lax
from jax.experimental import pallas as pl
from jax.experimental.pallas import tpu as pltpu

HEAD_DIM = 128
CHUNK = 64
SUB = 16
GROUP_SHIFT = 3
ROPE_THETA = 500000.0
ROPE_DIMS = HEAD_DIM // 4
RET_THETA = 10000.0
N_GROUPS = 8
EXP_PER_GROUP = 8
NORM_EPS = 1e-6
LN_EPS = 1e-5
LANE = 128
SUBLANES = 8
ROW_ALIGN = 2 * CHUNK
V7X_VMEM_LIMIT_BYTES = 56 * 1024 * 1024
MOE_TILE = 256
NEG = -1e30

F32 = jnp.float32
BF16 = jnp.bfloat16
HI = lax.Precision.HIGHEST


def _cparams(sem):
    return pltpu.CompilerParams(dimension_semantics=sem, vmem_limit_bytes=V7X_VMEM_LIMIT_BYTES)


def _divisor_tile(n, cap, align):
    best = None
    for t in range(align, min(n, cap) + 1, align):
        if n % t == 0:
            best = t
    assert best is not None, (n, cap, align)
    return best


def _dot(a, b):
    return jnp.dot(a, b, preferred_element_type=F32)


def _dot_nt(a, b):
    return lax.dot_general(a, b, (((1,), (1,)), ((), ())), preferred_element_type=F32)


def _dot_tn(a, b):
    return lax.dot_general(a, b, (((0,), (0,)), ((), ())), preferred_element_type=F32)


def _sigmoid(x):
    return 1.0 / (1.0 + jnp.exp(-x))


def _silu(x):
    return x * _sigmoid(x)


def _softplus(x):
    return jnp.maximum(x, 0.0) + jnp.log(1.0 + jnp.exp(-jnp.abs(x)))


def _valid_rows(row0, n, front, t_valid):
    rb = row0 + lax.broadcasted_iota(jnp.int32, (n, 1), 0)
    return (rb >= front) & (rb < front + t_valid)


def _layer_norm_rows(x, g_ref, b_ref):
    mu = jnp.mean(x, axis=-1, keepdims=True)
    xc = x - mu
    var = jnp.mean(xc * xc, axis=-1, keepdims=True)
    return xc * lax.rsqrt(var + LN_EPS) * g_ref[...] + b_ref[...]


def _ln_kernel(*refs, alpha, mode, front, t_valid, tpp, tm):
    h_ref = refs[0]
    if mode == "experts":
        y0_ref, y1_ref, rt_ref, g_ref, b_ref, o_ref, ob_ref = refs[1:]
        x = alpha * h_ref[...] + rt_ref[:, 2:3] * y0_ref[...] + rt_ref[:, 3:4] * y1_ref[...]
    elif mode == "router":
        y_ref, g_ref, b_ref, wh_ref, wl_ref, br_ref, o_ref, ob_ref, ro_ref = refs[1:]
        x = alpha * h_ref[...] + y_ref[...].astype(F32)
    else:
        y_ref, g_ref, b_ref, o_ref, ob_ref = refs[1:]
        x = alpha * h_ref[...] + y_ref[...]
    y = _layer_norm_rows(x, g_ref, b_ref)
    row0 = (pl.program_id(0) * tm) % tpp
    y = jnp.where(_valid_rows(row0, tm, front, t_valid), y, 0.0)
    o_ref[...] = y
    yb = y.astype(BF16)
    ob_ref[...] = yb
    if mode == "router":
        yl = (y - yb.astype(F32)).astype(BF16)
        lg = _dot(yb, wh_ref[...]) + _dot(yl, wh_ref[...]) + _dot(yb, wl_ref[...]) + br_ref[...]
        ro_ref[...] = _route(lg)


def _add_ln(h, y, g, b, alpha, lay, route=None, router=None):
    front, t_valid, tpp = lay
    n, d = h.shape
    tm = _divisor_tile(tpp, 128, 16)
    spec = pl.BlockSpec((tm, d), lambda i: (i, 0))
    vec = pl.BlockSpec((1, d), lambda i: (0, 0))
    lane_rows = pl.BlockSpec((tm, LANE), lambda i: (i, 0))
    out_shape = [jax.ShapeDtypeStruct((n, d), F32), jax.ShapeDtypeStruct((n, d), BF16)]
    out_specs = [spec, spec]
    if route is not None:
        mode = "experts"
        args = (h, y, y, route, g.reshape(1, d), b.reshape(1, d))
        in_specs = [spec, spec, pl.BlockSpec((tm, d), lambda i: (i + n // tm, 0)), lane_rows, vec, vec]
    elif router is not None:
        mode = "router"
        wmat = pl.BlockSpec((d, LANE), lambda i: (0, 0))
        args = (h, y, g.reshape(1, d), b.reshape(1, d)) + tuple(router)
        in_specs = [spec, spec, vec, vec, wmat, wmat, pl.BlockSpec((1, LANE), lambda i: (0, 0))]
        out_shape.append(jax.ShapeDtypeStruct((n, LANE), F32))
        out_specs.append(lane_rows)
    else:
        mode = "plain"
        args = (h, y, g.reshape(1, d), b.reshape(1, d))
        in_specs = [spec, spec, vec, vec]
    kern = functools.partial(_ln_kernel, alpha=alpha, mode=mode, front=front, t_valid=t_valid, tpp=tpp, tm=tm)
    return pl.pallas_call(
        kern, out_shape=tuple(out_shape), grid=(n // tm,), in_specs=in_specs, out_specs=tuple(out_specs),
        compiler_params=_cparams(("parallel",)), name="add_ln_" + mode,
    )(*args)


def _embed_ln_kernel(x_ref, meta_ref, g_ref, b_ref, o_ref, ob_ref, *, front, n_meta, n_blocks):
    j = pl.program_id(1)

    @pl.when((j > 0) & (j <= n_blocks))
    def _():
        y = _layer_norm_rows(x_ref[...], g_ref, b_ref)
        o_ref[...] = y
        ob_ref[...] = y.astype(BF16)

    @pl.when(j == 0)
    def _():
        y = _layer_norm_rows(meta_ref[...], g_ref, b_ref)
        o_ref[...] = jnp.zeros_like(o_ref)
        ob_ref[...] = jnp.zeros_like(ob_ref)
        o_ref[front:front + n_meta, :] = y
        ob_ref[front:front + n_meta, :] = y.astype(BF16)

    @pl.when(j > n_blocks)
    def _():
        o_ref[...] = jnp.zeros_like(o_ref)
        ob_ref[...] = jnp.zeros_like(ob_ref)


def _embed_ln(x, meta_tokens, g, b, lay):
    front, t_valid, tpp = lay
    bsz, seq, d = x.shape
    n_meta = meta_tokens.shape[0]
    assert front + n_meta == CHUNK and seq % CHUNK == 0 and tpp == seq + 2 * CHUNK
    nb = seq // CHUNK
    blocks = tpp // CHUNK
    spec = pl.BlockSpec((CHUNK, d), lambda bb, j: (bb * blocks + j, 0))
    vec = pl.BlockSpec((1, d), lambda bb, j: (0, 0))
    return pl.pallas_call(
        functools.partial(_embed_ln_kernel, front=front, n_meta=n_meta, n_blocks=nb),
        out_shape=(jax.ShapeDtypeStruct((bsz * tpp, d), F32), jax.ShapeDtypeStruct((bsz * tpp, d), BF16)),
        grid=(bsz, blocks),
        in_specs=[pl.BlockSpec((None, CHUNK, d), lambda bb, j: (bb, jnp.clip(j - 1, 0, nb - 1), 0)),
                  pl.BlockSpec((n_meta, d), lambda bb, j: (0, 0)), vec, vec],
        out_specs=(spec, spec),
        compiler_params=_cparams(("parallel", "parallel")), name="embed_ln",
    )(x, meta_tokens.astype(x.dtype), g.reshape(1, d), b.reshape(1, d))


def _final_ln_kernel(h_ref, y0_ref, y1_ref, rt_ref, g_ref, b_ref, o_ref, *, alpha):
    x = alpha * h_ref[...] + rt_ref[:, 2:3] * y0_ref[...] + rt_ref[:, 3:4] * y1_ref[...]
    o_ref[...] = _layer_norm_rows(x, g_ref, b_ref)


def _final_ln(h, y2, route, g, b, alpha, lay, bsz, seq):
    front, t_valid, tpp = lay
    n, d = h.shape
    blocks = tpp // CHUNK
    first = (front + t_valid - seq) // CHUNK
    row = lambda bb, j: bb * blocks + first + j
    return pl.pallas_call(
        functools.partial(_final_ln_kernel, alpha=alpha),
        out_shape=jax.ShapeDtypeStruct((bsz, seq, d), F32),
        grid=(bsz, seq // CHUNK),
        in_specs=[pl.BlockSpec((CHUNK, d), lambda bb, j: (row(bb, j), 0)),
                  pl.BlockSpec((CHUNK, d), lambda bb, j: (row(bb, j), 0)),
                  pl.BlockSpec((CHUNK, d), lambda bb, j: (row(bb, j) + n // CHUNK, 0)),
                  pl.BlockSpec((CHUNK, LANE), lambda bb, j: (row(bb, j), 0)),
                  pl.BlockSpec((1, d), lambda bb, j: (0, 0)), pl.BlockSpec((1, d), lambda bb, j: (0, 0))],
        out_specs=pl.BlockSpec((None, CHUNK, d), lambda bb, j: (bb, j, 0)),
        compiler_params=_cparams(("parallel", "parallel")), name="final_ln",
    )(h, y2, y2, route, g.reshape(1, d), b.reshape(1, d))


def _out_proj_kernel(*refs, ks):
    a_refs = refs[:len(ks)]
    w_ref, o_ref, wb_ref = refs[len(ks):]

    @pl.when(pl.program_id(1) == 0)
    def _():
        step = min(w_ref.shape[0], 512)
        for r0 in range(0, w_ref.shape[0], step):
            wb_ref[r0:r0 + step, :] = w_ref[r0:r0 + step, :].astype(BF16)

    acc = None
    off = 0
    for a_ref, k in zip(a_refs, ks):
        p = _dot(a_ref[...], wb_ref[off:off + k, :])
        acc = p if acc is None else acc + p
        off += k
    o_ref[...] = acc.astype(o_ref.dtype)


def _out_proj(a_list, w_all, layer):
    n = a_list[0].shape[0]
    ks = tuple(a.shape[1] for a in a_list)
    _, kdim, ndim = w_all.shape
    assert sum(ks) == kdim
    tm = _divisor_tile(n, 1408, 16)
    tn = _divisor_tile(ndim, 512, LANE)
    in_specs = [pl.BlockSpec((tm, k), lambda j, i: (i, 0)) for k in ks]
    in_specs.append(pl.BlockSpec((None, kdim, tn), lambda j, i: (layer, 0, j)))
    return pl.pallas_call(
        functools.partial(_out_proj_kernel, ks=ks),
        out_shape=jax.ShapeDtypeStruct((n, ndim), BF16),
        grid=(ndim // tn, n // tm),
        in_specs=in_specs,
        out_specs=pl.BlockSpec((tm, tn), lambda j, i: (i, j)),
        scratch_shapes=[pltpu.VMEM((kdim, tn), BF16)],
        compiler_params=_cparams(("parallel", "arbitrary")),
        name="out_proj",
    )(*a_list, w_all)


def _in_proj_kernel(a_ref, w_ref, wn_ref, o_ref, wb_ref, *, n_plain, shift):
    j = pl.program_id(0)
    tn = w_ref.shape[0]
    step = min(tn, LANE)

    @pl.when(pl.program_id(1) == 0)
    def _():
        @pl.when(j < n_plain)
        def _():
            for r0 in range(0, tn, step):
                wb_ref[r0:r0 + step, :] = w_ref[r0:r0 + step, :].astype(BF16)

        @pl.when(j >= n_plain)
        def _():
            for r0 in range(0, tn - step, step):
                wb_ref[r0:r0 + step, :] = w_ref[r0 + shift:r0 + shift + step, :].astype(BF16)
            wb_ref[tn - step:tn - shift, :] = w_ref[tn - step + shift:, :].astype(BF16)
            wb_ref[tn - shift:, :] = wn_ref[:shift, :].astype(BF16)

    o_ref[...] = _dot_nt(a_ref[...], wb_ref[...])


def _small_proj_kernel(a_ref, w_ref, o_ref):
    o_ref[...] = _dot_nt(a_ref[...], w_ref[...].astype(BF16))


def _in_proj(hb, w_in_t, layer, w):
    n, d = hb.shape
    n_small = w_in_t.shape[1] - 15 * w
    tm = _divisor_tile(n, 1408, 16)
    tn = _divisor_tile(w, 512, LANE)
    z = pl.pallas_call(
        functools.partial(_in_proj_kernel, n_plain=4 * w // tn, shift=n_small),
        out_shape=jax.ShapeDtypeStruct((n, 15 * w), F32),
        grid=(15 * w // tn, n // tm),
        in_specs=[pl.BlockSpec((tm, d), lambda j, i: (i, 0)),
                  pl.BlockSpec((None, tn, d), lambda j, i: (layer, j, 0)),
                  pl.BlockSpec((None, LANE, d), lambda j, i: (layer, (j + 1) * (tn // LANE), 0))],
        out_specs=pl.BlockSpec((tm, tn), lambda j, i: (i, j)),
        scratch_shapes=[pltpu.VMEM((tn, d), BF16)],
        compiler_params=_cparams(("parallel", "arbitrary")),
        name="in_proj",
    )(hb, w_in_t, w_in_t)
    zs = pl.pallas_call(
        _small_proj_kernel,
        out_shape=jax.ShapeDtypeStruct((n, LANE), F32),
        grid=(n // tm,),
        in_specs=[pl.BlockSpec((tm, d), lambda i: (i, 0)),
                  pl.BlockSpec((None, LANE, d), lambda i: (layer, 4 * w // LANE, 0))],
        out_specs=pl.BlockSpec((tm, LANE), lambda i: (i, 0)),
        compiler_params=_cparams(("parallel",)),
        name="small_proj",
    )(hb, w_in_t)
    return z, zs


def _tri_masks(n):
    r = lax.broadcasted_iota(jnp.int32, (n, n), 0)
    c = lax.broadcasted_iota(jnp.int32, (n, n), 1)
    return r, c


def _gated_rms(o, w_row, gate):
    o = o * lax.rsqrt(jnp.mean(o * o, axis=-1, keepdims=True) + NORM_EPS)
    if w_row is not None:
        o = o * w_row
    return o * _silu(gate)


def _prefix_rows(x, row, seg):
    pos = row & (seg - 1)
    s = 1
    while s < seg:
        x = x + jnp.where(pos >= s, pltpu.roll(x, s, 0), 0.0)
        s *= 2
    return x


def _mixer_a_kernel(q_ref, k_ref, v_ref, gt_ref, sm_ref, cw_ref, alog_ref, dtb_ref, nw_ref, o_ref,
                    s_ref, prev_ref, *, n_batch, n_heads, front, t_valid):
    c = pl.program_id(0)

    @pl.when(c == 0)
    def _():
        s_ref[...] = jnp.zeros_like(s_ref)
        prev_ref[...] = jnp.zeros_like(prev_ref)

    row = lax.broadcasted_iota(jnp.int32, (CHUNK, 1), 0)
    row8 = lax.broadcasted_iota(jnp.int32, (SUBLANES, 1), 0)
    valid = _valid_rows(c * CHUNK, CHUNK, front, t_valid).astype(F32)
    r, cc = _tri_masks(CHUNK)
    causal = r >= cc
    strict = r > cc
    eye = (r == cc).astype(F32)

    def conv_silu(x_ref, bi, p):
        cur = x_ref[bi]
        prev8 = prev_ref[p, bi]
        acc = cur * cw_ref[p, 3:4, :]
        for s in (1, 2, 3):
            rolled = pltpu.roll(cur, s, 0)
            top = jnp.where(row8 >= s, rolled[:SUBLANES], pltpu.roll(prev8, s, 0))
            acc = acc + jnp.concatenate([top, rolled[SUBLANES:]], axis=0) * cw_ref[p, 3 - s:4 - s, :]
        prev_ref[p, bi] = cur[CHUNK - SUBLANES:]
        return _silu(acc)

    q_all, k_all, v_all, beta_all, gcum_all, gcum_t = [], [], [], [], [], []
    for bi in range(n_batch):
        q_all.append(conv_silu(q_ref, bi, 0))
        k_all.append(conv_silu(k_ref, bi, 1) * valid)
        v_all.append(conv_silu(v_ref, bi, 2) * valid)
        sm = sm_ref[bi]
        beta_all.append(_sigmoid(sm) * valid)
        g_all = -jnp.exp(alog_ref[...]) * _softplus(sm + dtb_ref[...]) * valid
        gcum_all.append(_prefix_rows(g_all, row, CHUNK))
        gcum_t.append(lax.dot_general(gcum_all[bi], eye, (((0,), (0,)), ((), ())), preferred_element_type=F32,
                                      precision=HI))

    chains = [(bi, h) for bi in range(n_batch) for h in range(n_heads)]
    ids = range(len(chains))
    hsl = [slice(h * HEAD_DIM, (h + 1) * HEAD_DIM) for _, h in chains]
    beta = [beta_all[bi][:, h:h + 1] for bi, h in chains]
    gcol = [gcum_all[bi][:, n_heads + h:n_heads + h + 1] for bi, h in chains]
    decay = [jnp.where(causal, jnp.exp(jnp.minimum(gcol[t] - gcum_t[bi][n_heads + h:n_heads + h + 1, :], 0.0)), 0.0)
             for t, (bi, h) in enumerate(chains)]
    eg = [jnp.exp(g) for g in gcol]
    g_last = [g[CHUNK - 1:CHUNK, :] for g in gcol]
    qh = [q_all[bi][:, hsl[t]] for t, (bi, _) in enumerate(chains)]
    kh = [k_all[bi][:, hsl[t]] for t, (bi, _) in enumerate(chains)]
    q = [x * lax.rsqrt(jnp.sum(x * x, axis=-1, keepdims=True) + NORM_EPS) * HEAD_DIM ** -0.5 for x in qh]
    k = [x * lax.rsqrt(jnp.sum(x * x, axis=-1, keepdims=True) + NORM_EPS) for x in kh]
    kb = [k[t] * beta[t] for t in ids]
    k16 = [x.astype(BF16) for x in k]
    m = [jnp.where(strict, -(_dot_nt(kb[t].astype(BF16), k16[t]) * decay[t]), 0.0) for t in ids]
    attn = [_dot_nt(q[t].astype(BF16), k16[t]) * decay[t] for t in ids]
    inv = [eye + x for x in m]
    mp = [_dot(x.astype(BF16), x.astype(BF16)) for x in m]
    for _ in range(4):
        both = [_dot(jnp.concatenate([inv[t], mp[t]], axis=0).astype(BF16), mp[t].astype(BF16)) for t in ids]
        inv = [inv[t] + both[t][:CHUNK] for t in ids]
        mp = [both[t][CHUNK:] for t in ids]
    inv = [inv[t] + _dot(inv[t].astype(BF16), mp[t].astype(BF16)) for t in ids]
    sol = [_dot(inv[t].astype(BF16),
                jnp.concatenate([v_all[bi][:, hsl[t]] * beta[t], kb[t] * eg[t]], axis=1).astype(BF16))
           for t, (bi, _) in enumerate(chains)]
    s_old = [s_ref[t] for t in ids]
    s16 = [x.astype(BF16) for x in s_old]
    v_new = [sol[t][:, :HEAD_DIM] - _dot(sol[t][:, HEAD_DIM:].astype(BF16), s16[t]) for t in ids]
    o = [_dot((q[t] * eg[t]).astype(BF16), s16[t]) + _dot(attn[t].astype(BF16), v_new[t].astype(BF16)) for t in ids]
    for t, (bi, _) in enumerate(chains):
        s_ref[t] = s_old[t] * jnp.exp(g_last[t]) + _dot_tn(k[t] * jnp.exp(g_last[t] - gcol[t]), v_new[t])
        o_ref[bi, :, hsl[t]] = _gated_rms(o[t], nw_ref[...], gt_ref[bi, :, hsl[t]]).astype(o_ref.dtype)


def _mixer_b_kernel(q_ref, f_ref, i_ref, gt_ref, lb_ref, nw_ref, o_ref, st_ref, kbuf, bbuf, obuf,
                    *, n_batch, n_heads, front, t_valid):
    c = pl.program_id(0)

    @pl.when(c == 0)
    def _():
        st_ref[...] = jnp.zeros_like(st_ref)

    row = lax.broadcasted_iota(jnp.int32, (CHUNK, 1), 0)
    valid = _valid_rows(c * CHUNK, CHUNK, front, t_valid).astype(F32)
    row16 = lax.broadcasted_iota(jnp.int32, (SUB, 1), 0)

    lb = lb_ref[...]
    for bi in range(n_batch):
        f = lb + (1.0 - lb) * _sigmoid(f_ref[bi])
        kbuf[bi] = (1.0 - f) * valid
        bbuf[bi] = _prefix_rows(jnp.log(f) * valid, row, SUB)

    chains = [(bi, h) for bi in range(n_batch) for h in range(n_heads)]
    hsl = [slice(h * HEAD_DIM, (h + 1) * HEAD_DIM) for _, h in chains]
    st = [st_ref[t] for t in range(len(chains))]
    for blk in range(CHUNK // SUB):
        rs = slice(blk * SUB, (blk + 1) * SUB)
        for t, (bi, _) in enumerate(chains):
            hs = hsl[t]
            b16 = bbuf[bi, rs, hs]
            q16 = _silu(q_ref[bi, rs, hs])
            b_end = b16[SUB - 1:SUB, :]
            acc = _dot_nt((q16 * jnp.exp(b16)).astype(BF16), st[t].astype(BF16))
            parts = []
            for t0 in range(0, SUB, SUBLANES):
                ts = slice(t0, t0 + SUBLANES)
                bq, qq, rowq = b16[ts], q16[ts], row16[ts]
                part = acc[ts]
                for j in range(min(t0 + SUBLANES, SUB)):
                    jr = pl.ds(blk * SUB + j, 1)
                    sc = jnp.sum(qq * kbuf[bi, jr, hs] * jnp.exp(bq - bbuf[bi, jr, hs]), axis=-1, keepdims=True)
                    if j > t0:
                        sc = jnp.where(rowq >= j, sc, 0.0)
                    part = part + sc * i_ref[bi, jr, hs]
                parts.append(part)
            acc = jnp.concatenate(parts, axis=0)
            st[t] = st[t] * jnp.exp(b_end) + _dot_tn(i_ref[bi, rs, hs], kbuf[bi, rs, hs] * jnp.exp(b_end - b16))
            obuf[bi, rs, hs] = acc
    for t, (bi, _) in enumerate(chains):
        st_ref[t] = st[t]
        o_ref[bi, :, hsl[t]] = _gated_rms(obuf[bi, :, hsl[t]], nw_ref[...], gt_ref[bi, :, hsl[t]]).astype(o_ref.dtype)


def _rope_partial(x, cos, sin_lo, sin_hi):
    half = ROPE_DIMS // 2
    return x * cos + pltpu.roll(x, LANE - half, 1) * sin_lo + pltpu.roll(x, half, 1) * sin_hi


def _rope_c_kernel(q_ref, k_ref, v_ref, cos_ref, slo_ref, shi_ref, qo_ref, ko_ref, vo_ref, *, n_maps):
    cos, slo, shi = cos_ref[...], slo_ref[...], shi_ref[...]
    for mp in range(n_maps):
        ms = slice(mp * HEAD_DIM, (mp + 1) * HEAD_DIM)
        qo_ref[:, ms] = (_rope_partial(q_ref[:, ms], cos, slo, shi) * HEAD_DIM ** -0.5).astype(BF16)
        ko_ref[:, ms] = _rope_partial(k_ref[:, ms], cos, slo, shi).astype(BF16)
    vo_ref[...] = v_ref[...].astype(BF16)


def _mixer_c_kernel(qi_ref, kj_ref, q_ref, k_ref, v_ref, lam_ref, nw_ref, o_ref, m_ref, l_ref, acc_ref,
                    *, tq, front, out_scale, n_heads):
    p = pl.program_id(1)
    i = qi_ref[p]
    j = kj_ref[p]
    vw = 2 * HEAD_DIM

    @pl.when(j == 0)
    def _():
        m_ref[...] = jnp.full_like(m_ref, NEG)
        l_ref[...] = jnp.zeros_like(l_ref)
        acc_ref[...] = jnp.zeros_like(acc_ref)

    def step(masked):
        if masked:
            rq = i * tq + lax.broadcasted_iota(jnp.int32, (tq, 1), 0)
            rk = j * tq + lax.broadcasted_iota(jnp.int32, (1, tq), 1)
            msk = (rk <= rq) & (rk >= front)
        for hd in range(n_heads):
            v = v_ref[:, hd * vw:(hd + 1) * vw]
            rows = [2 * hd, 2 * hd + 1]
            msl = [slice(r * HEAD_DIM, (r + 1) * HEAD_DIM) for r in rows]
            s = [_dot_nt(q_ref[:, ms], k_ref[:, ms]) for ms in msl]
            if masked:
                s = [jnp.where(msk, x, NEG) for x in s]
            m_prev = [m_ref[r] for r in rows]
            m_new = [jnp.maximum(m_prev[t], jnp.max(s[t], axis=-1, keepdims=True)) for t in range(2)]
            pr = [jnp.exp(s[t] - jnp.tile(m_new[t], (1, tq // LANE))) for t in range(2)]
            pv = [_dot(pr[t].astype(BF16), v) for t in range(2)]
            for t, r in enumerate(rows):
                a = jnp.exp(m_prev[t] - m_new[t])
                l_ref[r] = a * l_ref[r] + jnp.sum(pr[t], axis=-1, keepdims=True)
                acc_ref[r] = jnp.tile(a, (1, vw // LANE)) * acc_ref[r] + pv[t]
                m_ref[r] = m_new[t]

    needs_mask = (j == i) | (j == 0)

    @pl.when(needs_mask)
    def _():
        step(True)

    @pl.when(jnp.logical_not(needs_mask))
    def _():
        step(False)

    @pl.when(j == i)
    def _():
        reps = (1, vw // LANE)
        for hd in range(n_heads):
            r0, r1 = 2 * hd, 2 * hd + 1
            o = acc_ref[r0] / jnp.tile(l_ref[r0], reps) - lam_ref[...] * (acc_ref[r1] / jnp.tile(l_ref[r1], reps))
            o = o * lax.rsqrt(jnp.mean(o * o, axis=-1, keepdims=True) + NORM_EPS) * nw_ref[...] * out_scale
            o_ref[:, hd * vw:(hd + 1) * vw] = o.astype(o_ref.dtype)


def _mixer_d_kernel(q_ref, k_ref, v_ref, gt_ref, cos_ref, sin_ref, dm_ref, qd_ref, kd_ref, cd_ref, o_ref, s_ref,
                    *, n_batch, n_heads):
    c = pl.program_id(0)

    @pl.when(c == 0)
    def _():
        s_ref[...] = jnp.zeros_like(s_ref)

    cos = cos_ref[...]
    sin = sin_ref[...]
    chains = [(bi, h) for bi in range(n_batch) for h in range(n_heads)]
    ids = range(len(chains))
    hsl = [slice(h * HEAD_DIM, (h + 1) * HEAD_DIM) for _, h in chains]
    q = [q_ref[bi, :, hsl[t]] * cos + pltpu.roll(q_ref[bi, :, hsl[t]], HEAD_DIM // 2, 1) * sin
         for t, (bi, _) in enumerate(chains)]
    k = [(k_ref[bi, :, hsl[t]] * cos + pltpu.roll(k_ref[bi, :, hsl[t]], HEAD_DIM // 2, 1) * sin) * HEAD_DIM ** -0.5
         for t, (bi, _) in enumerate(chains)]
    s_old = [s_ref[t] for t in ids]
    a = [_dot_nt(q[t].astype(BF16), k[t].astype(BF16)) * dm_ref[h] for t, (_, h) in enumerate(chains)]
    o = [_dot(a[t].astype(BF16), v_ref[bi, :, hsl[t]].astype(BF16))
         + _dot((q[t] * qd_ref[h]).astype(BF16), s_old[t].astype(BF16)) for t, (bi, h) in enumerate(chains)]
    for t, (bi, h) in enumerate(chains):
        s_ref[t] = cd_ref[h] * s_old[t] + _dot_tn(k[t] * kd_ref[h], v_ref[bi, :, hsl[t]])
        o_ref[bi, :, hsl[t]] = _gated_rms(o[t], None, gt_ref[bi, :, hsl[t]]).astype(o_ref.dtype)


def _route(lg):
    n_e = N_GROUPS * EXP_PER_GROUP
    lane_i = lax.broadcasted_iota(jnp.int32, lg.shape, 1)
    lane = lane_i.astype(F32)
    grp = (lane_i >> GROUP_SHIFT).astype(F32)
    big = float(4 * LANE)
    is_g = (lane_i >= n_e) & (lane_i < n_e + N_GROUPS)
    gl = jnp.where(is_g, lg, NEG)
    gmax = jnp.max(gl, axis=-1, keepdims=True)
    gsum = jnp.sum(jnp.where(is_g, jnp.exp(gl - gmax), 0.0), axis=-1, keepdims=True)
    g_w = 1.0 / gsum
    g_idx = jnp.min(jnp.where(is_g & (gl == gmax), lane - n_e, big), axis=-1, keepdims=True)
    in_grp = (lane_i < n_e) & (grp == g_idx)
    el = jnp.where(in_grp, lg, NEG)
    emax = jnp.max(el, axis=-1, keepdims=True)
    eexp = jnp.where(in_grp, jnp.exp(el - emax), 0.0)
    pe = eexp / jnp.sum(eexp, axis=-1, keepdims=True)
    p1 = jnp.max(jnp.where(in_grp, pe, -1.0), axis=-1, keepdims=True)
    i1 = jnp.min(jnp.where(in_grp & (pe == p1), lane, big), axis=-1, keepdims=True)
    rest = in_grp & (lane != i1)
    p2 = jnp.max(jnp.where(rest, pe, -1.0), axis=-1, keepdims=True)
    i2 = jnp.min(jnp.where(rest & (pe == p2), lane, big), axis=-1, keepdims=True)
    den = p1 + p2
    return (jnp.where(lane_i == 0, i1, 0.0) + jnp.where(lane_i == 1, i2, 0.0)
            + jnp.where(lane_i == 2, g_w * (p1 / den), 0.0) + jnp.where(lane_i == 3, g_w * (p2 / den), 0.0))


def _row_copy(src, src_row, dst, dst_row, sem):
    return pltpu.make_async_copy(src.at[pl.ds(src_row, 1)], dst.at[pl.ds(dst_row, 1)], sem)


def _moe_kernel(te_ref, cnt_ref, first_ref, src_ref, srcn_ref, dst_ref, x_hbm, wg_ref, wu_ref, wd_ref, y_hbm,
                xbuf, ybuf, wgb, wub, wdb, sem_in, sem_out, *, tm, n_tiles):
    del te_ref
    t = pl.program_id(0)
    slot = t % 2
    cnt = cnt_ref[t]

    def pairs(rows):
        return (rows + 1) // 2

    def gather_start(idx_ref, s, rows):
        def body(g, c):
            for u in range(2):
                r = 2 * g + u
                _row_copy(x_hbm, idx_ref[0, 0, r], xbuf.at[s], r, sem_in.at[s]).start(priority=u)
            return c
        lax.fori_loop(0, pairs(rows), body, 0)

    def wait_rows(src, dst, sem, rows):
        def tile_body(g, c):
            pltpu.make_async_copy(src.at[pl.ds(0, SUBLANES)], dst.at[pl.ds(g * SUBLANES, SUBLANES)], sem).wait()
            return c
        lax.fori_loop(0, rows // SUBLANES, tile_body, 0)

        def row_body(r, c):
            _row_copy(src, 0, dst, r, sem).wait()
            return c
        lax.fori_loop(rows - rows % SUBLANES, rows, row_body, 0)

    def gather_wait(s, rows):
        wait_rows(x_hbm, xbuf.at[s], sem_in.at[s], 2 * pairs(rows))

    def scatter_start(s, rows):
        def body(g, c):
            for u in range(2):
                r = 2 * g + u
                _row_copy(ybuf.at[s], r, y_hbm, dst_ref[0, 0, r], sem_out.at[s]).start(priority=u)
            return c
        lax.fori_loop(0, rows // 2, body, 0)

        @pl.when(rows % 2 == 1)
        def _():
            _row_copy(ybuf.at[s], rows - 1, y_hbm, dst_ref[0, 0, rows - 1], sem_out.at[s]).start()

    def scatter_wait(s, rows):
        wait_rows(ybuf.at[s], y_hbm, sem_out.at[s], rows)

    @pl.when(t == 0)
    def _():
        xbuf[...] = jnp.zeros_like(xbuf)

        @pl.when(cnt > 0)
        def _():
            gather_start(src_ref, 0, cnt)

    nxt = jnp.minimum(t + 1, n_tiles - 1)

    @pl.when((t + 1 < n_tiles) & (cnt_ref[nxt] > 0))
    def _():
        gather_start(srcn_ref, 1 - slot, cnt_ref[nxt])

    old = jnp.maximum(t - 2, 0)

    @pl.when((t >= 2) & (cnt_ref[old] > 0))
    def _():
        scatter_wait(slot, cnt_ref[old])

    @pl.when(cnt > 0)
    def _():
        @pl.when(first_ref[t] == 1)
        def _():
            wgb[...] = wg_ref[...].astype(BF16)
            wub[...] = wu_ref[...].astype(BF16)
            wdb[...] = wd_ref[...].astype(BF16)

        gather_wait(slot, cnt)
        x = xbuf[slot].astype(BF16)
        hg = _dot(x, wgb[...])
        hu = _dot(x, wub[...])
        act = (_silu(hg) * hu).astype(BF16)
        ybuf[slot] = _dot(act, wdb[...])
        scatter_start(slot, cnt)

    @pl.when(t == n_tiles - 1)
    def _():
        prev = jnp.maximum(t - 1, 0)

        @pl.when((t >= 1) & (cnt_ref[prev] > 0))
        def _():
            scatter_wait(1 - slot, cnt_ref[prev])

        @pl.when(cnt > 0)
        def _():
            scatter_wait(slot, cnt)


def _dispatch(route, n, tm, n_e):
    ids = jnp.concatenate([route[:, 0], route[:, 1]]).astype(jnp.int32)
    n_pairs = 2 * n
    n_tiles = n_pairs // tm + n_e
    order = jnp.argsort(ids, stable=True).astype(jnp.int32)
    counts = jnp.sum((ids[:, None] == jnp.arange(n_e, dtype=jnp.int32)[None, :]).astype(jnp.int32), axis=0)
    tiles_e = (counts + tm - 1) // tm
    tile_end = jnp.cumsum(tiles_e)
    tile_start = tile_end - tiles_e
    first = jnp.cumsum(counts) - counts
    tile = jnp.arange(n_tiles, dtype=jnp.int32)
    te = jnp.minimum(jnp.searchsorted(tile_end, tile, side="right").astype(jnp.int32), n_e - 1)
    cnt = jnp.clip(counts[te] - (tile - tile_start[te]) * tm, 0, tm)
    cnt = jnp.where(tile < tile_end[n_e - 1], cnt, 0).astype(jnp.int32)
    row = jnp.arange(tm, dtype=jnp.int32)[None, :]
    spos = (first[te] + (tile - tile_start[te]) * tm)[:, None] + row
    dst = jnp.where(row < cnt[:, None], order[jnp.clip(spos, 0, n_pairs - 1)], 0)
    src = dst % n
    te = jnp.where(cnt > 0, te, jnp.max(jnp.where(cnt > 0, te, 0)))
    new_expert = jnp.concatenate([jnp.ones((1,), jnp.int32), (te[1:] != te[:-1]).astype(jnp.int32)])
    return te, cnt, new_expert, src.reshape(n_tiles, 1, tm), dst.reshape(n_tiles, 1, tm)


def _moe_routed(h, route, wg, wu, wd, layer, tm):
    n, d = h.shape
    _, n_e, _, f = wg.shape
    te, cnt, new_expert, src, dst = _dispatch(route, n, tm, n_e)
    n_tiles = te.shape[0]

    def idx_spec(shift):
        return pl.BlockSpec((1, 1, tm), lambda t, *_: (jnp.minimum(t + shift, n_tiles - 1), 0, 0),
                            memory_space=pltpu.SMEM)

    grid_spec = pltpu.PrefetchScalarGridSpec(
        num_scalar_prefetch=3, grid=(n_tiles,),
        in_specs=[idx_spec(0), idx_spec(1), idx_spec(0),
                  pl.BlockSpec(memory_space=pl.ANY),
                  pl.BlockSpec((None, None, d, f), lambda t, te, *_: (layer, te[t], 0, 0)),
                  pl.BlockSpec((None, None, d, f), lambda t, te, *_: (layer, te[t], 0, 0)),
                  pl.BlockSpec((None, None, f, d), lambda t, te, *_: (layer, te[t], 0, 0))],
        out_specs=pl.BlockSpec(memory_space=pl.ANY),
        scratch_shapes=[pltpu.VMEM((2, tm, d), F32), pltpu.VMEM((2, tm, d), F32),
                        pltpu.VMEM((d, f), BF16), pltpu.VMEM((d, f), BF16), pltpu.VMEM((f, d), BF16),
                        pltpu.SemaphoreType.DMA((2,)), pltpu.SemaphoreType.DMA((2,))])
    return pl.pallas_call(
        functools.partial(_moe_kernel, tm=tm, n_tiles=n_tiles),
        out_shape=jax.ShapeDtypeStruct((2 * n, d), F32),
        grid_spec=grid_spec,
        compiler_params=_cparams(("arbitrary",)),
        name="moe_routed",
    )(te, cnt, new_expert, src, src, dst, h, wg, wu, wd)


def _mixers(z, zs, lay, bsz, w, prm):
    front, t_valid, tpp = lay
    n = z.shape[0]
    nc = tpp // CHUNK
    heads = w // HEAD_DIM
    c_heads = w // (2 * HEAD_DIM)
    z3 = z.reshape(bsz, tpp, z.shape[1])
    zs3 = zs.reshape(bsz, tpp, LANE)

    def cspec(region):
        return pl.BlockSpec((bsz, CHUNK, w), lambda c, region=region: (0, c, region))

    def fixed(shape):
        nd = len(shape)
        return pl.BlockSpec(shape, lambda c: (0,) * nd)

    ospec = pl.BlockSpec((bsz, CHUNK, w), lambda c: (0, c, 0))
    oshape3 = jax.ShapeDtypeStruct((bsz, tpp, w), BF16)
    oshape = jax.ShapeDtypeStruct((n, w), BF16)
    grid = (nc,)
    sem = ("arbitrary",)
    state = pltpu.VMEM((bsz * heads, HEAD_DIM, HEAD_DIM), F32)

    oa = pl.pallas_call(
        functools.partial(_mixer_a_kernel, n_batch=bsz, n_heads=heads, front=front, t_valid=t_valid),
        out_shape=oshape3, grid=grid,
        in_specs=[cspec(0), cspec(1), cspec(2), cspec(3),
                  pl.BlockSpec((bsz, CHUNK, LANE), lambda c: (0, c, 0)),
                  fixed((3, 4, w)), fixed((1, LANE)), fixed((1, LANE)), fixed((1, HEAD_DIM))],
        out_specs=ospec,
        scratch_shapes=[state, pltpu.VMEM((3, bsz, SUBLANES, w), F32)],
        compiler_params=_cparams(sem), name="mixer_a",
    )(z3, z3, z3, z3, zs3, prm["conv_w"], prm["a_log_row"], prm["dt_bias_row"], prm["norm_a"]).reshape(n, w)

    ob = pl.pallas_call(
        functools.partial(_mixer_b_kernel, n_batch=bsz, n_heads=heads, front=front, t_valid=t_valid),
        out_shape=oshape3, grid=grid,
        in_specs=[cspec(4), cspec(5), cspec(6), cspec(7), fixed((1, w)), fixed((1, HEAD_DIM))],
        out_specs=ospec,
        scratch_shapes=[state] + [pltpu.VMEM((bsz, CHUNK, w), F32)] * 3,
        compiler_params=_cparams(sem), name="mixer_b",
    )(z3, z3, z3, z3, prm["lb"], prm["norm_b"]).reshape(n, w)

    tq = _divisor_tile(tpp, 384, LANE)
    nq = tpp // tq
    vw = 2 * HEAD_DIM
    rc = prm["rope_c"]
    rspec = pl.BlockSpec((tq, w), lambda i: (i, 0))
    tab = pl.BlockSpec((tq, HEAD_DIM), lambda i: (i % nq, 0))
    qr, kr, vr = pl.pallas_call(
        functools.partial(_rope_c_kernel, n_maps=heads),
        out_shape=(oshape, oshape, oshape), grid=(n // tq,),
        in_specs=[pl.BlockSpec((tq, w), lambda i: (i, 8)), pl.BlockSpec((tq, w), lambda i: (i, 9)),
                  pl.BlockSpec((tq, w), lambda i: (i, 10)), tab, tab, tab],
        out_specs=(rspec, rspec, rspec),
        compiler_params=_cparams(("parallel",)), name="rope_c",
    )(z, z, z, rc[0], rc[1], rc[2])

    pairs = [(i, j) for i in range(nq) for j in range(i + 1)]
    qi = jnp.asarray([p[0] for p in pairs], jnp.int32)
    kj = jnp.asarray([p[1] for p in pairs], jnp.int32)
    qspec = pl.BlockSpec((tq, w), lambda b, p, qi, kj: (b * nq + qi[p], 0))
    kspec = pl.BlockSpec((tq, w), lambda b, p, qi, kj: (b * nq + kj[p], 0))
    row_vw = pl.BlockSpec((1, vw), lambda b, p, qi, kj: (0, 0))
    oc = pl.pallas_call(
        functools.partial(_mixer_c_kernel, tq=tq, front=front, out_scale=prm["c_out_scale"], n_heads=c_heads),
        out_shape=oshape,
        grid_spec=pltpu.PrefetchScalarGridSpec(
            num_scalar_prefetch=2, grid=(bsz, len(pairs)),
            in_specs=[qspec, kspec, kspec, row_vw, row_vw],
            out_specs=qspec,
            scratch_shapes=[pltpu.VMEM((2 * c_heads, tq, LANE), F32), pltpu.VMEM((2 * c_heads, tq, LANE), F32),
                            pltpu.VMEM((2 * c_heads, tq, vw), F32)]),
        compiler_params=_cparams(("parallel", "arbitrary")), name="mixer_c",
    )(qi, kj, qr, kr, vr, prm["lam_row"], prm["subln_c"])

    tabd = pl.BlockSpec((CHUNK, HEAD_DIM), lambda c: (c, 0))
    rd = prm["rope_d"]
    od = pl.pallas_call(
        functools.partial(_mixer_d_kernel, n_batch=bsz, n_heads=heads),
        out_shape=oshape3, grid=grid,
        in_specs=[cspec(11), cspec(12), cspec(13), cspec(14), tabd, tabd,
                  fixed((heads, CHUNK, CHUNK)), fixed((heads, CHUNK, HEAD_DIM)), fixed((heads, CHUNK, HEAD_DIM)),
                  fixed((heads, 1, HEAD_DIM))],
        out_specs=ospec,
        scratch_shapes=[state],
        compiler_params=_cparams(sem), name="mixer_d",
    )(z3, z3, z3, z3, rd[0], rd[1], prm["ret_dmask"], prm["ret_qdec"], prm["ret_kdec"], prm["ret_cdec"]).reshape(n, w)
    return oa, ob, oc, od


def _rope_tables(front, tpp):
    pos = (jnp.arange(tpp) - front).astype(F32)[:, None]
    half = ROPE_DIMS // 2
    inv = 1.0 / (ROPE_THETA ** (jnp.arange(half, dtype=F32) / half))
    ang = pos * inv[None, :]
    z_rest = jnp.zeros((tpp, HEAD_DIM - ROPE_DIMS), F32)
    z_half = jnp.zeros((tpp, half), F32)
    cos_c = jnp.concatenate([jnp.cos(ang), jnp.cos(ang), jnp.ones_like(z_rest)], axis=1)
    sin_lo = jnp.concatenate([-jnp.sin(ang), z_half, z_rest], axis=1)
    sin_hi = jnp.concatenate([z_half, jnp.sin(ang), z_rest], axis=1)
    half_d = HEAD_DIM // 2
    inv_d = 1.0 / (RET_THETA ** (jnp.arange(half_d, dtype=F32) / half_d))
    ang_d = pos * inv_d[None, :]
    cos_d = jnp.concatenate([jnp.cos(ang_d), jnp.cos(ang_d)], axis=1)
    sin_d = jnp.concatenate([-jnp.sin(ang_d), jnp.sin(ang_d)], axis=1)
    return (cos_c, sin_lo, sin_hi), (cos_d, sin_d)


def _retention_tables(n_heads):
    lg = jnp.log(1.0 - 2.0 ** (-5.0 - jnp.arange(n_heads, dtype=F32)))
    idx = jnp.arange(CHUNK, dtype=F32)
    rel = idx[:, None] - idx[None, :]
    causal = idx[:, None] >= idx[None, :]
    dmask = jnp.exp(jnp.where(causal[None], rel[None] * lg[:, None, None], -jnp.inf))
    qdec = jnp.exp((idx[None, :] + 1.0) * lg[:, None])
    kdec = jnp.exp((CHUNK - 1.0 - idx[None, :]) * lg[:, None])
    cdec = jnp.exp(CHUNK * lg)
    bc = lambda t: jnp.broadcast_to(t[..., None], t.shape + (HEAD_DIM,))
    return dmask, bc(qdec), bc(kdec), bc(cdec[:, None])


def _row(v, width=LANE, offset=0):
    out = jnp.zeros((1, width), F32)
    return out.at[0, offset:offset + v.shape[0]].set(v.astype(F32))


def kernel(x, meta_tokens, emb_ln_g, emb_ln_b, w_in, conv_a, a_log, dt_bias, norm_a, hgrn_lb, norm_b, lam_q1, lam_k1,
           lam_q2, lam_k2, subln_c, w_out, ln1_g, ln1_b, w_rg, b_rg, w_re, b_re, w_gate, w_up, w_down, ln2_g, ln2_b):
    bsz, seq, d = x.shape
    depth = w_in.shape[0]
    n_meta = meta_tokens.shape[0]
    w = d // 4
    a_heads = w // HEAD_DIM
    t_valid = n_meta + seq
    front = (-t_valid) % CHUNK
    tpp = -(-(front + t_valid) // ROW_ALIGN) * ROW_ALIGN
    lay = (front, t_valid, tpp)
    alpha = (2 * depth) ** 0.25

    h, hb = _embed_ln(x, meta_tokens, emb_ln_g, emb_ln_b, lay)

    lbs = jax.nn.softmax(hgrn_lb.astype(F32), axis=0)
    lbs = jnp.cumsum(lbs, axis=0) - lbs[0:1]
    rope_c, rope_d = _rope_tables(front, tpp)
    dmask, qdec, kdec, cdec = _retention_tables(a_heads)

    w_in_t = jnp.swapaxes(w_in, 1, 2)

    for l in range(depth):
        lam_init = 0.8 - 0.6 * math.exp(-0.3 * l)
        lam = (jnp.exp(jnp.sum(lam_q1[l].astype(F32) * lam_k1[l].astype(F32)))
               - jnp.exp(jnp.sum(lam_q2[l].astype(F32) * lam_k2[l].astype(F32))) + lam_init)
        prm = {
            "conv_w": conv_a[l].astype(F32).reshape(3, w, -1).transpose(0, 2, 1),
            "a_log_row": _row(a_log[l], offset=a_heads),
            "dt_bias_row": _row(dt_bias[l], offset=a_heads),
            "norm_a": norm_a[l].astype(F32).reshape(1, HEAD_DIM),
            "lb": lbs[l].reshape(1, w),
            "norm_b": norm_b[l].astype(F32).reshape(1, HEAD_DIM),
            "rope_c": rope_c, "rope_d": rope_d,
            "lam_row": jnp.full((1, 2 * HEAD_DIM), lam, F32),
            "subln_c": subln_c[l].astype(F32).reshape(1, 2 * HEAD_DIM),
            "c_out_scale": 1.0 - lam_init,
            "ret_dmask": dmask, "ret_qdec": qdec, "ret_kdec": kdec, "ret_cdec": cdec,
        }
        z, zs = _in_proj(hb, w_in_t, l, w)
        o_parts = _mixers(z, zs, lay, bsz, w, prm)
        y = _out_proj(list(o_parts), w_out, l)
        n_e = w_re.shape[2]
        w_r = jnp.concatenate([w_re[l], w_rg[l], jnp.zeros((d, LANE - n_e - N_GROUPS), F32)], axis=1).astype(F32)
        b_r = jnp.concatenate([b_re[l], b_rg[l], jnp.zeros((LANE - n_e - N_GROUPS,), F32)]).reshape(1, LANE)
        w_r_hi = w_r.astype(BF16)
        w_r_lo = (w_r - w_r_hi.astype(F32)).astype(BF16)
        h, hb, route = _add_ln(h, y, ln1_g[l], ln1_b[l], alpha, lay, router=(w_r_hi, w_r_lo, b_r))
        y2 = _moe_routed(h, route, w_gate, w_up, w_down, l, MOE_TILE)
        if l + 1 < depth:
            h, hb = _add_ln(h, y2, ln2_g[l], ln2_b[l], alpha, lay, route=route)

    return _final_ln(h, y2, route, ln2_g[depth - 1], ln2_b[depth - 1], alpha, lay, bsz, seq)
```

```python
import functools
import math

import jax
import jax.numpy as jnp
from jax import lax
from jax.experimental import pallas as pl
from jax.experimental.pallas import tpu as pltpu

HEAD_DIM = 128
CHUNK = 64
SUB = 16
GROUP_SHIFT = 3
ROPE_THETA = 500000.0
ROPE_DIMS = HEAD_DIM // 4
RET_THETA = 10000.0
N_GROUPS = 8
EXP_PER_GROUP = 8
NORM_EPS = 1e-6
LN_EPS = 1e-5
LANE = 128
SUBLANES = 8
ROW_ALIGN = 2 * CHUNK
V7X_VMEM_LIMIT_BYTES = 56 * 1024 * 1024
MOE_TILE = 256
NEG = -1e30

F32 = jnp.float32
BF16 = jnp.bfloat16
HI = lax.Precision.HIGHEST


def _cparams(sem):
    return pltpu.CompilerParams(dimension_semantics=sem, vmem_limit_bytes=V7X_VMEM_LIMIT_BYTES)


def _divisor_tile(n, cap, align):
    best = None
    for t in range(align, min(n, cap) + 1, align):
        if n % t == 0:
            best = t
    assert best is not None, (n, cap, align)
    return best


def _dot(a, b):
    return jnp.dot(a, b, preferred_element_type=F32)


def _dot_nt(a, b):
    return lax.dot_general(a, b, (((1,), (1,)), ((), ())), preferred_element_type=F32)


def _dot_tn(a, b):
    return lax.dot_general(a, b, (((0,), (0,)), ((), ())), preferred_element_type=F32)


def _sigmoid(x):
    return 1.0 / (1.0 + jnp.exp(-x))


def _silu(x):
    return x * _sigmoid(x)


def _softplus(x):
    return jnp.maximum(x, 0.0) + jnp.log(1.0 + jnp.exp(-jnp.abs(x)))


def _valid_rows(row0, n, front, t_valid):
    rb = row0 + lax.broadcasted_iota(jnp.int32, (n, 1), 0)
    return (rb >= front) & (rb < front + t_valid)


def _layer_norm_rows(x, g_ref, b_ref):
    mu = jnp.mean(x, axis=-1, keepdims=True)
    xc = x - mu
    var = jnp.mean(xc * xc, axis=-1, keepdims=True)
    return xc * lax.rsqrt(var + LN_EPS) * g_ref[...] + b_ref[...]


def _ln_kernel(*refs, alpha, mode, front, t_valid, tpp, tm):
    h_ref = refs[0]
    if mode == "experts":
        y0_ref, y1_ref, rt_ref, g_ref, b_ref, o_ref, ob_ref = refs[1:]
        x = alpha * h_ref[...] + rt_ref[:, 2:3] * y0_ref[...] + rt_ref[:, 3:4] * y1_ref[...]
    elif mode == "router":
        y_ref, g_ref, b_ref, wh_ref, wl_ref, br_ref, o_ref, ob_ref, ro_ref = refs[1:]
        x = alpha * h_ref[...] + y_ref[...].astype(F32)
    else:
        y_ref, g_ref, b_ref, o_ref, ob_ref = refs[1:]
        x = alpha * h_ref[...] + y_ref[...]
    y = _layer_norm_rows(x, g_ref, b_ref)
    row0 = (pl.program_id(0) * tm) % tpp
    y = jnp.where(_valid_rows(row0, tm, front, t_valid), y, 0.0)
    o_ref[...] = y
    yb = y.astype(BF16)
    ob_ref[...] = yb
    if mode == "router":
        yl = (y - yb.astype(F32)).astype(BF16)
        lg = _dot(yb, wh_ref[...]) + _dot(yl, wh_ref[...]) + _dot(yb, wl_ref[...]) + br_ref[...]
        ro_ref[...] = _route(lg)


def _add_ln(h, y, g, b, alpha, lay, route=None, router=None):
    front, t_valid, tpp = lay
    n, d = h.shape
    tm = _divisor_tile(tpp, 192, 16)
    spec = pl.BlockSpec((tm, d), lambda i: (i, 0))
    vec = pl.BlockSpec((1, d), lambda i: (0, 0))
    lane_rows = pl.BlockSpec((tm, LANE), lambda i: (i, 0))
    out_shape = [jax.ShapeDtypeStruct((n, d), F32), jax.ShapeDtypeStruct((n, d), BF16)]
    out_specs = [spec, spec]
    if route is not None:
        mode = "experts"
        args = (h, y, y, route, g.reshape(1, d), b.reshape(1, d))
        in_specs = [spec, spec, pl.BlockSpec((tm, d), lambda i: (i + n // tm, 0)), lane_rows, vec, vec]
    elif router is not None:
        mode = "router"
        wmat = pl.BlockSpec((d, LANE), lambda i: (0, 0))
        args = (h, y, g.reshape(1, d), b.reshape(1, d)) + tuple(router)
        in_specs = [spec, spec, vec, vec, wmat, wmat, pl.BlockSpec((1, LANE), lambda i: (0, 0))]
        out_shape.append(jax.ShapeDtypeStruct((n, LANE), F32))
        out_specs.append(lane_rows)
    else:
        mode = "plain"
        args = (h, y, g.reshape(1, d), b.reshape(1, d))
        in_specs = [spec, spec, vec, vec]
    kern = functools.partial(_ln_kernel, alpha=alpha, mode=mode, front=front, t_valid=t_valid, tpp=tpp, tm=tm)
    return pl.pallas_call(
        kern, out_shape=tuple(out_shape), grid=(n // tm,), in_specs=in_specs, out_specs=tuple(out_specs),
        compiler_params=_cparams(("parallel",)), name="add_ln_" + mode,
    )(*args)


def _embed_ln_kernel(x_ref, meta_ref, g_ref, b_ref, o_ref, ob_ref, *, front, n_meta, n_blocks):
    j = pl.program_id(1)

    @pl.when((j > 0) & (j <= n_blocks))
    def _():
        y = _layer_norm_rows(x_ref[...], g_ref, b_ref)
        o_ref[...] = y
        ob_ref[...] = y.astype(BF16)

    @pl.when(j == 0)
    def _():
        y = _layer_norm_rows(meta_ref[...], g_ref, b_ref)
        o_ref[...] = jnp.zeros_like(o_ref)
        ob_ref[...] = jnp.zeros_like(ob_ref)
        o_ref[front:front + n_meta, :] = y
        ob_ref[front:front + n_meta, :] = y.astype(BF16)

    @pl.when(j > n_blocks)
    def _():
        o_ref[...] = jnp.zeros_like(o_ref)
        ob_ref[...] = jnp.zeros_like(ob_ref)


def _embed_ln(x, meta_tokens, g, b, lay):
    front, t_valid, tpp = lay
    bsz, seq, d = x.shape
    n_meta = meta_tokens.shape[0]
    assert front + n_meta == CHUNK and seq % CHUNK == 0 and tpp == seq + 2 * CHUNK
    nb = seq // CHUNK
    blocks = tpp // CHUNK
    spec = pl.BlockSpec((CHUNK, d), lambda bb, j: (bb * blocks + j, 0))
    vec = pl.BlockSpec((1, d), lambda bb, j: (0, 0))
    return pl.pallas_call(
        functools.partial(_embed_ln_kernel, front=front, n_meta=n_meta, n_blocks=nb),
        out_shape=(jax.ShapeDtypeStruct((bsz * tpp, d), F32), jax.ShapeDtypeStruct((bsz * tpp, d), BF16)),
        grid=(bsz, blocks),
        in_specs=[pl.BlockSpec((None, CHUNK, d), lambda bb, j: (bb, jnp.clip(j - 1, 0, nb - 1), 0)),
                  pl.BlockSpec((n_meta, d), lambda bb, j: (0, 0)), vec, vec],
        out_specs=(spec, spec),
        compiler_params=_cparams(("parallel", "parallel")), name="embed_ln",
    )(x, meta_tokens.astype(x.dtype), g.reshape(1, d), b.reshape(1, d))


def _final_ln_kernel(h_ref, y0_ref, y1_ref, rt_ref, g_ref, b_ref, o_ref, *, alpha):
    x = alpha * h_ref[...] + rt_ref[:, 2:3] * y0_ref[...] + rt_ref[:, 3:4] * y1_ref[...]
    o_ref[...] = _layer_norm_rows(x, g_ref, b_ref)


def _final_ln(h, y2, route, g, b, alpha, lay, bsz, seq):
    front, t_valid, tpp = lay
    n, d = h.shape
    blocks = tpp // CHUNK
    first = (front + t_valid - seq) // CHUNK
    row = lambda bb, j: bb * blocks + first + j
    return pl.pallas_call(
        functools.partial(_final_ln_kernel, alpha=alpha),
        out_shape=jax.ShapeDtypeStruct((bsz, seq, d), F32),
        grid=(bsz, seq // CHUNK),
        in_specs=[pl.BlockSpec((CHUNK, d), lambda bb, j: (row(bb, j), 0)),
                  pl.BlockSpec((CHUNK, d), lambda bb, j: (row(bb, j), 0)),
                  pl.BlockSpec((CHUNK, d), lambda bb, j: (row(bb, j) + n // CHUNK, 0)),
                  pl.BlockSpec((CHUNK, LANE), lambda bb, j: (row(bb, j), 0)),
                  pl.BlockSpec((1, d), lambda bb, j: (0, 0)), pl.BlockSpec((1, d), lambda bb, j: (0, 0))],
        out_specs=pl.BlockSpec((None, CHUNK, d), lambda bb, j: (bb, j, 0)),
        compiler_params=_cparams(("parallel", "parallel")), name="final_ln",
    )(h, y2, y2, route, g.reshape(1, d), b.reshape(1, d))


def _out_proj_kernel(*refs, ks):
    a_refs = refs[:len(ks)]
    w_ref, o_ref, wb_ref = refs[len(ks):]

    @pl.when(pl.program_id(1) == 0)
    def _():
        step = min(w_ref.shape[0], 512)
        for r0 in range(0, w_ref.shape[0], step):
            wb_ref[r0:r0 + step, :] = w_ref[r0:r0 + step, :].astype(BF16)

    acc = None
    off = 0
    for a_ref, k in zip(a_refs, ks):
        p = _dot(a_ref[...], wb_ref[off:off + k, :])
        acc = p if acc is None else acc + p
        off += k
    o_ref[...] = acc.astype(o_ref.dtype)


def _out_proj(a_list, w_all, layer, wide, tpp):
    assert not wide[0]
    n = a_list[0].shape[0]
    _, kdim, ndim = w_all.shape
    ks = tuple(kdim // len(a_list) for _ in a_list)
    tm = _divisor_tile(tpp, 1408, 16)
    tn = _divisor_tile(ndim, 512, LANE)
    row_tiles = tpp // tm
    in_specs = [pl.BlockSpec((tm, k), (lambda j, i: (i % row_tiles, i // row_tiles)) if wd else (lambda j, i: (i, 0)))
                for k, wd in zip(ks, wide)]
    in_specs.append(pl.BlockSpec((None, kdim, tn), lambda j, i: (layer, 0, j)))
    return pl.pallas_call(
        functools.partial(_out_proj_kernel, ks=ks),
        out_shape=jax.ShapeDtypeStruct((n, ndim), BF16),
        grid=(ndim // tn, n // tm),
        in_specs=in_specs,
        out_specs=pl.BlockSpec((tm, tn), lambda j, i: (i, j)),
        scratch_shapes=[pltpu.VMEM((kdim, tn), BF16)],
        compiler_params=_cparams(("parallel", "arbitrary")),
        name="out_proj",
    )(*a_list, w_all)


def _in_proj_kernel(a_ref, w_ref, wn_ref, o_ref, wb_ref, *, n_plain, shift):
    j = pl.program_id(0)
    tn = w_ref.shape[0]
    step = min(tn, LANE)

    @pl.when(pl.program_id(1) == 0)
    def _():
        @pl.when(j < n_plain)
        def _():
            for r0 in range(0, tn, step):
                wb_ref[r0:r0 + step, :] = w_ref[r0:r0 + step, :].astype(BF16)

        @pl.when(j >= n_plain)
        def _():
            for r0 in range(0, tn - step, step):
                wb_ref[r0:r0 + step, :] = w_ref[r0 + shift:r0 + shift + step, :].astype(BF16)
            wb_ref[tn - step:tn - shift, :] = w_ref[tn - step + shift:, :].astype(BF16)
            wb_ref[tn - shift:, :] = wn_ref[:shift, :].astype(BF16)

    o_ref[...] = _dot_nt(a_ref[...], wb_ref[...])


def _small_proj_kernel(a_ref, w_ref, o_ref):
    o_ref[...] = _dot_nt(a_ref[...], w_ref[...].astype(BF16))


def _in_proj(hb, w_in_t, layer, w):
    n, d = hb.shape
    n_small = w_in_t.shape[1] - 15 * w
    tm = _divisor_tile(n, 1408, 16)
    tn = _divisor_tile(w, 512, LANE)
    z = pl.pallas_call(
        functools.partial(_in_proj_kernel, n_plain=4 * w // tn, shift=n_small),
        out_shape=jax.ShapeDtypeStruct((n, 15 * w), F32),
        grid=(15 * w // tn, n // tm),
        in_specs=[pl.BlockSpec((tm, d), lambda j, i: (i, 0)),
                  pl.BlockSpec((None, tn, d), lambda j, i: (layer, j, 0)),
                  pl.BlockSpec((None, LANE, d), lambda j, i: (layer, (j + 1) * (tn // LANE), 0))],
        out_specs=pl.BlockSpec((tm, tn), lambda j, i: (i, j)),
        scratch_shapes=[pltpu.VMEM((tn, d), BF16)],
        compiler_params=_cparams(("parallel", "arbitrary")),
        name="in_proj",
    )(hb, w_in_t, w_in_t)
    zs = pl.pallas_call(
        _small_proj_kernel,
        out_shape=jax.ShapeDtypeStruct((n, LANE), F32),
        grid=(n // tm,),
        in_specs=[pl.BlockSpec((tm, d), lambda i: (i, 0)),
                  pl.BlockSpec((None, LANE, d), lambda i: (layer, 4 * w // LANE, 0))],
        out_specs=pl.BlockSpec((tm, LANE), lambda i: (i, 0)),
        compiler_params=_cparams(("parallel",)),
        name="small_proj",
    )(hb, w_in_t)
    return z, zs


def _tri_masks(n):
    r = lax.broadcasted_iota(jnp.int32, (n, n), 0)
    c = lax.broadcasted_iota(jnp.int32, (n, n), 1)
    return r, c


def _gated_rms(o, w_row, gate):
    o = o * lax.rsqrt(jnp.mean(o * o, axis=-1, keepdims=True) + NORM_EPS)
    if w_row is not None:
        o = o * w_row
    return o * _silu(gate)


def _prefix_rows(x, row, seg):
    pos = row & (seg - 1)
    s = 1
    while s < seg:
        x = x + jnp.where(pos >= s, pltpu.roll(x, s, 0), 0.0)
        s *= 2
    return x


def _mixer_a_kernel(q_ref, k_ref, v_ref, gt_ref, sm_ref, cw_ref, alog_ref, dtb_ref, nw_ref, o_ref,
                    s_ref, prev_ref, *, n_batch, n_heads, front, t_valid):
    c = pl.program_id(0)

    @pl.when(c == 0)
    def _():
        s_ref[...] = jnp.zeros_like(s_ref)
        prev_ref[...] = jnp.zeros_like(prev_ref)

    row = lax.broadcasted_iota(jnp.int32, (CHUNK, 1), 0)
    row8 = lax.broadcasted_iota(jnp.int32, (SUBLANES, 1), 0)
    valid = _valid_rows(c * CHUNK, CHUNK, front, t_valid).astype(F32)
    r, cc = _tri_masks(CHUNK)
    causal = r >= cc
    strict = r > cc
    eye = (r == cc).astype(F32)

    def conv_silu(x_ref, bi, p):
        cur = x_ref[bi]
        prev8 = prev_ref[p, bi]
        acc = cur * cw_ref[p, 3:4, :]
        for s in (1, 2, 3):
            rolled = pltpu.roll(cur, s, 0)
            top = jnp.where(row8 >= s, rolled[:SUBLANES], pltpu.roll(prev8, s, 0))
            acc = acc + jnp.concatenate([top, rolled[SUBLANES:]], axis=0) * cw_ref[p, 3 - s:4 - s, :]
        prev_ref[p, bi] = cur[CHUNK - SUBLANES:]
        return _silu(acc)

    q_all, k_all, v_all, beta_all, gcum_all, gcum_t = [], [], [], [], [], []
    for bi in range(n_batch):
        q_all.append(conv_silu(q_ref, bi, 0))
        k_all.append(conv_silu(k_ref, bi, 1) * valid)
        v_all.append(conv_silu(v_ref, bi, 2) * valid)
        sm = sm_ref[bi]
        beta_all.append(_sigmoid(sm) * valid)
        g_all = -jnp.exp(alog_ref[...]) * _softplus(sm + dtb_ref[...]) * valid
        gcum_all.append(_prefix_rows(g_all, row, CHUNK))
        gcum_t.append(lax.dot_general(gcum_all[bi], eye, (((0,), (0,)), ((), ())), preferred_element_type=F32,
                                      precision=HI))

    chains = [(bi, h) for bi in range(n_batch) for h in range(n_heads)]
    ids = range(len(chains))
    hsl = [slice(h * HEAD_DIM, (h + 1) * HEAD_DIM) for _, h in chains]
    beta = [beta_all[bi][:, h:h + 1] for bi, h in chains]
    gcol = [gcum_all[bi][:, n_heads + h:n_heads + h + 1] for bi, h in chains]
    decay = [jnp.where(causal, jnp.exp(jnp.minimum(gcol[t] - gcum_t[bi][n_heads + h:n_heads + h + 1, :], 0.0)), 0.0)
             for t, (bi, h) in enumerate(chains)]
    eg = [jnp.exp(g) for g in gcol]
    g_last = [g[CHUNK - 1:CHUNK, :] for g in gcol]
    qh = [q_all[bi][:, hsl[t]] for t, (bi, _) in enumerate(chains)]
    kh = [k_all[bi][:, hsl[t]] for t, (bi, _) in enumerate(chains)]
    q = [x * lax.rsqrt(jnp.sum(x * x, axis=-1, keepdims=True) + NORM_EPS) * HEAD_DIM ** -0.5 for x in qh]
    k = [x * lax.rsqrt(jnp.sum(x * x, axis=-1, keepdims=True) + NORM_EPS) for x in kh]
    kb = [k[t] * beta[t] for t in ids]
    k16 = [x.astype(BF16) for x in k]
    m = [jnp.where(strict, -(_dot_nt(kb[t].astype(BF16), k16[t]) * decay[t]), 0.0) for t in ids]
    attn = [_dot_nt(q[t].astype(BF16), k16[t]) * decay[t] for t in ids]
    inv = [eye + x for x in m]
    mp = [_dot(x.astype(BF16), x.astype(BF16)) for x in m]
    for _ in range(4):
        both = [_dot(jnp.concatenate([inv[t], mp[t]], axis=0).astype(BF16), mp[t].astype(BF16)) for t in ids]
        inv = [inv[t] + both[t][:CHUNK] for t in ids]
        mp = [both[t][CHUNK:] for t in ids]
    inv = [inv[t] + _dot(inv[t].astype(BF16), mp[t].astype(BF16)) for t in ids]
    sol = [_dot(inv[t].astype(BF16),
                jnp.concatenate([v_all[bi][:, hsl[t]] * beta[t], kb[t] * eg[t]], axis=1).astype(BF16))
           for t, (bi, _) in enumerate(chains)]
    s_old = [s_ref[t] for t in ids]
    s16 = [x.astype(BF16) for x in s_old]
    v_new = [sol[t][:, :HEAD_DIM] - _dot(sol[t][:, HEAD_DIM:].astype(BF16), s16[t]) for t in ids]
    o = [_dot((q[t] * eg[t]).astype(BF16), s16[t]) + _dot(attn[t].astype(BF16), v_new[t].astype(BF16)) for t in ids]
    for t, (bi, _) in enumerate(chains):
        s_ref[t] = s_old[t] * jnp.exp(g_last[t]) + _dot_tn(k[t] * jnp.exp(g_last[t] - gcol[t]), v_new[t])
        o_ref[bi, :, hsl[t]] = _gated_rms(o[t], nw_ref[...], gt_ref[bi, :, hsl[t]]).astype(o_ref.dtype)


def _mixer_b_kernel(q_ref, f_ref, i_ref, gt_ref, lb_ref, nw_ref, o_ref, st_ref, kbuf, bbuf, obuf,
                    *, n_batch, n_heads, front, t_valid):
    c = pl.program_id(0)

    @pl.when(c == 0)
    def _():
        st_ref[...] = jnp.zeros_like(st_ref)

    row = lax.broadcasted_iota(jnp.int32, (CHUNK, 1), 0)
    valid = _valid_rows(c * CHUNK, CHUNK, front, t_valid).astype(F32)
    row16 = lax.broadcasted_iota(jnp.int32, (SUB, 1), 0)

    lb = lb_ref[...]
    for bi in range(n_batch):
        f = lb + (1.0 - lb) * _sigmoid(f_ref[bi])
        kbuf[bi] = (1.0 - f) * valid
        bbuf[bi] = _prefix_rows(jnp.log(f) * valid, row, SUB)

    chains = [(bi, h) for bi in range(n_batch) for h in range(n_heads)]
    hsl = [slice(h * HEAD_DIM, (h + 1) * HEAD_DIM) for _, h in chains]
    st = [st_ref[t] for t in range(len(chains))]
    for blk in range(CHUNK // SUB):
        rs = slice(blk * SUB, (blk + 1) * SUB)
        for t, (bi, _) in enumerate(chains):
            hs = hsl[t]
            b16 = bbuf[bi, rs, hs]
            q16 = _silu(q_ref[bi, rs, hs])
            b_end = b16[SUB - 1:SUB, :]
            acc = _dot_nt((q16 * jnp.exp(b16)).astype(BF16), st[t].astype(BF16))
            parts = []
            for t0 in range(0, SUB, SUBLANES):
                ts = slice(t0, t0 + SUBLANES)
                bq, qq, rowq = b16[ts], q16[ts], row16[ts]
                part = acc[ts]
                for j in range(min(t0 + SUBLANES, SUB)):
                    jr = pl.ds(blk * SUB + j, 1)
                    sc = jnp.sum(qq * kbuf[bi, jr, hs] * jnp.exp(bq - bbuf[bi, jr, hs]), axis=-1, keepdims=True)
                    if j > t0:
                        sc = jnp.where(rowq >= j, sc, 0.0)
                    part = part + sc * i_ref[bi, jr, hs]
                parts.append(part)
            acc = jnp.concatenate(parts, axis=0)
            st[t] = st[t] * jnp.exp(b_end) + _dot_tn(i_ref[bi, rs, hs], kbuf[bi, rs, hs] * jnp.exp(b_end - b16))
            obuf[bi, rs, hs] = acc
    for t, (bi, _) in enumerate(chains):
        st_ref[t] = st[t]
        o_ref[bi, :, hsl[t]] = _gated_rms(obuf[bi, :, hsl[t]], nw_ref[...], gt_ref[bi, :, hsl[t]]).astype(o_ref.dtype)


def _rope_partial(x, cos, sin_lo, sin_hi):
    half = ROPE_DIMS // 2
    return x * cos + pltpu.roll(x, LANE - half, 1) * sin_lo + pltpu.roll(x, half, 1) * sin_hi


def _rope_c_kernel(q_ref, k_ref, v_ref, cos_ref, slo_ref, shi_ref, qo_ref, ko_ref, vo_ref, *, n_maps):
    cos, slo, shi = cos_ref[...], slo_ref[...], shi_ref[...]
    for mp in range(n_maps):
        ms = slice(mp * HEAD_DIM, (mp + 1) * HEAD_DIM)
        qo_ref[:, ms] = (_rope_partial(q_ref[:, ms], cos, slo, shi) * HEAD_DIM ** -0.5).astype(BF16)
        ko_ref[:, ms] = _rope_partial(k_ref[:, ms], cos, slo, shi).astype(BF16)
    vo_ref[...] = v_ref[...].astype(BF16)


def _mixer_c_kernel(qi_ref, kj_ref, q_ref, k_ref, v_ref, lam_ref, nw_ref, o_ref, m_ref, l_ref, acc_ref,
                    *, tq, front, out_scale, n_heads):
    p = pl.program_id(0)
    i = qi_ref[p]
    j = kj_ref[p]
    vw = 2 * HEAD_DIM

    @pl.when(j == 0)
    def _():
        m_ref[...] = jnp.full_like(m_ref, NEG)
        l_ref[...] = jnp.zeros_like(l_ref)
        acc_ref[...] = jnp.zeros_like(acc_ref)

    def step(masked):
        if masked:
            rq = i * tq + lax.broadcasted_iota(jnp.int32, (tq, 1), 0)
            rk = j * tq + lax.broadcasted_iota(jnp.int32, (1, tq), 1)
            msk = (rk <= rq) & (rk >= front)
        for hd in range(n_heads):
            v = v_ref[:, hd * vw:(hd + 1) * vw]
            rows = [2 * hd, 2 * hd + 1]
            msl = [slice(r * HEAD_DIM, (r + 1) * HEAD_DIM) for r in rows]
            s = [_dot_nt(q_ref[:, ms], k_ref[:, ms]) for ms in msl]
            if masked:
                s = [jnp.where(msk, x, NEG) for x in s]
            m_prev = [m_ref[r] for r in rows]
            m_new = [jnp.maximum(m_prev[t], jnp.max(s[t], axis=-1, keepdims=True)) for t in range(2)]
            pr = [jnp.exp(s[t] - jnp.tile(m_new[t], (1, tq // LANE))) for t in range(2)]
            pv = [_dot(pr[t].astype(BF16), v) for t in range(2)]
            for t, r in enumerate(rows):
                a = jnp.exp(m_prev[t] - m_new[t])
                l_ref[r] = a * l_ref[r] + jnp.sum(pr[t], axis=-1, keepdims=True)
                acc_ref[r] = jnp.tile(a, (1, vw // LANE)) * acc_ref[r] + pv[t]
                m_ref[r] = m_new[t]

    needs_mask = (j == i) | (j == 0)

    @pl.when(needs_mask)
    def _():
        step(True)

    @pl.when(jnp.logical_not(needs_mask))
    def _():
        step(False)

    @pl.when(j == i)
    def _():
        reps = (1, vw // LANE)
        for hd in range(n_heads):
            r0, r1 = 2 * hd, 2 * hd + 1
            o = acc_ref[r0] / jnp.tile(l_ref[r0], reps) - lam_ref[...] * (acc_ref[r1] / jnp.tile(l_ref[r1], reps))
            o = o * lax.rsqrt(jnp.mean(o * o, axis=-1, keepdims=True) + NORM_EPS) * nw_ref[...] * out_scale
            o_ref[:, hd * vw:(hd + 1) * vw] = o.astype(o_ref.dtype)


def _mixer_d_kernel(q_ref, k_ref, v_ref, gt_ref, cos_ref, sin_ref, dm_ref, qd_ref, kd_ref, cd_ref, o_ref, s_ref,
                    *, n_batch, n_heads):
    c = pl.program_id(0)

    @pl.when(c == 0)
    def _():
        s_ref[...] = jnp.zeros_like(s_ref)

    cos = cos_ref[...]
    sin = sin_ref[...]
    chains = [(bi, h) for bi in range(n_batch) for h in range(n_heads)]
    ids = range(len(chains))
    hsl = [slice(h * HEAD_DIM, (h + 1) * HEAD_DIM) for _, h in chains]
    q = [q_ref[bi, :, hsl[t]] * cos + pltpu.roll(q_ref[bi, :, hsl[t]], HEAD_DIM // 2, 1) * sin
         for t, (bi, _) in enumerate(chains)]
    k = [(k_ref[bi, :, hsl[t]] * cos + pltpu.roll(k_ref[bi, :, hsl[t]], HEAD_DIM // 2, 1) * sin) * HEAD_DIM ** -0.5
         for t, (bi, _) in enumerate(chains)]
    s_old = [s_ref[t] for t in ids]
    a = [_dot_nt(q[t].astype(BF16), k[t].astype(BF16)) * dm_ref[h] for t, (_, h) in enumerate(chains)]
    o = [_dot(a[t].astype(BF16), v_ref[bi, :, hsl[t]].astype(BF16))
         + _dot((q[t] * qd_ref[h]).astype(BF16), s_old[t].astype(BF16)) for t, (bi, h) in enumerate(chains)]
    for t, (bi, h) in enumerate(chains):
        s_ref[t] = cd_ref[h] * s_old[t] + _dot_tn(k[t] * kd_ref[h], v_ref[bi, :, hsl[t]])
        o_ref[bi, :, hsl[t]] = _gated_rms(o[t], None, gt_ref[bi, :, hsl[t]]).astype(o_ref.dtype)


def _route(lg):
    n_e = N_GROUPS * EXP_PER_GROUP
    lane_i = lax.broadcasted_iota(jnp.int32, lg.shape, 1)
    lane = lane_i.astype(F32)
    grp = (lane_i >> GROUP_SHIFT).astype(F32)
    big = float(4 * LANE)
    is_g = (lane_i >= n_e) & (lane_i < n_e + N_GROUPS)
    gl = jnp.where(is_g, lg, NEG)
    gmax = jnp.max(gl, axis=-1, keepdims=True)
    gsum = jnp.sum(jnp.where(is_g, jnp.exp(gl - gmax), 0.0), axis=-1, keepdims=True)
    g_w = 1.0 / gsum
    g_idx = jnp.min(jnp.where(is_g & (gl == gmax), lane - n_e, big), axis=-1, keepdims=True)
    in_grp = (lane_i < n_e) & (grp == g_idx)
    el = jnp.where(in_grp, lg, NEG)
    emax = jnp.max(el, axis=-1, keepdims=True)
    eexp = jnp.where(in_grp, jnp.exp(el - emax), 0.0)
    pe = eexp / jnp.sum(eexp, axis=-1, keepdims=True)
    p1 = jnp.max(jnp.where(in_grp, pe, -1.0), axis=-1, keepdims=True)
    i1 = jnp.min(jnp.where(in_grp & (pe == p1), lane, big), axis=-1, keepdims=True)
    rest = in_grp & (lane != i1)
    p2 = jnp.max(jnp.where(rest, pe, -1.0), axis=-1, keepdims=True)
    i2 = jnp.min(jnp.where(rest & (pe == p2), lane, big), axis=-1, keepdims=True)
    den = p1 + p2
    return (jnp.where(lane_i == 0, i1, 0.0) + jnp.where(lane_i == 1, i2, 0.0)
            + jnp.where(lane_i == 2, g_w * (p1 / den), 0.0) + jnp.where(lane_i == 3, g_w * (p2 / den), 0.0))


def _row_copy(src, src_row, dst, dst_row, sem):
    return pltpu.make_async_copy(src.at[pl.ds(src_row, 1)], dst.at[pl.ds(dst_row, 1)], sem)


def _moe_kernel(te_ref, cnt_ref, first_ref, src_ref, srcn_ref, dst_ref, x_hbm, wg_ref, wu_ref, wd_ref, y_hbm,
                xbuf, ybuf, wgb, wub, wdb, sem_in, sem_out, *, tm, n_tiles):
    del te_ref
    t = pl.program_id(0)
    slot = t % 2
    cnt = cnt_ref[t]

    def pairs(rows):
        return (rows + 1) // 2

    def gather_start(idx_ref, s, rows):
        def body(g, c):
            for u in range(2):
                r = 2 * g + u
                _row_copy(x_hbm, idx_ref[0, 0, r], xbuf.at[s], r, sem_in.at[s]).start(priority=u)
            return c
        lax.fori_loop(0, pairs(rows), body, 0)

    def wait_rows(src, dst, sem, rows):
        def tile_body(g, c):
            pltpu.make_async_copy(src.at[pl.ds(0, SUBLANES)], dst.at[pl.ds(g * SUBLANES, SUBLANES)], sem).wait()
            return c
        lax.fori_loop(0, rows // SUBLANES, tile_body, 0)

        def row_body(r, c):
            _row_copy(src, 0, dst, r, sem).wait()
            return c
        lax.fori_loop(rows - rows % SUBLANES, rows, row_body, 0)

    def gather_wait(s, rows):
        wait_rows(x_hbm, xbuf.at[s], sem_in.at[s], 2 * pairs(rows))

    def scatter_start(s, rows):
        def body(g, c):
            for u in range(2):
                r = 2 * g + u
                _row_copy(ybuf.at[s], r, y_hbm, dst_ref[0, 0, r], sem_out.at[s]).start(priority=u)
            return c
        lax.fori_loop(0, rows // 2, body, 0)

        @pl.when(rows % 2 == 1)
        def _():
            _row_copy(ybuf.at[s], rows - 1, y_hbm, dst_ref[0, 0, rows - 1], sem_out.at[s]).start()

    def scatter_wait(s, rows):
        wait_rows(ybuf.at[s], y_hbm, sem_out.at[s], rows)

    @pl.when(t == 0)
    def _():
        xbuf[...] = jnp.zeros_like(xbuf)

        @pl.when(cnt > 0)
        def _():
            gather_start(src_ref, 0, cnt)

    nxt = jnp.minimum(t + 1, n_tiles - 1)

    @pl.when((t + 1 < n_tiles) & (cnt_ref[nxt] > 0))
    def _():
        gather_start(srcn_ref, 1 - slot, cnt_ref[nxt])

    old = jnp.maximum(t - 2, 0)

    @pl.when((t >= 2) & (cnt_ref[old] > 0))
    def _():
        scatter_wait(slot, cnt_ref[old])

    @pl.when(cnt > 0)
    def _():
        @pl.when(first_ref[t] == 1)
        def _():
            wgb[...] = wg_ref[...].astype(BF16)
            wub[...] = wu_ref[...].astype(BF16)
            wdb[...] = wd_ref[...].astype(BF16)

        gather_wait(slot, cnt)
        x = xbuf[slot].astype(BF16)
        hg = _dot(x, wgb[...])
        hu = _dot(x, wub[...])
        act = (_silu(hg) * hu).astype(BF16)
        ybuf[slot] = _dot(act, wdb[...])
        scatter_start(slot, cnt)

    @pl.when(t == n_tiles - 1)
    def _():
        prev = jnp.maximum(t - 1, 0)

        @pl.when((t >= 1) & (cnt_ref[prev] > 0))
        def _():
            scatter_wait(1 - slot, cnt_ref[prev])

        @pl.when(cnt > 0)
        def _():
            scatter_wait(slot, cnt)


def _dispatch(route, n, tm, n_e):
    ids = jnp.concatenate([route[:, 0], route[:, 1]]).astype(jnp.int32)
    n_pairs = 2 * n
    n_tiles = n_pairs // tm + n_e
    order = jnp.argsort(ids, stable=True).astype(jnp.int32)
    counts = jnp.sum((ids[:, None] == jnp.arange(n_e, dtype=jnp.int32)[None, :]).astype(jnp.int32), axis=0)
    tiles_e = (counts + tm - 1) // tm
    tile_end = jnp.cumsum(tiles_e)
    tile_start = tile_end - tiles_e
    first = jnp.cumsum(counts) - counts
    tile = jnp.arange(n_tiles, dtype=jnp.int32)
    te = jnp.minimum(jnp.searchsorted(tile_end, tile, side="right").astype(jnp.int32), n_e - 1)
    cnt = jnp.clip(counts[te] - (tile - tile_start[te]) * tm, 0, tm)
    cnt = jnp.where(tile < tile_end[n_e - 1], cnt, 0).astype(jnp.int32)
    row = jnp.arange(tm, dtype=jnp.int32)[None, :]
    spos = (first[te] + (tile - tile_start[te]) * tm)[:, None] + row
    dst = jnp.where(row < cnt[:, None], order[jnp.clip(spos, 0, n_pairs - 1)], 0)
    src = dst % n
    te = jnp.where(cnt > 0, te, jnp.max(jnp.where(cnt > 0, te, 0)))
    new_expert = jnp.concatenate([jnp.ones((1,), jnp.int32), (te[1:] != te[:-1]).astype(jnp.int32)])
    return te, cnt, new_expert, src.reshape(n_tiles, 1, tm), dst.reshape(n_tiles, 1, tm)


def _moe_routed(h, route, wg, wu, wd, layer, tm):
    n, d = h.shape
    _, n_e, _, f = wg.shape
    te, cnt, new_expert, src, dst = _dispatch(route, n, tm, n_e)
    n_tiles = te.shape[0]

    def idx_spec(shift):
        return pl.BlockSpec((1, 1, tm), lambda t, *_: (jnp.minimum(t + shift, n_tiles - 1), 0, 0),
                            memory_space=pltpu.SMEM)

    grid_spec = pltpu.PrefetchScalarGridSpec(
        num_scalar_prefetch=3, grid=(n_tiles,),
        in_specs=[idx_spec(0), idx_spec(1), idx_spec(0),
                  pl.BlockSpec(memory_space=pl.ANY),
                  pl.BlockSpec((None, None, d, f), lambda t, te, *_: (layer, te[t], 0, 0)),
                  pl.BlockSpec((None, None, d, f), lambda t, te, *_: (layer, te[t], 0, 0)),
                  pl.BlockSpec((None, None, f, d), lambda t, te, *_: (layer, te[t], 0, 0))],
        out_specs=pl.BlockSpec(memory_space=pl.ANY),
        scratch_shapes=[pltpu.VMEM((2, tm, d), F32), pltpu.VMEM((2, tm, d), F32),
                        pltpu.VMEM((d, f), BF16), pltpu.VMEM((d, f), BF16), pltpu.VMEM((f, d), BF16),
                        pltpu.SemaphoreType.DMA((2,)), pltpu.SemaphoreType.DMA((2,))])
    return pl.pallas_call(
        functools.partial(_moe_kernel, tm=tm, n_tiles=n_tiles),
        out_shape=jax.ShapeDtypeStruct((2 * n, d), F32),
        grid_spec=grid_spec,
        compiler_params=_cparams(("arbitrary",)),
        name="moe_routed",
    )(te, cnt, new_expert, src, src, dst, h, wg, wu, wd)


def _mixers(z, zs, lay, bsz, w, prm):
    front, t_valid, tpp = lay
    n = z.shape[0]
    nc = tpp // CHUNK
    heads = w // HEAD_DIM
    c_heads = w // (2 * HEAD_DIM)
    z3 = z.reshape(bsz, tpp, z.shape[1])
    zs3 = zs.reshape(bsz, tpp, LANE)

    def cspec(region):
        return pl.BlockSpec((bsz, CHUNK, w), lambda c, region=region: (0, c, region))

    def fixed(shape):
        nd = len(shape)
        return pl.BlockSpec(shape, lambda c: (0,) * nd)

    ospec = pl.BlockSpec((bsz, CHUNK, w), lambda c: (0, c, 0))
    oshape3 = jax.ShapeDtypeStruct((bsz, tpp, w), BF16)
    oshape = jax.ShapeDtypeStruct((n, w), BF16)
    grid = (nc,)
    sem = ("arbitrary",)
    state = pltpu.VMEM((bsz * heads, HEAD_DIM, HEAD_DIM), F32)

    oa = pl.pallas_call(
        functools.partial(_mixer_a_kernel, n_batch=bsz, n_heads=heads, front=front, t_valid=t_valid),
        out_shape=oshape3, grid=grid,
        in_specs=[cspec(0), cspec(1), cspec(2), cspec(3),
                  pl.BlockSpec((bsz, CHUNK, LANE), lambda c: (0, c, 0)),
                  fixed((3, 4, w)), fixed((1, LANE)), fixed((1, LANE)), fixed((1, HEAD_DIM))],
        out_specs=ospec,
        scratch_shapes=[state, pltpu.VMEM((3, bsz, SUBLANES, w), F32)],
        compiler_params=_cparams(sem), name="mixer_a",
    )(z3, z3, z3, z3, zs3, prm["conv_w"], prm["a_log_row"], prm["dt_bias_row"], prm["norm_a"]).reshape(n, w)

    ob = pl.pallas_call(
        functools.partial(_mixer_b_kernel, n_batch=bsz, n_heads=heads, front=front, t_valid=t_valid),
        out_shape=oshape3, grid=grid,
        in_specs=[cspec(4), cspec(5), cspec(6), cspec(7), fixed((1, w)), fixed((1, HEAD_DIM))],
        out_specs=ospec,
        scratch_shapes=[state] + [pltpu.VMEM((bsz, CHUNK, w), F32)] * 3,
        compiler_params=_cparams(sem), name="mixer_b",
    )(z3, z3, z3, z3, prm["lb"], prm["norm_b"]).reshape(n, w)

    tq = _divisor_tile(tpp, 384, LANE)
    nq = tpp // tq
    vw = 2 * HEAD_DIM
    rc = prm["rope_c"]
    wide = jax.ShapeDtypeStruct((tpp, bsz * w), BF16)
    rspec = pl.BlockSpec((tq, w), lambda i: (i % nq, i // nq))
    tab = pl.BlockSpec((tq, HEAD_DIM), lambda i: (i % nq, 0))
    qr, kr, vr = pl.pallas_call(
        functools.partial(_rope_c_kernel, n_maps=heads),
        out_shape=(wide, wide, wide), grid=(n // tq,),
        in_specs=[pl.BlockSpec((tq, w), lambda i: (i, 8)), pl.BlockSpec((tq, w), lambda i: (i, 9)),
                  pl.BlockSpec((tq, w), lambda i: (i, 10)), tab, tab, tab],
        out_specs=(rspec, rspec, rspec),
        compiler_params=_cparams(("parallel",)), name="rope_c",
    )(z, z, z, rc[0], rc[1], rc[2])

    pairs = [(i, j) for i in range(nq) for j in range(i + 1)]
    qi = jnp.asarray([p[0] for p in pairs], jnp.int32)
    kj = jnp.asarray([p[1] for p in pairs], jnp.int32)
    qspec = pl.BlockSpec((tq, bsz * w), lambda p, qi, kj: (qi[p], 0))
    kspec = pl.BlockSpec((tq, bsz * w), lambda p, qi, kj: (kj[p], 0))
    row_vw = pl.BlockSpec((1, vw), lambda p, qi, kj: (0, 0))
    n_chains = 2 * bsz * c_heads
    oc = pl.pallas_call(
        functools.partial(_mixer_c_kernel, tq=tq, front=front, out_scale=prm["c_out_scale"], n_heads=bsz * c_heads),
        out_shape=wide,
        grid_spec=pltpu.PrefetchScalarGridSpec(
            num_scalar_prefetch=2, grid=(len(pairs),),
            in_specs=[qspec, kspec, kspec, row_vw, row_vw],
            out_specs=qspec,
            scratch_shapes=[pltpu.VMEM((n_chains, tq, LANE), F32), pltpu.VMEM((n_chains, tq, LANE), F32),
                            pltpu.VMEM((n_chains, tq, vw), F32)]),
        compiler_params=_cparams(("arbitrary",)), name="mixer_c",
    )(qi, kj, qr, kr, vr, prm["lam_row"], prm["subln_c"])

    tabd = pl.BlockSpec((CHUNK, HEAD_DIM), lambda c: (c, 0))
    rd = prm["rope_d"]
    od = pl.pallas_call(
        functools.partial(_mixer_d_kernel, n_batch=bsz, n_heads=heads),
        out_shape=oshape3, grid=grid,
        in_specs=[cspec(11), cspec(12), cspec(13), cspec(14), tabd, tabd,
                  fixed((heads, CHUNK, CHUNK)), fixed((heads, CHUNK, HEAD_DIM)), fixed((heads, CHUNK, HEAD_DIM)),
                  fixed((heads, 1, HEAD_DIM))],
        out_specs=ospec,
        scratch_shapes=[state],
        compiler_params=_cparams(sem), name="mixer_d",
    )(z3, z3, z3, z3, rd[0], rd[1], prm["ret_dmask"], prm["ret_qdec"], prm["ret_kdec"], prm["ret_cdec"]).reshape(n, w)
    return oa, ob, oc, od


def _rope_tables(front, tpp):
    pos = (jnp.arange(tpp) - front).astype(F32)[:, None]
    half = ROPE_DIMS // 2
    inv = 1.0 / (ROPE_THETA ** (jnp.arange(half, dtype=F32) / half))
    ang = pos * inv[None, :]
    z_rest = jnp.zeros((tpp, HEAD_DIM - ROPE_DIMS), F32)
    z_half = jnp.zeros((tpp, half), F32)
    cos_c = jnp.concatenate([jnp.cos(ang), jnp.cos(ang), jnp.ones_like(z_rest)], axis=1)
    sin_lo = jnp.concatenate([-jnp.sin(ang), z_half, z_rest], axis=1)
    sin_hi = jnp.concatenate([z_half, jnp.sin(ang), z_rest], axis=1)
    half_d = HEAD_DIM // 2
    inv_d = 1.0 / (RET_THETA ** (jnp.arange(half_d, dtype=F32) / half_d))
    ang_d = pos * inv_d[None, :]
    cos_d = jnp.concatenate([jnp.cos(ang_d), jnp.cos(ang_d)], axis=1)
    sin_d = jnp.concatenate([-jnp.sin(ang_d), jnp.sin(ang_d)], axis=1)
    return (cos_c, sin_lo, sin_hi), (cos_d, sin_d)


def _retention_tables(n_heads):
    lg = jnp.log(1.0 - 2.0 ** (-5.0 - jnp.arange(n_heads, dtype=F32)))
    idx = jnp.arange(CHUNK, dtype=F32)
    rel = idx[:, None] - idx[None, :]
    causal = idx[:, None] >= idx[None, :]
    dmask = jnp.exp(jnp.where(causal[None], rel[None] * lg[:, None, None], -jnp.inf))
    qdec = jnp.exp((idx[None, :] + 1.0) * lg[:, None])
    kdec = jnp.exp((CHUNK - 1.0 - idx[None, :]) * lg[:, None])
    cdec = jnp.exp(CHUNK * lg)
    bc = lambda t: jnp.broadcast_to(t[..., None], t.shape + (HEAD_DIM,))
    return dmask, bc(qdec), bc(kdec), bc(cdec[:, None])


def _row(v, width=LANE, offset=0):
    out = jnp.zeros((1, width), F32)
    return out.at[0, offset:offset + v.shape[0]].set(v.astype(F32))


def kernel(x, meta_tokens, emb_ln_g, emb_ln_b, w_in, conv_a, a_log, dt_bias, norm_a, hgrn_lb, norm_b, lam_q1, lam_k1,
           lam_q2, lam_k2, subln_c, w_out, ln1_g, ln1_b, w_rg, b_rg, w_re, b_re, w_gate, w_up, w_down, ln2_g, ln2_b):
    bsz, seq, d = x.shape
    depth = w_in.shape[0]
    n_meta = meta_tokens.shape[0]
    w = d // 4
    a_heads = w // HEAD_DIM
    t_valid = n_meta + seq
    front = (-t_valid) % CHUNK
    tpp = -(-(front + t_valid) // ROW_ALIGN) * ROW_ALIGN
    lay = (front, t_valid, tpp)
    alpha = (2 * depth) ** 0.25

    h, hb = _embed_ln(x, meta_tokens, emb_ln_g, emb_ln_b, lay)

    lbs = jax.nn.softmax(hgrn_lb.astype(F32), axis=0)
    lbs = jnp.cumsum(lbs, axis=0) - lbs[0:1]
    rope_c, rope_d = _rope_tables(front, tpp)
    dmask, qdec, kdec, cdec = _retention_tables(a_heads)

    w_in_t = jnp.swapaxes(w_in, 1, 2)

    for l in range(depth):
        lam_init = 0.8 - 0.6 * math.exp(-0.3 * l)
        lam = (jnp.exp(jnp.sum(lam_q1[l].astype(F32) * lam_k1[l].astype(F32)))
               - jnp.exp(jnp.sum(lam_q2[l].astype(F32) * lam_k2[l].astype(F32))) + lam_init)
        prm = {
            "conv_w": conv_a[l].astype(F32).reshape(3, w, -1).transpose(0, 2, 1),
            "a_log_row": _row(a_log[l], offset=a_heads),
            "dt_bias_row": _row(dt_bias[l], offset=a_heads),
            "norm_a": norm_a[l].astype(F32).reshape(1, HEAD_DIM),
            "lb": lbs[l].reshape(1, w),
            "norm_b": norm_b[l].astype(F32).reshape(1, HEAD_DIM),
            "rope_c": rope_c, "rope_d": rope_d,
            "lam_row": jnp.full((1, 2 * HEAD_DIM), lam, F32),
            "subln_c": subln_c[l].astype(F32).reshape(1, 2 * HEAD_DIM),
            "c_out_scale": 1.0 - lam_init,
            "ret_dmask": dmask, "ret_qdec": qdec, "ret_kdec": kdec, "ret_cdec": cdec,
        }
        z, zs = _in_proj(hb, w_in_t, l, w)
        o_parts = _mixers(z, zs, lay, bsz, w, prm)
        y = _out_proj(list(o_parts), w_out, l, (False, False, True, False), tpp)
        n_e = w_re.shape[2]
        w_r = jnp.concatenate([w_re[l], w_rg[l], jnp.zeros((d, LANE - n_e - N_GROUPS), F32)], axis=1).astype(F32)
        b_r = jnp.concatenate([b_re[l], b_rg[l], jnp.zeros((LANE - n_e - N_GROUPS,), F32)]).reshape(1, LANE)
        w_r_hi = w_r.astype(BF16)
        w_r_lo = (w_r - w_r_hi.astype(F32)).astype(BF16)
        h, hb, route = _add_ln(h, y, ln1_g[l], ln1_b[l], alpha, lay, router=(w_r_hi, w_r_lo, b_r))
        y2 = _moe_routed(h, route, w_gate, w_up, w_down, l, MOE_TILE)
        if l + 1 < depth:
            h, hb = _add_ln(h, y2, ln2_g[l], ln2_b[l], alpha, lay, route=route)

    return _final_ln(h, y2, route, ln2_g[depth - 1], ln2_b[depth - 1], alpha, lay, bsz, seq)
```

```python
import functools
import math

import jax
import jax.numpy as jnp
from jax import lax
from jax.experimental import pallas as pl
from jax.experimental.pallas import tpu as pltpu

HEAD_DIM = 128
CHUNK = 64
SUB = 16
GROUP_SHIFT = 3
ROPE_THETA = 500000.0
ROPE_DIMS = HEAD_DIM // 4
RET_THETA = 10000.0
N_GROUPS = 8
EXP_PER_GROUP = 8
NORM_EPS = 1e-6
LN_EPS = 1e-5
LANE = 128
SUBLANES = 8
ROW_ALIGN = 2 * CHUNK
V7X_VMEM_LIMIT_BYTES = 56 * 1024 * 1024
MOE_TILE = 256
NEG = -1e30

F32 = jnp.float32
BF16 = jnp.bfloat16
HI = lax.Precision.HIGHEST


def _cparams(sem):
    return pltpu.CompilerParams(dimension_semantics=sem, vmem_limit_bytes=V7X_VMEM_LIMIT_BYTES)


def _divisor_tile(n, cap, align):
    best = None
    for t in range(align, min(n, cap) + 1, align):
        if n % t == 0:
            best = t
    assert best is not None, (n, cap, align)
    return best


def _dot(a, b):
    return jnp.dot(a, b, preferred_element_type=F32)


def _dot_nt(a, b):
    return lax.dot_general(a, b, (((1,), (1,)), ((), ())), preferred_element_type=F32)


def _dot_tn(a, b):
    return lax.dot_general(a, b, (((0,), (0,)), ((), ())), preferred_element_type=F32)


def _sigmoid(x):
    return 1.0 / (1.0 + jnp.exp(-x))


def _silu(x):
    return x * _sigmoid(x)


def _softplus(x):
    return jnp.maximum(x, 0.0) + jnp.log(1.0 + jnp.exp(-jnp.abs(x)))


def _valid_rows(row0, n, front, t_valid):
    rb = row0 + lax.broadcasted_iota(jnp.int32, (n, 1), 0)
    return (rb >= front) & (rb < front + t_valid)


def _layer_norm_rows(x, g_ref, b_ref):
    mu = jnp.mean(x, axis=-1, keepdims=True)
    xc = x - mu
    var = jnp.mean(xc * xc, axis=-1, keepdims=True)
    return xc * lax.rsqrt(var + LN_EPS) * g_ref[...] + b_ref[...]


def _ln_kernel(*refs, alpha, mode, front, t_valid, tpp, tm):
    h_ref = refs[0]
    if mode == "experts":
        y0_ref, y1_ref, rt_ref, g_ref, b_ref, o_ref, ob_ref = refs[1:]
        x = alpha * h_ref[...] + rt_ref[:, 2:3] * y0_ref[...] + rt_ref[:, 3:4] * y1_ref[...]
    elif mode == "router":
        y_ref, g_ref, b_ref, wh_ref, wl_ref, br_ref, o_ref, ob_ref, ro_ref = refs[1:]
        x = alpha * h_ref[...] + y_ref[...].astype(F32)
    else:
        y_ref, g_ref, b_ref, o_ref, ob_ref = refs[1:]
        x = alpha * h_ref[...] + y_ref[...]
    y = _layer_norm_rows(x, g_ref, b_ref)
    row0 = (pl.program_id(0) * tm) % tpp
    y = jnp.where(_valid_rows(row0, tm, front, t_valid), y, 0.0)
    o_ref[...] = y
    yb = y.astype(BF16)
    ob_ref[...] = yb
    if mode == "router":
        yl = (y - yb.astype(F32)).astype(BF16)
        lg = _dot(yb, wh_ref[...]) + _dot(yl, wh_ref[...]) + _dot(yb, wl_ref[...]) + br_ref[...]
        ro_ref[...] = _route(lg)


def _add_ln(h, y, g, b, alpha, lay, route=None, router=None):
    front, t_valid, tpp = lay
    n, d = h.shape
    tm = _divisor_tile(tpp, 192, 16)
    spec = pl.BlockSpec((tm, d), lambda i: (i, 0))
    vec = pl.BlockSpec((1, d), lambda i: (0, 0))
    lane_rows = pl.BlockSpec((tm, LANE), lambda i: (i, 0))
    out_shape = [jax.ShapeDtypeStruct((n, d), F32), jax.ShapeDtypeStruct((n, d), BF16)]
    out_specs = [spec, spec]
    if route is not None:
        mode = "experts"
        args = (h, y, y, route, g.reshape(1, d), b.reshape(1, d))
        in_specs = [spec, spec, pl.BlockSpec((tm, d), lambda i: (i + n // tm, 0)), lane_rows, vec, vec]
    elif router is not None:
        mode = "router"
        wmat = pl.BlockSpec((d, LANE), lambda i: (0, 0))
        args = (h, y, g.reshape(1, d), b.reshape(1, d)) + tuple(router)
        in_specs = [spec, spec, vec, vec, wmat, wmat, pl.BlockSpec((1, LANE), lambda i: (0, 0))]
        out_shape.append(jax.ShapeDtypeStruct((n, LANE), F32))
        out_specs.append(lane_rows)
    else:
        mode = "plain"
        args = (h, y, g.reshape(1, d), b.reshape(1, d))
        in_specs = [spec, spec, vec, vec]
    kern = functools.partial(_ln_kernel, alpha=alpha, mode=mode, front=front, t_valid=t_valid, tpp=tpp, tm=tm)
    return pl.pallas_call(
        kern, out_shape=tuple(out_shape), grid=(n // tm,), in_specs=in_specs, out_specs=tuple(out_specs),
        compiler_params=_cparams(("parallel",)), name="add_ln_" + mode,
    )(*args)


def _embed_ln_kernel(x_ref, meta_ref, g_ref, b_ref, o_ref, ob_ref, *, front, n_meta, n_blocks):
    j = pl.program_id(1)

    @pl.when((j > 0) & (j <= n_blocks))
    def _():
        y = _layer_norm_rows(x_ref[...], g_ref, b_ref)
        o_ref[...] = y
        ob_ref[...] = y.astype(BF16)

    @pl.when(j == 0)
    def _():
        y = _layer_norm_rows(meta_ref[...], g_ref, b_ref)
        o_ref[...] = jnp.zeros_like(o_ref)
        ob_ref[...] = jnp.zeros_like(ob_ref)
        o_ref[front:front + n_meta, :] = y
        ob_ref[front:front + n_meta, :] = y.astype(BF16)

    @pl.when(j > n_blocks)
    def _():
        o_ref[...] = jnp.zeros_like(o_ref)
        ob_ref[...] = jnp.zeros_like(ob_ref)


def _embed_ln(x, meta_tokens, g, b, lay):
    front, t_valid, tpp = lay
    bsz, seq, d = x.shape
    n_meta = meta_tokens.shape[0]
    assert front + n_meta == CHUNK and seq % CHUNK == 0 and tpp == seq + 2 * CHUNK
    nb = seq // CHUNK
    blocks = tpp // CHUNK
    spec = pl.BlockSpec((CHUNK, d), lambda bb, j: (bb * blocks + j, 0))
    vec = pl.BlockSpec((1, d), lambda bb, j: (0, 0))
    return pl.pallas_call(
        functools.partial(_embed_ln_kernel, front=front, n_meta=n_meta, n_blocks=nb),
        out_shape=(jax.ShapeDtypeStruct((bsz * tpp, d), F32), jax.ShapeDtypeStruct((bsz * tpp, d), BF16)),
        grid=(bsz, blocks),
        in_specs=[pl.BlockSpec((None, CHUNK, d), lambda bb, j: (bb, jnp.clip(j - 1, 0, nb - 1), 0)),
                  pl.BlockSpec((n_meta, d), lambda bb, j: (0, 0)), vec, vec],
        out_specs=(spec, spec),
        compiler_params=_cparams(("parallel", "parallel")), name="embed_ln",
    )(x, meta_tokens.astype(x.dtype), g.reshape(1, d), b.reshape(1, d))


def _final_ln_kernel(h_ref, y0_ref, y1_ref, rt_ref, g_ref, b_ref, o_ref, *, alpha):
    x = alpha * h_ref[...] + rt_ref[:, 2:3] * y0_ref[...] + rt_ref[:, 3:4] * y1_ref[...]
    o_ref[...] = _layer_norm_rows(x, g_ref, b_ref)


def _final_ln(h, y2, route, g, b, alpha, lay, bsz, seq):
    front, t_valid, tpp = lay
    n, d = h.shape
    blocks = tpp // CHUNK
    first = (front + t_valid - seq) // CHUNK
    row = lambda bb, j: bb * blocks + first + j
    return pl.pallas_call(
        functools.partial(_final_ln_kernel, alpha=alpha),
        out_shape=jax.ShapeDtypeStruct((bsz, seq, d), F32),
        grid=(bsz, seq // CHUNK),
        in_specs=[pl.BlockSpec((CHUNK, d), lambda bb, j: (row(bb, j), 0)),
                  pl.BlockSpec((CHUNK, d), lambda bb, j: (row(bb, j), 0)),
                  pl.BlockSpec((CHUNK, d), lambda bb, j: (row(bb, j) + n // CHUNK, 0)),
                  pl.BlockSpec((CHUNK, LANE), lambda bb, j: (row(bb, j), 0)),
                  pl.BlockSpec((1, d), lambda bb, j: (0, 0)), pl.BlockSpec((1, d), lambda bb, j: (0, 0))],
        out_specs=pl.BlockSpec((None, CHUNK, d), lambda bb, j: (bb, j, 0)),
        compiler_params=_cparams(("parallel", "parallel")), name="final_ln",
    )(h, y2, y2, route, g.reshape(1, d), b.reshape(1, d))


def _out_proj_kernel(*refs, ks):
    a_refs = refs[:len(ks)]
    w_ref, o_ref, wb_ref = refs[len(ks):]

    @pl.when(pl.program_id(1) == 0)
    def _():
        step = min(w_ref.shape[0], 512)
        for r0 in range(0, w_ref.shape[0], step):
            wb_ref[r0:r0 + step, :] = w_ref[r0:r0 + step, :].astype(BF16)

    acc = None
    off = 0
    for a_ref, k in zip(a_refs, ks):
        p = _dot(a_ref[...], wb_ref[off:off + k, :])
        acc = p if acc is None else acc + p
        off += k
    o_ref[...] = acc.astype(o_ref.dtype)


def _out_proj(a_list, w_all, layer, wide, tpp):
    assert not wide[0]
    n = a_list[0].shape[0]
    _, kdim, ndim = w_all.shape
    ks = tuple(kdim // len(a_list) for _ in a_list)
    tm = _divisor_tile(tpp, 1408, 16)
    tn = _divisor_tile(ndim, 512, LANE)
    row_tiles = tpp // tm
    in_specs = [pl.BlockSpec((tm, k), (lambda j, i: (i % row_tiles, i // row_tiles)) if wd else (lambda j, i: (i, 0)))
                for k, wd in zip(ks, wide)]
    in_specs.append(pl.BlockSpec((None, kdim, tn), lambda j, i: (layer, 0, j)))
    return pl.pallas_call(
        functools.partial(_out_proj_kernel, ks=ks),
        out_shape=jax.ShapeDtypeStruct((n, ndim), BF16),
        grid=(ndim // tn, n // tm),
        in_specs=in_specs,
        out_specs=pl.BlockSpec((tm, tn), lambda j, i: (i, j)),
        scratch_shapes=[pltpu.VMEM((kdim, tn), BF16)],
        compiler_params=_cparams(("parallel", "arbitrary")),
        name="out_proj",
    )(*a_list, w_all)


def _in_proj_kernel(a_ref, w_ref, wn_ref, o_ref, wb_ref, *, n_plain, shift):
    j = pl.program_id(0)
    tn = w_ref.shape[0]
    step = min(tn, LANE)

    @pl.when(pl.program_id(1) == 0)
    def _():
        @pl.when(j < n_plain)
        def _():
            for r0 in range(0, tn, step):
                wb_ref[r0:r0 + step, :] = w_ref[r0:r0 + step, :].astype(BF16)

        @pl.when(j >= n_plain)
        def _():
            for r0 in range(0, tn - step, step):
                wb_ref[r0:r0 + step, :] = w_ref[r0 + shift:r0 + shift + step, :].astype(BF16)
            wb_ref[tn - step:tn - shift, :] = w_ref[tn - step + shift:, :].astype(BF16)
            wb_ref[tn - shift:, :] = wn_ref[:shift, :].astype(BF16)

    o_ref[...] = _dot_nt(a_ref[...], wb_ref[...])


def _small_proj_kernel(a_ref, w_ref, o_ref):
    o_ref[...] = _dot_nt(a_ref[...], w_ref[...].astype(BF16))


def _in_proj(hb, w_in_t, layer, w):
    n, d = hb.shape
    n_small = w_in_t.shape[1] - 15 * w
    tm = _divisor_tile(n, 1408, 16)
    tn = _divisor_tile(w, 512, LANE)
    z = pl.pallas_call(
        functools.partial(_in_proj_kernel, n_plain=4 * w // tn, shift=n_small),
        out_shape=jax.ShapeDtypeStruct((n, 15 * w), F32),
        grid=(15 * w // tn, n // tm),
        in_specs=[pl.BlockSpec((tm, d), lambda j, i: (i, 0)),
                  pl.BlockSpec((None, tn, d), lambda j, i: (layer, j, 0)),
                  pl.BlockSpec((None, LANE, d), lambda j, i: (layer, (j + 1) * (tn // LANE), 0))],
        out_specs=pl.BlockSpec((tm, tn), lambda j, i: (i, j)),
        scratch_shapes=[pltpu.VMEM((tn, d), BF16)],
        compiler_params=_cparams(("parallel", "arbitrary")),
        name="in_proj",
    )(hb, w_in_t, w_in_t)
    zs = pl.pallas_call(
        _small_proj_kernel,
        out_shape=jax.ShapeDtypeStruct((n, LANE), F32),
        grid=(n // tm,),
        in_specs=[pl.BlockSpec((tm, d), lambda i: (i, 0)),
                  pl.BlockSpec((None, LANE, d), lambda i: (layer, 4 * w // LANE, 0))],
        out_specs=pl.BlockSpec((tm, LANE), lambda i: (i, 0)),
        compiler_params=_cparams(("parallel",)),
        name="small_proj",
    )(hb, w_in_t)
    return z, zs


def _tri_masks(n):
    r = lax.broadcasted_iota(jnp.int32, (n, n), 0)
    c = lax.broadcasted_iota(jnp.int32, (n, n), 1)
    return r, c


def _gated_rms(o, w_row, gate):
    o = o * lax.rsqrt(jnp.mean(o * o, axis=-1, keepdims=True) + NORM_EPS)
    if w_row is not None:
        o = o * w_row
    return o * _silu(gate)


def _prefix_rows(x, row, seg):
    pos = row & (seg - 1)
    s = 1
    while s < seg:
        x = x + jnp.where(pos >= s, pltpu.roll(x, s, 0), 0.0)
        s *= 2
    return x


def _mixer_a_kernel(q_ref, k_ref, v_ref, gt_ref, sm_ref, cw_ref, alog_ref, dtb_ref, nw_ref, o_ref,
                    s_ref, prev_ref, *, n_batch, n_heads, front, t_valid):
    c = pl.program_id(0)

    @pl.when(c == 0)
    def _():
        s_ref[...] = jnp.zeros_like(s_ref)
        prev_ref[...] = jnp.zeros_like(prev_ref)

    row = lax.broadcasted_iota(jnp.int32, (CHUNK, 1), 0)
    row8 = lax.broadcasted_iota(jnp.int32, (SUBLANES, 1), 0)
    valid = _valid_rows(c * CHUNK, CHUNK, front, t_valid).astype(F32)
    r, cc = _tri_masks(CHUNK)
    causal = r >= cc
    strict = r > cc
    eye = (r == cc).astype(F32)

    def conv_silu(x_ref, bi, p):
        cur = x_ref[bi]
        prev8 = prev_ref[p, bi]
        acc = cur * cw_ref[p, 3:4, :]
        for s in (1, 2, 3):
            rolled = pltpu.roll(cur, s, 0)
            top = jnp.where(row8 >= s, rolled[:SUBLANES], pltpu.roll(prev8, s, 0))
            acc = acc + jnp.concatenate([top, rolled[SUBLANES:]], axis=0) * cw_ref[p, 3 - s:4 - s, :]
        prev_ref[p, bi] = cur[CHUNK - SUBLANES:]
        return _silu(acc)

    q_all, k_all, v_all, beta_all, gcum_all, gcum_t = [], [], [], [], [], []
    for bi in range(n_batch):
        q_all.append(conv_silu(q_ref, bi, 0))
        k_all.append(conv_silu(k_ref, bi, 1) * valid)
        v_all.append(conv_silu(v_ref, bi, 2) * valid)
        sm = sm_ref[bi]
        beta_all.append(_sigmoid(sm) * valid)
        g_all = -jnp.exp(alog_ref[...]) * _softplus(sm + dtb_ref[...]) * valid
        gcum_all.append(_prefix_rows(g_all, row, CHUNK))
        gcum_t.append(lax.dot_general(gcum_all[bi], eye, (((0,), (0,)), ((), ())), preferred_element_type=F32,
                                      precision=HI))

    chains = [(bi, h) for bi in range(n_batch) for h in range(n_heads)]
    ids = range(len(chains))
    hsl = [slice(h * HEAD_DIM, (h + 1) * HEAD_DIM) for _, h in chains]
    beta = [beta_all[bi][:, h:h + 1] for bi, h in chains]
    gcol = [gcum_all[bi][:, n_heads + h:n_heads + h + 1] for bi, h in chains]
    decay = [jnp.where(causal, jnp.exp(jnp.minimum(gcol[t] - gcum_t[bi][n_heads + h:n_heads + h + 1, :], 0.0)), 0.0)
             for t, (bi, h) in enumerate(chains)]
    eg = [jnp.exp(g) for g in gcol]
    g_last = [g[CHUNK - 1:CHUNK, :] for g in gcol]
    qh = [q_all[bi][:, hsl[t]] for t, (bi, _) in enumerate(chains)]
    kh = [k_all[bi][:, hsl[t]] for t, (bi, _) in enumerate(chains)]
    q = [x * lax.rsqrt(jnp.sum(x * x, axis=-1, keepdims=True) + NORM_EPS) * HEAD_DIM ** -0.5 for x in qh]
    k = [x * lax.rsqrt(jnp.sum(x * x, axis=-1, keepdims=True) + NORM_EPS) for x in kh]
    kb = [k[t] * beta[t] for t in ids]
    k16 = [x.astype(BF16) for x in k]
    m = [jnp.where(strict, -(_dot_nt(kb[t].astype(BF16), k16[t]) * decay[t]), 0.0) for t in ids]
    attn = [_dot_nt(q[t].astype(BF16), k16[t]) * decay[t] for t in ids]
    inv = [eye + x for x in m]
    mp = [_dot(x.astype(BF16), x.astype(BF16)) for x in m]
    for _ in range(4):
        both = [_dot(jnp.concatenate([inv[t], mp[t]], axis=0).astype(BF16), mp[t].astype(BF16)) for t in ids]
        inv = [inv[t] + both[t][:CHUNK] for t in ids]
        mp = [both[t][CHUNK:] for t in ids]
    inv = [inv[t] + _dot(inv[t].astype(BF16), mp[t].astype(BF16)) for t in ids]
    sol = [_dot(inv[t].astype(BF16),
                jnp.concatenate([v_all[bi][:, hsl[t]] * beta[t], kb[t] * eg[t]], axis=1).astype(BF16))
           for t, (bi, _) in enumerate(chains)]
    s_old = [s_ref[t] for t in ids]
    s16 = [x.astype(BF16) for x in s_old]
    v_new = [sol[t][:, :HEAD_DIM] - _dot(sol[t][:, HEAD_DIM:].astype(BF16), s16[t]) for t in ids]
    o = [_dot((q[t] * eg[t]).astype(BF16), s16[t]) + _dot(attn[t].astype(BF16), v_new[t].astype(BF16)) for t in ids]
    for t, (bi, _) in enumerate(chains):
        s_ref[t] = s_old[t] * jnp.exp(g_last[t]) + _dot_tn(k[t] * jnp.exp(g_last[t] - gcol[t]), v_new[t])
        o_ref[bi, :, hsl[t]] = _gated_rms(o[t], nw_ref[...], gt_ref[bi, :, hsl[t]]).astype(o_ref.dtype)


def _mixer_b_kernel(q_ref, f_ref, i_ref, gt_ref, lb_ref, nw_ref, o_ref, st_ref, kbuf, bbuf, obuf,
                    *, n_batch, n_heads, front, t_valid):
    c = pl.program_id(0)

    @pl.when(c == 0)
    def _():
        st_ref[...] = jnp.zeros_like(st_ref)

    row = lax.broadcasted_iota(jnp.int32, (CHUNK, 1), 0)
    valid = _valid_rows(c * CHUNK, CHUNK, front, t_valid).astype(F32)
    row16 = lax.broadcasted_iota(jnp.int32, (SUB, 1), 0)

    lb = lb_ref[...]
    for bi in range(n_batch):
        f = lb + (1.0 - lb) * _sigmoid(f_ref[bi])
        kbuf[bi] = (1.0 - f) * valid
        bbuf[bi] = _prefix_rows(jnp.log(f) * valid, row, SUB)

    chains = [(bi, h) for bi in range(n_batch) for h in range(n_heads)]
    hsl = [slice(h * HEAD_DIM, (h + 1) * HEAD_DIM) for _, h in chains]
    st = [st_ref[t] for t in range(len(chains))]
    for blk in range(CHUNK // SUB):
        rs = slice(blk * SUB, (blk + 1) * SUB)
        for t, (bi, _) in enumerate(chains):
            hs = hsl[t]
            b16 = bbuf[bi, rs, hs]
            q16 = _silu(q_ref[bi, rs, hs])
            b_end = b16[SUB - 1:SUB, :]
            acc = _dot_nt((q16 * jnp.exp(b16)).astype(BF16), st[t].astype(BF16))
            parts = []
            for t0 in range(0, SUB, SUBLANES):
                ts = slice(t0, t0 + SUBLANES)
                bq, qq, rowq = b16[ts], q16[ts], row16[ts]
                part = acc[ts]
                for j in range(min(t0 + SUBLANES, SUB)):
                    jr = pl.ds(blk * SUB + j, 1)
                    sc = jnp.sum(qq * kbuf[bi, jr, hs] * jnp.exp(bq - bbuf[bi, jr, hs]), axis=-1, keepdims=True)
                    if j > t0:
                        sc = jnp.where(rowq >= j, sc, 0.0)
                    part = part + sc * i_ref[bi, jr, hs]
                parts.append(part)
            acc = jnp.concatenate(parts, axis=0)
            st[t] = st[t] * jnp.exp(b_end) + _dot_tn(i_ref[bi, rs, hs], kbuf[bi, rs, hs] * jnp.exp(b_end - b16))
            obuf[bi, rs, hs] = acc
    for t, (bi, _) in enumerate(chains):
        st_ref[t] = st[t]
        o_ref[bi, :, hsl[t]] = _gated_rms(obuf[bi, :, hsl[t]], nw_ref[...], gt_ref[bi, :, hsl[t]]).astype(o_ref.dtype)


def _rope_partial(x, cos, sin_lo, sin_hi):
    half = ROPE_DIMS // 2
    return x * cos + pltpu.roll(x, LANE - half, 1) * sin_lo + pltpu.roll(x, half, 1) * sin_hi


def _rope_c_kernel(q_ref, k_ref, v_ref, cos_ref, slo_ref, shi_ref, qo_ref, ko_ref, vo_ref, *, n_maps):
    cos, slo, shi = cos_ref[...], slo_ref[...], shi_ref[...]
    for mp in range(n_maps):
        ms = slice(mp * HEAD_DIM, (mp + 1) * HEAD_DIM)
        qo_ref[:, ms] = (_rope_partial(q_ref[:, ms], cos, slo, shi) * HEAD_DIM ** -0.5).astype(BF16)
        ko_ref[:, ms] = _rope_partial(k_ref[:, ms], cos, slo, shi).astype(BF16)
    vo_ref[...] = v_ref[...].astype(BF16)


def _mixer_c_kernel(qi_ref, kj_ref, q_ref, k_ref, v_ref, lam_ref, nw_ref, o_ref, m_ref, l_ref, acc_ref,
                    *, tq, front, out_scale, n_heads):
    p = pl.program_id(0)
    i = qi_ref[p]
    j = kj_ref[p]
    vw = 2 * HEAD_DIM

    @pl.when(j == 0)
    def _():
        m_ref[...] = jnp.full_like(m_ref, NEG)
        l_ref[...] = jnp.zeros_like(l_ref)
        acc_ref[...] = jnp.zeros_like(acc_ref)

    def step(masked):
        if masked:
            rq = i * tq + lax.broadcasted_iota(jnp.int32, (tq, 1), 0)
            rk = j * tq + lax.broadcasted_iota(jnp.int32, (1, tq), 1)
            msk = (rk <= rq) & (rk >= front)
        for hd in range(n_heads):
            v = v_ref[:, hd * vw:(hd + 1) * vw]
            rows = [2 * hd, 2 * hd + 1]
            msl = [slice(r * HEAD_DIM, (r + 1) * HEAD_DIM) for r in rows]
            s = [_dot_nt(q_ref[:, ms], k_ref[:, ms]) for ms in msl]
            if masked:
                s = [jnp.where(msk, x, NEG) for x in s]
            m_prev = [m_ref[r] for r in rows]
            m_new = [jnp.maximum(m_prev[t], jnp.max(s[t], axis=-1, keepdims=True)) for t in range(2)]
            pr = [jnp.exp(s[t] - jnp.tile(m_new[t], (1, tq // LANE))) for t in range(2)]
            pv = [_dot(pr[t].astype(BF16), v) for t in range(2)]
            for t, r in enumerate(rows):
                a = jnp.exp(m_prev[t] - m_new[t])
                l_ref[r] = a * l_ref[r] + jnp.sum(pr[t], axis=-1, keepdims=True)
                acc_ref[r] = jnp.tile(a, (1, vw // LANE)) * acc_ref[r] + pv[t]
                m_ref[r] = m_new[t]

    needs_mask = (j == i) | (j == 0)

    @pl.when(needs_mask)
    def _():
        step(True)

    @pl.when(jnp.logical_not(needs_mask))
    def _():
        step(False)

    @pl.when(j == i)
    def _():
        reps = (1, vw // LANE)
        for hd in range(n_heads):
            r0, r1 = 2 * hd, 2 * hd + 1
            o = acc_ref[r0] / jnp.tile(l_ref[r0], reps) - lam_ref[...] * (acc_ref[r1] / jnp.tile(l_ref[r1], reps))
            o = o * lax.rsqrt(jnp.mean(o * o, axis=-1, keepdims=True) + NORM_EPS) * nw_ref[...] * out_scale
            o_ref[:, hd * vw:(hd + 1) * vw] = o.astype(o_ref.dtype)


def _mixer_d_kernel(q_ref, k_ref, v_ref, gt_ref, cos_ref, sin_ref, dm_ref, qd_ref, kd_ref, cd_ref, o_ref, s_ref,
                    *, n_batch, n_heads):
    c = pl.program_id(0)

    @pl.when(c == 0)
    def _():
        s_ref[...] = jnp.zeros_like(s_ref)

    cos = cos_ref[...]
    sin = sin_ref[...]
    chains = [(bi, h) for bi in range(n_batch) for h in range(n_heads)]
    ids = range(len(chains))
    hsl = [slice(h * HEAD_DIM, (h + 1) * HEAD_DIM) for _, h in chains]
    q = [q_ref[bi, :, hsl[t]] * cos + pltpu.roll(q_ref[bi, :, hsl[t]], HEAD_DIM // 2, 1) * sin
         for t, (bi, _) in enumerate(chains)]
    k = [(k_ref[bi, :, hsl[t]] * cos + pltpu.roll(k_ref[bi, :, hsl[t]], HEAD_DIM // 2, 1) * sin) * HEAD_DIM ** -0.5
         for t, (bi, _) in enumerate(chains)]
    s_old = [s_ref[t] for t in ids]
    a = [_dot_nt(q[t].astype(BF16), k[t].astype(BF16)) * dm_ref[h] for t, (_, h) in enumerate(chains)]
    o = [_dot(a[t].astype(BF16), v_ref[bi, :, hsl[t]].astype(BF16))
         + _dot((q[t] * qd_ref[h]).astype(BF16), s_old[t].astype(BF16)) for t, (bi, h) in enumerate(chains)]
    for t, (bi, h) in enumerate(chains):
        s_ref[t] = cd_ref[h] * s_old[t] + _dot_tn(k[t] * kd_ref[h], v_ref[bi, :, hsl[t]])
        o_ref[bi, :, hsl[t]] = _gated_rms(o[t], None, gt_ref[bi, :, hsl[t]]).astype(o_ref.dtype)


def _route(lg):
    n_e = N_GROUPS * EXP_PER_GROUP
    lane_i = lax.broadcasted_iota(jnp.int32, lg.shape, 1)
    lane = lane_i.astype(F32)
    grp = (lane_i >> GROUP_SHIFT).astype(F32)
    big = float(4 * LANE)
    is_g = (lane_i >= n_e) & (lane_i < n_e + N_GROUPS)
    gl = jnp.where(is_g, lg, NEG)
    gmax = jnp.max(gl, axis=-1, keepdims=True)
    gsum = jnp.sum(jnp.where(is_g, jnp.exp(gl - gmax), 0.0), axis=-1, keepdims=True)
    g_w = 1.0 / gsum
    g_idx = jnp.min(jnp.where(is_g & (gl == gmax), lane - n_e, big), axis=-1, keepdims=True)
    in_grp = (lane_i < n_e) & (grp == g_idx)
    el = jnp.where(in_grp, lg, NEG)
    emax = jnp.max(el, axis=-1, keepdims=True)
    eexp = jnp.where(in_grp, jnp.exp(el - emax), 0.0)
    pe = eexp / jnp.sum(eexp, axis=-1, keepdims=True)
    p1 = jnp.max(jnp.where(in_grp, pe, -1.0), axis=-1, keepdims=True)
    i1 = jnp.min(jnp.where(in_grp & (pe == p1), lane, big), axis=-1, keepdims=True)
    rest = in_grp & (lane != i1)
    p2 = jnp.max(jnp.where(rest, pe, -1.0), axis=-1, keepdims=True)
    i2 = jnp.min(jnp.where(rest & (pe == p2), lane, big), axis=-1, keepdims=True)
    den = p1 + p2
    return (jnp.where(lane_i == 0, i1, 0.0) + jnp.where(lane_i == 1, i2, 0.0)
            + jnp.where(lane_i == 2, g_w * (p1 / den), 0.0) + jnp.where(lane_i == 3, g_w * (p2 / den), 0.0))


def _row_copy(src, src_row, dst, dst_row, sem):
    return pltpu.make_async_copy(src.at[pl.ds(src_row, 1)], dst.at[pl.ds(dst_row, 1)], sem)


def _moe_kernel(te_ref, cnt_ref, first_ref, src_ref, srcn_ref, dst_ref, x_hbm, wg_ref, wu_ref, wd_ref, y_hbm,
                xbuf, ybuf, wgb, wub, wdb, sem_in, sem_out, *, tm, n_tiles):
    del te_ref
    t = pl.program_id(0)
    slot = t % 2
    cnt = cnt_ref[t]

    def pairs(rows):
        return (rows + 1) // 2

    def gather_start(idx_ref, s, rows):
        def body(g, c):
            for u in range(2):
                r = 2 * g + u
                _row_copy(x_hbm, idx_ref[0, 0, r], xbuf.at[s], r, sem_in.at[s]).start(priority=u)
            return c
        lax.fori_loop(0, pairs(rows), body, 0)

    def wait_rows(src, dst, sem, rows):
        def tile_body(g, c):
            pltpu.make_async_copy(src.at[pl.ds(0, SUBLANES)], dst.at[pl.ds(g * SUBLANES, SUBLANES)], sem).wait()
            return c
        lax.fori_loop(0, rows // SUBLANES, tile_body, 0)

        def row_body(r, c):
            _row_copy(src, 0, dst, r, sem).wait()
            return c
        lax.fori_loop(rows - rows % SUBLANES, rows, row_body, 0)

    def gather_wait(s, rows):
        wait_rows(x_hbm, xbuf.at[s], sem_in.at[s], 2 * pairs(rows))

    def scatter_start(s, rows):
        def body(g, c):
            for u in range(2):
                r = 2 * g + u
                _row_copy(ybuf.at[s], r, y_hbm, dst_ref[0, 0, r], sem_out.at[s]).start(priority=u)
            return c
        lax.fori_loop(0, rows // 2, body, 0)

        @pl.when(rows % 2 == 1)
        def _():
            _row_copy(ybuf.at[s], rows - 1, y_hbm, dst_ref[0, 0, rows - 1], sem_out.at[s]).start()

    def scatter_wait(s, rows):
        wait_rows(ybuf.at[s], y_hbm, sem_out.at[s], rows)

    @pl.when(t == 0)
    def _():
        xbuf[...] = jnp.zeros_like(xbuf)

        @pl.when(cnt > 0)
        def _():
            gather_start(src_ref, 0, cnt)

    nxt = jnp.minimum(t + 1, n_tiles - 1)

    @pl.when((t + 1 < n_tiles) & (cnt_ref[nxt] > 0))
    def _():
        gather_start(srcn_ref, 1 - slot, cnt_ref[nxt])

    old = jnp.maximum(t - 2, 0)

    @pl.when((t >= 2) & (cnt_ref[old] > 0))
    def _():
        scatter_wait(slot, cnt_ref[old])

    @pl.when(cnt > 0)
    def _():
        @pl.when(first_ref[t] == 1)
        def _():
            wgb[...] = wg_ref[...].astype(BF16)
            wub[...] = wu_ref[...].astype(BF16)
            wdb[...] = wd_ref[...].astype(BF16)

        gather_wait(slot, cnt)

        def expert_mlp(rows):
            x = xbuf[slot, :rows].astype(BF16)
            hg = _dot(x, wgb[...])
            hu = _dot(x, wub[...])
            act = (_silu(hg) * hu).astype(BF16)
            ybuf[slot, :rows] = _dot(act, wdb[...])

        @pl.when(cnt <= tm // 2)
        def _():
            expert_mlp(tm // 2)

        @pl.when(cnt > tm // 2)
        def _():
            expert_mlp(tm)

        scatter_start(slot, cnt)

    @pl.when(t == n_tiles - 1)
    def _():
        prev = jnp.maximum(t - 1, 0)

        @pl.when((t >= 1) & (cnt_ref[prev] > 0))
        def _():
            scatter_wait(1 - slot, cnt_ref[prev])

        @pl.when(cnt > 0)
        def _():
            scatter_wait(slot, cnt)


def _dispatch(route, n, tm, n_e):
    ids = jnp.concatenate([route[:, 0], route[:, 1]]).astype(jnp.int32)
    n_pairs = 2 * n
    n_tiles = n_pairs // tm + n_e
    order = jnp.argsort(ids, stable=True).astype(jnp.int32)
    counts = jnp.sum((ids[:, None] == jnp.arange(n_e, dtype=jnp.int32)[None, :]).astype(jnp.int32), axis=0)
    tiles_e = (counts + tm - 1) // tm
    tile_end = jnp.cumsum(tiles_e)
    tile_start = tile_end - tiles_e
    first = jnp.cumsum(counts) - counts
    tile = jnp.arange(n_tiles, dtype=jnp.int32)
    te = jnp.minimum(jnp.searchsorted(tile_end, tile, side="right").astype(jnp.int32), n_e - 1)
    cnt = jnp.clip(counts[te] - (tile - tile_start[te]) * tm, 0, tm)
    cnt = jnp.where(tile < tile_end[n_e - 1], cnt, 0).astype(jnp.int32)
    row = jnp.arange(tm, dtype=jnp.int32)[None, :]
    spos = (first[te] + (tile - tile_start[te]) * tm)[:, None] + row
    dst = jnp.where(row < cnt[:, None], order[jnp.clip(spos, 0, n_pairs - 1)], 0)
    src = dst % n
    te = jnp.where(cnt > 0, te, jnp.max(jnp.where(cnt > 0, te, 0)))
    new_expert = jnp.concatenate([jnp.ones((1,), jnp.int32), (te[1:] != te[:-1]).astype(jnp.int32)])
    return te, cnt, new_expert, src.reshape(n_tiles, 1, tm), dst.reshape(n_tiles, 1, tm)


def _moe_routed(h, route, wg, wu, wd, layer, tm):
    n, d = h.shape
    _, n_e, _, f = wg.shape
    te, cnt, new_expert, src, dst = _dispatch(route, n, tm, n_e)
    n_tiles = te.shape[0]

    def idx_spec(shift):
        return pl.BlockSpec((1, 1, tm), lambda t, *_: (jnp.minimum(t + shift, n_tiles - 1), 0, 0),
                            memory_space=pltpu.SMEM)

    grid_spec = pltpu.PrefetchScalarGridSpec(
        num_scalar_prefetch=3, grid=(n_tiles,),
        in_specs=[idx_spec(0), idx_spec(1), idx_spec(0),
                  pl.BlockSpec(memory_space=pl.ANY),
                  pl.BlockSpec((None, None, d, f), lambda t, te, *_: (layer, te[t], 0, 0)),
                  pl.BlockSpec((None, None, d, f), lambda t, te, *_: (layer, te[t], 0, 0)),
                  pl.BlockSpec((None, None, f, d), lambda t, te, *_: (layer, te[t], 0, 0))],
        out_specs=pl.BlockSpec(memory_space=pl.ANY),
        scratch_shapes=[pltpu.VMEM((2, tm, d), F32), pltpu.VMEM((2, tm, d), F32),
                        pltpu.VMEM((d, f), BF16), pltpu.VMEM((d, f), BF16), pltpu.VMEM((f, d), BF16),
                        pltpu.SemaphoreType.DMA((2,)), pltpu.SemaphoreType.DMA((2,))])
    return pl.pallas_call(
        functools.partial(_moe_kernel, tm=tm, n_tiles=n_tiles),
        out_shape=jax.ShapeDtypeStruct((2 * n, d), F32),
        grid_spec=grid_spec,
        compiler_params=_cparams(("arbitrary",)),
        name="moe_routed",
    )(te, cnt, new_expert, src, src, dst, h, wg, wu, wd)


def _mixers(z, zs, lay, bsz, w, prm):
    front, t_valid, tpp = lay
    n = z.shape[0]
    nc = tpp // CHUNK
    heads = w // HEAD_DIM
    c_heads = w // (2 * HEAD_DIM)
    z3 = z.reshape(bsz, tpp, z.shape[1])
    zs3 = zs.reshape(bsz, tpp, LANE)

    def cspec(region):
        return pl.BlockSpec((bsz, CHUNK, w), lambda c, region=region: (0, c, region))

    def fixed(shape):
        nd = len(shape)
        return pl.BlockSpec(shape, lambda c: (0,) * nd)

    ospec = pl.BlockSpec((bsz, CHUNK, w), lambda c: (0, c, 0))
    oshape3 = jax.ShapeDtypeStruct((bsz, tpp, w), BF16)
    oshape = jax.ShapeDtypeStruct((n, w), BF16)
    grid = (nc,)
    sem = ("arbitrary",)
    state = pltpu.VMEM((bsz * heads, HEAD_DIM, HEAD_DIM), F32)

    oa = pl.pallas_call(
        functools.partial(_mixer_a_kernel, n_batch=bsz, n_heads=heads, front=front, t_valid=t_valid),
        out_shape=oshape3, grid=grid,
        in_specs=[cspec(0), cspec(1), cspec(2), cspec(3),
                  pl.BlockSpec((bsz, CHUNK, LANE), lambda c: (0, c, 0)),
                  fixed((3, 4, w)), fixed((1, LANE)), fixed((1, LANE)), fixed((1, HEAD_DIM))],
        out_specs=ospec,
        scratch_shapes=[state, pltpu.VMEM((3, bsz, SUBLANES, w), F32)],
        compiler_params=_cparams(sem), name="mixer_a",
    )(z3, z3, z3, z3, zs3, prm["conv_w"], prm["a_log_row"], prm["dt_bias_row"], prm["norm_a"]).reshape(n, w)

    ob = pl.pallas_call(
        functools.partial(_mixer_b_kernel, n_batch=bsz, n_heads=heads, front=front, t_valid=t_valid),
        out_shape=oshape3, grid=grid,
        in_specs=[cspec(4), cspec(5), cspec(6), cspec(7), fixed((1, w)), fixed((1, HEAD_DIM))],
        out_specs=ospec,
        scratch_shapes=[state] + [pltpu.VMEM((bsz, CHUNK, w), F32)] * 3,
        compiler_params=_cparams(sem), name="mixer_b",
    )(z3, z3, z3, z3, prm["lb"], prm["norm_b"]).reshape(n, w)

    tq = _divisor_tile(tpp, 384, LANE)
    nq = tpp // tq
    vw = 2 * HEAD_DIM
    rc = prm["rope_c"]
    wide = jax.ShapeDtypeStruct((tpp, bsz * w), BF16)
    rspec = pl.BlockSpec((tq, w), lambda i: (i % nq, i // nq))
    tab = pl.BlockSpec((tq, HEAD_DIM), lambda i: (i % nq, 0))
    qr, kr, vr = pl.pallas_call(
        functools.partial(_rope_c_kernel, n_maps=heads),
        out_shape=(wide, wide, wide), grid=(n // tq,),
        in_specs=[pl.BlockSpec((tq, w), lambda i: (i, 8)), pl.BlockSpec((tq, w), lambda i: (i, 9)),
                  pl.BlockSpec((tq, w), lambda i: (i, 10)), tab, tab, tab],
        out_specs=(rspec, rspec, rspec),
        compiler_params=_cparams(("parallel",)), name="rope_c",
    )(z, z, z, rc[0], rc[1], rc[2])

    pairs = [(i, j) for i in range(nq) for j in range(i + 1)]
    qi = jnp.asarray([p[0] for p in pairs], jnp.int32)
    kj = jnp.asarray([p[1] for p in pairs], jnp.int32)
    qspec = pl.BlockSpec((tq, bsz * w), lambda p, qi, kj: (qi[p], 0))
    kspec = pl.BlockSpec((tq, bsz * w), lambda p, qi, kj: (kj[p], 0))
    row_vw = pl.BlockSpec((1, vw), lambda p, qi, kj: (0, 0))
    n_chains = 2 * bsz * c_heads
    oc = pl.pallas_call(
        functools.partial(_mixer_c_kernel, tq=tq, front=front, out_scale=prm["c_out_scale"], n_heads=bsz * c_heads),
        out_shape=wide,
        grid_spec=pltpu.PrefetchScalarGridSpec(
            num_scalar_prefetch=2, grid=(len(pairs),),
            in_specs=[qspec, kspec, kspec, row_vw, row_vw],
            out_specs=qspec,
            scratch_shapes=[pltpu.VMEM((n_chains, tq, LANE), F32), pltpu.VMEM((n_chains, tq, LANE), F32),
                            pltpu.VMEM((n_chains, tq, vw), F32)]),
        compiler_params=_cparams(("arbitrary",)), name="mixer_c",
    )(qi, kj, qr, kr, vr, prm["lam_row"], prm["subln_c"])

    tabd = pl.BlockSpec((CHUNK, HEAD_DIM), lambda c: (c, 0))
    rd = prm["rope_d"]
    od = pl.pallas_call(
        functools.partial(_mixer_d_kernel, n_batch=bsz, n_heads=heads),
        out_shape=oshape3, grid=grid,
        in_specs=[cspec(11), cspec(12), cspec(13), cspec(14), tabd, tabd,
                  fixed((heads, CHUNK, CHUNK)), fixed((heads, CHUNK, HEAD_DIM)), fixed((heads, CHUNK, HEAD_DIM)),
                  fixed((heads, 1, HEAD_DIM))],
        out_specs=ospec,
        scratch_shapes=[state],
        compiler_params=_cparams(sem), name="mixer_d",
    )(z3, z3, z3, z3, rd[0], rd[1], prm["ret_dmask"], prm["ret_qdec"], prm["ret_kdec"], prm["ret_cdec"]).reshape(n, w)
    return oa, ob, oc, od


def _rope_tables(front, tpp):
    pos = (jnp.arange(tpp) - front).astype(F32)[:, None]
    half = ROPE_DIMS // 2
    inv = 1.0 / (ROPE_THETA ** (jnp.arange(half, dtype=F32) / half))
    ang = pos * inv[None, :]
    z_rest = jnp.zeros((tpp, HEAD_DIM - ROPE_DIMS), F32)
    z_half = jnp.zeros((tpp, half), F32)
    cos_c = jnp.concatenate([jnp.cos(ang), jnp.cos(ang), jnp.ones_like(z_rest)], axis=1)
    sin_lo = jnp.concatenate([-jnp.sin(ang), z_half, z_rest], axis=1)
    sin_hi = jnp.concatenate([z_half, jnp.sin(ang), z_rest], axis=1)
    half_d = HEAD_DIM // 2
    inv_d = 1.0 / (RET_THETA ** (jnp.arange(half_d, dtype=F32) / half_d))
    ang_d = pos * inv_d[None, :]
    cos_d = jnp.concatenate([jnp.cos(ang_d), jnp.cos(ang_d)], axis=1)
    sin_d = jnp.concatenate([-jnp.sin(ang_d), jnp.sin(ang_d)], axis=1)
    return (cos_c, sin_lo, sin_hi), (cos_d, sin_d)


def _retention_tables(n_heads):
    lg = jnp.log(1.0 - 2.0 ** (-5.0 - jnp.arange(n_heads, dtype=F32)))
    idx = jnp.arange(CHUNK, dtype=F32)
    rel = idx[:, None] - idx[None, :]
    causal = idx[:, None] >= idx[None, :]
    dmask = jnp.exp(jnp.where(causal[None], rel[None] * lg[:, None, None], -jnp.inf))
    qdec = jnp.exp((idx[None, :] + 1.0) * lg[:, None])
    kdec = jnp.exp((CHUNK - 1.0 - idx[None, :]) * lg[:, None])
    cdec = jnp.exp(CHUNK * lg)
    bc = lambda t: jnp.broadcast_to(t[..., None], t.shape + (HEAD_DIM,))
    return dmask, bc(qdec), bc(kdec), bc(cdec[:, None])


def _row(v, width=LANE, offset=0):
    out = jnp.zeros((1, width), F32)
    return out.at[0, offset:offset + v.shape[0]].set(v.astype(F32))


def kernel(x, meta_tokens, emb_ln_g, emb_ln_b, w_in, conv_a, a_log, dt_bias, norm_a, hgrn_lb, norm_b, lam_q1, lam_k1,
           lam_q2, lam_k2, subln_c, w_out, ln1_g, ln1_b, w_rg, b_rg, w_re, b_re, w_gate, w_up, w_down, ln2_g, ln2_b):
    bsz, seq, d = x.shape
    depth = w_in.shape[0]
    n_meta = meta_tokens.shape[0]
    w = d // 4
    a_heads = w // HEAD_DIM
    t_valid = n_meta + seq
    front = (-t_valid) % CHUNK
    tpp = -(-(front + t_valid) // ROW_ALIGN) * ROW_ALIGN
    lay = (front, t_valid, tpp)
    alpha = (2 * depth) ** 0.25

    h, hb = _embed_ln(x, meta_tokens, emb_ln_g, emb_ln_b, lay)

    lbs = jax.nn.softmax(hgrn_lb.astype(F32), axis=0)
    lbs = jnp.cumsum(lbs, axis=0) - lbs[0:1]
    rope_c, rope_d = _rope_tables(front, tpp)
    dmask, qdec, kdec, cdec = _retention_tables(a_heads)

    w_in_t = jnp.swapaxes(w_in, 1, 2)

    for l in range(depth):
        lam_init = 0.8 - 0.6 * math.exp(-0.3 * l)
        lam = (jnp.exp(jnp.sum(lam_q1[l].astype(F32) * lam_k1[l].astype(F32)))
               - jnp.exp(jnp.sum(lam_q2[l].astype(F32) * lam_k2[l].astype(F32))) + lam_init)
        prm = {
            "conv_w": conv_a[l].astype(F32).reshape(3, w, -1).transpose(0, 2, 1),
            "a_log_row": _row(a_log[l], offset=a_heads),
            "dt_bias_row": _row(dt_bias[l], offset=a_heads),
            "norm_a": norm_a[l].astype(F32).reshape(1, HEAD_DIM),
            "lb": lbs[l].reshape(1, w),
            "norm_b": norm_b[l].astype(F32).reshape(1, HEAD_DIM),
            "rope_c": rope_c, "rope_d": rope_d,
            "lam_row": jnp.full((1, 2 * HEAD_DIM), lam, F32),
            "subln_c": subln_c[l].astype(F32).reshape(1, 2 * HEAD_DIM),
            "c_out_scale": 1.0 - lam_init,
            "ret_dmask": dmask, "ret_qdec": qdec, "ret_kdec": kdec, "ret_cdec": cdec,
        }
        z, zs = _in_proj(hb, w_in_t, l, w)
        o_parts = _mixers(z, zs, lay, bsz, w, prm)
        y = _out_proj(list(o_parts), w_out, l, (False, False, True, False), tpp)
        n_e = w_re.shape[2]
        w_r = jnp.concatenate([w_re[l], w_rg[l], jnp.zeros((d, LANE - n_e - N_GROUPS), F32)], axis=1).astype(F32)
        b_r = jnp.concatenate([b_re[l], b_rg[l], jnp.zeros((LANE - n_e - N_GROUPS,), F32)]).reshape(1, LANE)
        w_r_hi = w_r.astype(BF16)
        w_r_lo = (w_r - w_r_hi.astype(F32)).astype(BF16)
        h, hb, route = _add_ln(h, y, ln1_g[l], ln1_b[l], alpha, lay, router=(w_r_hi, w_r_lo, b_r))
        y2 = _moe_routed(h, route, w_gate, w_up, w_down, l, MOE_TILE)
        if l + 1 < depth:
            h, hb = _add_ln(h, y2, ln2_g[l], ln2_b[l], alpha, lay, route=route)

    return _final_ln(h, y2, route, ln2_g[depth - 1], ln2_b[depth - 1], alpha, lay, bsz, seq)
```
